```python
import math
import jax, jax.numpy as jnp
from jax import lax
import numpy as np

D_MODEL = 1024
BATCH = 8
SEQ = 4096
DEPTH = 1
DEC_BATCH = 8
DEC_SEQ = 32
PAST_LEN = 1024

CHUNK = 64
N_META = 16
H_ATT = 4
D_QK = 64
D_V = 2 * D_QK
W_ATT = H_ATT * D_V
W_CONV = D_MODEL - W_ATT
CONV_W = 3
N_SPLIT = 8
W_IN = W_ATT * 4 + W_CONV * 4
N_BUCKETS = 32
MAX_DIST = 128
Q_BLOCK = 128
EPS = 1e-6
NEG = -1e30
SCALE = D_QK ** -0.5

kernel_name = "hybrid_diffattn_shortconv_stream_step"

f32 = jnp.float32


def rmsnorm(x, g):
    xf = x.astype(f32)
    y = xf * lax.rsqrt(jnp.mean(xf * xf, axis=-1, keepdims=True) + EPS)
    return (y * g.astype(f32)).astype(x.dtype)


def rel_bucket(rel):
    nb = N_BUCKETS // 2
    ret = jnp.where(rel > 0, nb, 0)
    n = jnp.abs(rel)
    max_exact = nb // 2
    nf = jnp.maximum(n, 1).astype(f32)
    large = max_exact + (jnp.log(nf / max_exact) / math.log(MAX_DIST / max_exact)
                         * (nb - max_exact)).astype(jnp.int32)
    large = jnp.minimum(large, nb - 1)
    return ret + jnp.where(n < max_exact, n, large)


def chunk_id(pos):
    return jnp.where(pos < 0, -1, pos // CHUNK)


def split_proj(n, w):
    p = n @ w
    b, l = p.shape[:2]
    q, k, v, z_a, b_g, c_g, u, z_c = jnp.split(p, N_SPLIT, axis=-1)
    q = q.reshape(b, l, H_ATT, 2, D_QK)
    k = k.reshape(b, l, H_ATT, 2, D_QK)
    v = v.reshape(b, l, H_ATT, D_V)
    return q, k, v, z_a, b_g, c_g, u, z_c


def diff_lambda(lq1, lk1, lq2, lk2, lam_init):
    return (jnp.exp(jnp.sum(lq1.astype(f32) * lk1.astype(f32)))
            - jnp.exp(jnp.sum(lq2.astype(f32) * lk2.astype(f32))) + lam_init)


def diff_attend(q, k, v, q_pos, k_pos, lam, rel_bias):
    s = jnp.einsum('bqhcd,bkhcd->bchqk', q, k, preferred_element_type=f32) * SCALE
    bias = rel_bias[rel_bucket(k_pos[None, :] - q_pos[:, None])]
    s = s + jnp.transpose(bias, (2, 0, 1)).astype(f32)[None, None]
    vis = chunk_id(k_pos)[None, :] <= chunk_id(q_pos)[:, None]
    s = jnp.where(vis, s, NEG)
    p = jax.nn.softmax(s, axis=-1)
    a = p[:, 0] - lam * p[:, 1]
    return jnp.einsum('bhqk,bkhd->bqhd', a.astype(v.dtype), v)


def short_conv(u_pad, w):
    l = u_pad.shape[1] - (CONV_W - 1)
    out = w[0] * u_pad[:, 0:l]
    for j in range(1, CONV_W):
        out = out + w[j] * u_pad[:, j:j + l]
    return out


def merge_branches(o, sub_g, lam_init, z_a, conv_out, b_g, z_c, w_o):
    o = rmsnorm(o, sub_g) * (1.0 - lam_init)
    o = o.reshape(*o.shape[:2], W_ATT) * jax.nn.silu(z_a)
    y_c = jax.nn.silu(z_c) * b_g * conv_out
    return jnp.concatenate([o, y_c], axis=-1) @ w_o


def setup_inputs(seed: int = 0) -> dict:
    key = jax.random.key(seed)
    ks = jax.random.split(key, 17)
    nrm = jax.random.normal
    return {
        "x_prompt": nrm(ks[0], (BATCH, SEQ, D_MODEL), f32),
        "x_sample": nrm(ks[1], (DEC_BATCH, DEC_SEQ, D_MODEL), f32),
        "cache_k": nrm(ks[2], (DEPTH, DEC_BATCH, N_META + PAST_LEN, H_ATT, 2 * D_QK), f32),
        "cache_v": nrm(ks[3], (DEPTH, DEC_BATCH, N_META + PAST_LEN, H_ATT, D_V), f32),
        "state_conv": nrm(ks[4], (DEPTH, DEC_BATCH, CONV_W - 1, W_CONV), f32),
        "meta_tokens": nrm(ks[5], (N_META, D_MODEL), f32),
        "rel_bias": 0.1 * nrm(ks[6], (N_BUCKETS, H_ATT), f32),
        "norm_g": 1.0 + 0.1 * nrm(ks[7], (DEPTH, D_MODEL), f32),
        "w_in": nrm(ks[8], (DEPTH, D_MODEL, W_IN), f32) * D_MODEL ** -0.5,
        "conv_w": nrm(ks[9], (DEPTH, CONV_W, W_CONV), f32) * CONV_W ** -0.5,
        "lambda_q1": 0.1 * nrm(ks[10], (DEPTH, D_QK), f32),
        "lambda_k1": 0.1 * nrm(ks[11], (DEPTH, D_QK), f32),
        "lambda_q2": 0.1 * nrm(ks[12], (DEPTH, D_QK), f32),
        "lambda_k2": 0.1 * nrm(ks[13], (DEPTH, D_QK), f32),
        "subln_g": 1.0 + 0.1 * nrm(ks[14], (DEPTH, D_V), f32),
        "w_out": nrm(ks[15], (DEPTH, W_ATT + W_CONV, D_MODEL), f32) * (W_ATT + W_CONV) ** -0.5,
        "final_g": 1.0 + 0.1 * nrm(ks[16], (D_MODEL,), f32),
    }


def reference(x_prompt, x_sample, cache_k, cache_v, state_conv, meta_tokens, rel_bias,
              norm_g, w_in, conv_w, lambda_q1, lambda_k1, lambda_q2, lambda_k2,
              subln_g, w_out, final_g):
    bp, sp = x_prompt.shape[:2]
    bd, sd = x_sample.shape[:2]
    past = cache_k.shape[2] - N_META
    n_blk = sp // Q_BLOCK

    meta_pos = jnp.arange(-N_META, 0, dtype=jnp.int32)
    p_qpos = jnp.arange(sp, dtype=jnp.int32)
    p_kpos = jnp.concatenate([meta_pos, p_qpos])
    d_qpos = past + jnp.arange(sd, dtype=jnp.int32)
    d_kpos = jnp.concatenate([meta_pos, jnp.arange(past, dtype=jnp.int32), d_qpos])

    hp = jnp.concatenate(
        [jnp.broadcast_to(meta_tokens[None].astype(x_prompt.dtype), (bp, N_META, D_MODEL)), x_prompt],
        axis=1)
    hd = x_sample

    kp_l, vp_l, cp_l, kd_l, vd_l, cd_l = [], [], [], [], [], []
    for l in range(DEPTH):
        lam_init = 0.8 - 0.6 * math.exp(-0.3 * l)
        lam = diff_lambda(lambda_q1[l], lambda_k1[l], lambda_q2[l], lambda_k2[l], lam_init)

        q, k, v, z_a, b_g, c_g, u, z_c = split_proj(rmsnorm(hp, norm_g[l]), w_in[l])
        o_meta = diff_attend(q[:, :N_META], k[:, :N_META], v[:, :N_META],
                             meta_pos, meta_pos, lam, rel_bias)
        q_blk = jnp.transpose(q[:, N_META:].reshape(bp, n_blk, Q_BLOCK, H_ATT, 2, D_QK),
                              (1, 0, 2, 3, 4, 5))
        pos_blk = p_qpos.reshape(n_blk, Q_BLOCK)
        o_frm = lax.map(lambda a: diff_attend(a[0], k, v, a[1], p_kpos, lam, rel_bias),
                        (q_blk, pos_blk))
        o_frm = jnp.transpose(o_frm, (1, 0, 2, 3, 4)).reshape(bp, sp, H_ATT, D_V)
        o = jnp.concatenate([o_meta, o_frm], axis=1)
        u_pad = jnp.concatenate([jnp.zeros((bp, CONV_W - 1, W_CONV), u.dtype), c_g * u], axis=1)
        hp = hp + merge_branches(o, subln_g[l], lam_init, z_a,
                                 short_conv(u_pad, conv_w[l]), b_g, z_c, w_out[l])
        kp_l.append(k.reshape(bp, N_META + sp, H_ATT, 2 * D_QK))
        vp_l.append(v)
        cp_l.append(u_pad[:, -(CONV_W - 1):])

        q2, k2, v2, z_a2, b_g2, c_g2, u2, z_c2 = split_proj(rmsnorm(hd, norm_g[l]), w_in[l])
        k_all = jnp.concatenate(
            [cache_k[l].reshape(bd, N_META + past, H_ATT, 2, D_QK).astype(k2.dtype), k2], axis=1)
        v_all = jnp.concatenate([cache_v[l].astype(v2.dtype), v2], axis=1)
        o2 = diff_attend(q2, k_all, v_all, d_qpos, d_kpos, lam, rel_bias)
        u2_pad = jnp.concatenate([state_conv[l].astype(u2.dtype), c_g2 * u2], axis=1)
        hd = hd + merge_branches(o2, subln_g[l], lam_init, z_a2,
                                 short_conv(u2_pad, conv_w[l]), b_g2, z_c2, w_out[l])
        kd_l.append(k2.reshape(bd, sd, H_ATT, 2 * D_QK))
        vd_l.append(v2)
        cd_l.append(u2_pad[:, -(CONV_W - 1):])

    y_prompt = rmsnorm(hp[:, N_META:], final_g)
    y_sample = rmsnorm(hd, final_g)
    return (y_prompt, y_sample, jnp.stack(kp_l), jnp.stack(vp_l), jnp.stack(cp_l),
            jnp.stack(kd_l), jnp.stack(vd_l), jnp.stack(cd_l))
```

```python
import functools
import math

import numpy as np
import jax
import jax.numpy as jnp
from jax import lax
from jax.experimental import pallas as pl
from jax.experimental.pallas import tpu as pltpu

f32 = jnp.float32
bf16 = jnp.bfloat16

CHUNK = 64
N_META = 16
H_ATT = 4
D_QK = 64
D_V = 2 * D_QK
W_ATT = H_ATT * D_V
CONV_W = 3
N_SPLIT = 8
N_BUCKETS = 32
MAX_DIST = 128
EPS = 1e-6
NEG = -1e30
SCALE = D_QK ** -0.5
LOG2E = math.log2(math.e)
LAM_INIT = 0.8 - 0.6 * math.exp(-0.3 * 0)
FAR_BUCKET = N_BUCKETS // 2 - 1

LANES = 128
ATT_TILE = 256
VMEM_LIMIT = 48 * 1024 * 1024


def _silu(z):
    return z * (1.0 / (1.0 + jnp.exp(-z)))


def _rms(x, g):
    return x * lax.rsqrt(jnp.mean(x * x, axis=-1, keepdims=True) + EPS) * g


def _in_proj_kernel(x_ref, cinit_ref, g_ref, w_ref, cw_ref,
                    q_ref, k_ref, kb_ref, v_ref, vb_ref, za_ref, yc_ref, clast_ref,
                    carry_ref, *, tm, wg):
    j = pl.program_id(1)

    @pl.when(j == 0)
    def _():
        carry_ref[...] = cinit_ref[0]

    x = x_ref[0]
    h = _rms(x, g_ref[...]).astype(bf16)

    def proj(g):
        return jnp.dot(h, w_ref[:, g * wg:(g + 1) * wg], preferred_element_type=f32)

    q_ref[0] = (proj(0) * (SCALE * LOG2E)).astype(bf16)
    k = proj(1)
    k_ref[0] = k
    kb_ref[0] = k.astype(bf16)
    v = proj(2)
    v_ref[0] = v
    vb_ref[0] = v.astype(bf16)
    za_ref[0] = proj(3)

    cu = proj(5) * proj(6)
    prev = carry_ref[...]
    row = lax.broadcasted_iota(jnp.int32, cu.shape, 0)
    cu_m1 = jnp.where(row == 0, prev[1:2], pltpu.roll(cu, 1, axis=0))
    cu_m2 = jnp.where(row == 0, prev[0:1],
                      jnp.where(row == 1, prev[1:2], pltpu.roll(cu, 2, axis=0)))
    cw = cw_ref[...]
    conv = cw[0:1] * cu_m2 + cw[1:2] * cu_m1 + cw[2:3] * cu
    yc_ref[0] = (_silu(proj(7)) * proj(4) * conv).astype(bf16)

    last = cu[tm - 2:tm]
    carry_ref[...] = last
    clast_ref[0] = last


def _in_proj(x, conv_init, norm_g, w_bf, conv_w, tm):
    b, l, d = x.shape
    w_all = w_bf.shape[1]
    wg = w_all // N_SPLIT
    nj = l // tm
    row_spec = lambda width: pl.BlockSpec((1, tm, width), lambda i, j: (i, j, 0))
    const = lambda shape: pl.BlockSpec(shape, lambda i, j: (0,) * len(shape))
    state_spec = pl.BlockSpec((1, CONV_W - 1, wg), lambda i, j: (i, 0, 0))
    out_shape = (
        jax.ShapeDtypeStruct((b, l, wg), bf16),
        jax.ShapeDtypeStruct((b, l, wg), f32),
        jax.ShapeDtypeStruct((b, l, wg), bf16),
        jax.ShapeDtypeStruct((b, l, wg), f32),
        jax.ShapeDtypeStruct((b, l, wg), bf16),
        jax.ShapeDtypeStruct((b, l, wg), f32),
        jax.ShapeDtypeStruct((b, l, wg), bf16),
        jax.ShapeDtypeStruct((b, CONV_W - 1, wg), f32),
    )
    return pl.pallas_call(
        functools.partial(_in_proj_kernel, tm=tm, wg=wg),
        grid=(b, nj),
        in_specs=[row_spec(d), state_spec, const((1, d)), const((d, w_all)), const((CONV_W, wg))],
        out_specs=(row_spec(wg),) * 7 + (state_spec,),
        out_shape=out_shape,
        scratch_shapes=[pltpu.VMEM((CONV_W - 1, wg), f32)],
        compiler_params=pltpu.CompilerParams(
            dimension_semantics=("arbitrary", "arbitrary"), vmem_limit_bytes=VMEM_LIMIT),
    )(x, conv_init, norm_g.reshape(1, d), w_bf, conv_w)


def _rel_bucket(rel):
    nb = N_BUCKETS // 2
    ret = jnp.where(rel > 0, nb, 0)
    n = jnp.abs(rel)
    max_exact = nb // 2
    nf = jnp.maximum(n, 1).astype(f32)
    large = max_exact + (jnp.log(nf / max_exact) / math.log(MAX_DIST / max_exact)
                         * (nb - max_exact)).astype(jnp.int32)
    large = jnp.minimum(large, nb - 1)
    return ret + jnp.where(n < max_exact, n, large)


def _chunk_id_np(pos):
    return np.where(pos < 0, -1, pos // CHUNK)


def _assert_far_past(q_pos, k_pos):
    rel = k_pos[None, :] - q_pos[:, None]
    nb = N_BUCKETS // 2
    max_exact = nb // 2
    n = np.abs(rel).astype(np.float64)
    large = max_exact + np.log(n / max_exact) / math.log(MAX_DIST / max_exact) * (nb - max_exact)
    assert np.all(rel < 0) and np.all(large >= nb), "tile is not in the saturated bucket"
    assert np.all(_chunk_id_np(k_pos)[None, :] <= _chunk_id_np(q_pos)[:, None])


def _bias_kernel(bucket_ref, vis_ref, rb_ref, o_ref):
    h = pl.program_id(0)
    bucket = bucket_ref[...]
    far = rb_ref[FAR_BUCKET, h]
    acc = jnp.zeros(bucket.shape, f32)
    for b in range(N_BUCKETS):
        acc = jnp.where(bucket == b, rb_ref[b, h] - far, acc)
    o_ref[0] = jnp.where(vis_ref[...] != 0, acc * LOG2E, NEG)


def _build_bias(q_pos, k_pos, k_valid, rel_bias):
    rel = k_pos[None, :] - q_pos[:, None]
    vis = (_chunk_id_np(k_pos)[None, :] <= _chunk_id_np(q_pos)[:, None]) & k_valid[None, :]
    bucket = _rel_bucket(jnp.asarray(rel, jnp.int32))
    r, c = rel.shape
    full = pl.BlockSpec((r, c), lambda h: (0, 0))
    return pl.pallas_call(
        _bias_kernel,
        grid=(H_ATT,),
        in_specs=[full, full, pl.BlockSpec(memory_space=pltpu.SMEM)],
        out_specs=pl.BlockSpec((1, r, c), lambda h: (h, 0, 0)),
        out_shape=jax.ShapeDtypeStruct((H_ATT, r, c), f32),
    )(bucket, jnp.asarray(vis.astype(np.int32)), rel_bias)


def _diff_lambda(lam_ref):
    lv = lam_ref[...]
    s1 = jnp.sum(lv[0:1] * lv[1:2], axis=-1, keepdims=True)
    s2 = jnp.sum(lv[2:3] * lv[3:4], axis=-1, keepdims=True)
    return jnp.exp(s1) - jnp.exp(s2) + LAM_INIT


def _gate_heads(o1, o2, lam, sg, za):
    o = o1 - lam * o2
    return _rms(o, sg) * (1.0 - LAM_INIT) * _silu(za)


def _prompt_attn_kernel(q_ref, k_ref, v_ref, km_ref, vm_ref, bd_ref, bm_ref, za_ref, sg_ref,
                        lam_ref, o_ref, m_sc, l_sc, acc_sc, *, t):
    i = pl.program_id(2)
    q = q_ref[0]
    lane = lax.broadcasted_iota(jnp.int32, q.shape, 1)
    zero = jnp.zeros_like(q)
    qs = (jnp.where(lane < D_QK, q, zero), jnp.where(lane >= D_QK, q, zero))

    m_sc[...] = jnp.full(m_sc.shape, NEG, f32)
    l_sc[...] = jnp.zeros(l_sc.shape, f32)
    acc_sc[...] = jnp.zeros(acc_sc.shape, f32)

    def step(kt, vt, bias):
        for c in range(2):
            s = lax.dot_general(qs[c], kt, (((1,), (1,)), ((), ())), preferred_element_type=f32)
            if bias is not None:
                s = s + bias
            m_prev = m_sc[c]
            m_new = jnp.maximum(m_prev, jnp.max(s, axis=-1, keepdims=True))
            alpha = jnp.exp2(m_prev - m_new)
            e = jnp.exp2(s - m_new)
            l_sc[c] = alpha * l_sc[c] + jnp.sum(e, axis=-1, keepdims=True)
            acc_sc[c] = alpha * acc_sc[c] + jnp.dot(e.astype(bf16), vt, preferred_element_type=f32)
            m_sc[c] = m_new

    def tile(j):
        start = pl.multiple_of(j * t, t)
        return k_ref[0, pl.ds(start, t), :], v_ref[0, pl.ds(start, t), :]

    step(km_ref[...], vm_ref[...], bm_ref[jnp.minimum(i, 1), 0])

    def far_tile(j, carry):
        step(*tile(j), None)
        return carry

    lax.fori_loop(0, jnp.maximum(i - 1, 0), far_tile, 0)

    @pl.when(i >= 1)
    def _():
        step(*tile(i - 1), bd_ref[1, 0])

    step(*tile(i), bd_ref[0, 0])

    o1 = acc_sc[0] / l_sc[0]
    o2 = acc_sc[1] / l_sc[1]
    o_ref[0] = _gate_heads(o1, o2, _diff_lambda(lam_ref), sg_ref[...], za_ref[0]).astype(bf16)


def _prompt_attn(q, kb, vb, km, vm, bias_diag, bias_meta, za, subln_g, lam_vecs):
    b, l, _ = q.shape
    t = ATT_TILE
    nq = l // t
    tile_spec = pl.BlockSpec((1, t, D_V), lambda bi, h, i: (bi, i, h))
    seq_spec = pl.BlockSpec((1, l, D_V), lambda bi, h, i: (bi, 0, h))
    meta_spec = pl.BlockSpec((LANES, D_V), lambda bi, h, i: (0, h))
    return pl.pallas_call(
        functools.partial(_prompt_attn_kernel, t=t),
        grid=(b, H_ATT, nq),
        in_specs=[
            tile_spec, seq_spec, seq_spec, meta_spec, meta_spec,
            pl.BlockSpec((2, 1, t, t), lambda bi, h, i: (0, h, 0, 0)),
            pl.BlockSpec((2, 1, t, LANES), lambda bi, h, i: (0, h, 0, 0)),
            tile_spec,
            pl.BlockSpec((1, D_V), lambda bi, h, i: (0, 0)),
            pl.BlockSpec((4, D_QK), lambda bi, h, i: (0, 0)),
        ],
        out_specs=tile_spec,
        out_shape=jax.ShapeDtypeStruct((b, l, W_ATT), bf16),
        scratch_shapes=[pltpu.VMEM((2, t, 1), f32), pltpu.VMEM((2, t, 1), f32),
                        pltpu.VMEM((2, t, D_V), f32)],
        compiler_params=pltpu.CompilerParams(
            dimension_semantics=("arbitrary", "arbitrary", "arbitrary"),
            vmem_limit_bytes=VMEM_LIMIT),
    )(q, kb, vb, km, vm, bias_diag, bias_meta, za, subln_g, lam_vecs)


def _merge(og, yc, x, wo_ref, fg):
    y = x + jnp.dot(og, wo_ref[0:W_ATT, :], preferred_element_type=f32)
    y = y + jnp.dot(yc, wo_ref[W_ATT:, :], preferred_element_type=f32)
    return _rms(y, fg)


def _out_proj_kernel(og_ref, yc_ref, x_ref, wo_ref, fg_ref, y_ref):
    y_ref[...] = _merge(og_ref[...], yc_ref[...], x_ref[...], wo_ref, fg_ref[...])


def _out_proj(og, yc, x, wo_bf, final_g, tm):
    n, d = x.shape
    w_att = og.shape[1]
    w_conv = yc.shape[1]
    row_spec = lambda width: pl.BlockSpec((tm, width), lambda i: (i, 0))
    return pl.pallas_call(
        _out_proj_kernel,
        grid=(n // tm,),
        in_specs=[row_spec(w_att), row_spec(w_conv), row_spec(d),
                  pl.BlockSpec((w_att + w_conv, d), lambda i: (0, 0)),
                  pl.BlockSpec((1, d), lambda i: (0, 0))],
        out_specs=row_spec(d),
        out_shape=jax.ShapeDtypeStruct((n, d), f32),
        compiler_params=pltpu.CompilerParams(
            dimension_semantics=("arbitrary",), vmem_limit_bytes=VMEM_LIMIT),
    )(og, yc, x, wo_bf, final_g.reshape(1, d))


def _small_attn_kernel(*refs, has_cache):
    if has_cache:
        (q_ref, kn_ref, vn_ref, bn_ref, ck_ref, cv_ref, bc_ref,
         za_ref, yc_ref, x_ref, sg_ref, lam_ref, wo_ref, fg_ref, y_ref) = refs
    else:
        (q_ref, kn_ref, vn_ref, bn_ref,
         za_ref, yc_ref, x_ref, sg_ref, lam_ref, wo_ref, fg_ref, y_ref) = refs
    q = q_ref[0]
    lane = lax.broadcasted_iota(jnp.int32, (q.shape[0], D_V), 1)
    lam = _diff_lambda(lam_ref)
    nt = (((1,), (1,)), ((), ()))
    heads = []
    for h in range(H_ATT):
        cols = slice(h * D_V, (h + 1) * D_V)
        qh = q[:, cols]
        zero = jnp.zeros_like(qh)
        kn = kn_ref[0][:, cols]
        vn = vn_ref[0][:, cols]
        if has_cache:
            ck = ck_ref[0][:, cols].astype(bf16)
            cv = cv_ref[0][:, cols].astype(bf16)
        outs = []
        for c in range(2):
            qc = jnp.where((lane < D_QK) if c == 0 else (lane >= D_QK), qh, zero)
            sn = lax.dot_general(qc, kn, nt, preferred_element_type=f32) + bn_ref[h]
            m = jnp.max(sn, axis=-1, keepdims=True)
            if has_cache:
                sc = lax.dot_general(qc, ck, nt, preferred_element_type=f32) + bc_ref[h]
                m = jnp.maximum(m, jnp.max(sc, axis=-1, keepdims=True))
            en = jnp.exp2(sn - m)
            l = jnp.sum(en, axis=-1, keepdims=True)
            acc = jnp.dot(en.astype(bf16), vn, preferred_element_type=f32)
            if has_cache:
                ec = jnp.exp2(sc - m)
                l = l + jnp.sum(ec, axis=-1, keepdims=True)
                acc = acc + jnp.dot(ec.astype(bf16), cv, preferred_element_type=f32)
            outs.append(acc / l)
        heads.append(_gate_heads(outs[0], outs[1], lam, sg_ref[...], za_ref[0][:, cols]))
    og = jnp.concatenate(heads, axis=-1).astype(bf16)
    y_ref[0] = _merge(og, yc_ref[0], x_ref[0], wo_ref, fg_ref[...])


def _small_attn(q, kb, vb, bias_new, za, yc, x, subln_g, lam_vecs, wo_bf, final_g,
                cache_k=None, cache_v=None, bias_cache=None):
    b, lq, d = x.shape
    has_cache = cache_k is not None
    per_b = lambda rows, width: pl.BlockSpec((1, rows, width), lambda i: (i, 0, 0))
    const = lambda shape: pl.BlockSpec(shape, lambda i: (0,) * len(shape))
    w = q.shape[2]
    args = [q, kb, vb, bias_new]
    specs = [per_b(lq, w), per_b(lq, w), per_b(lq, w), const(bias_new.shape)]
    if has_cache:
        lk = cache_k.shape[1]
        args += [cache_k, cache_v, bias_cache]
        specs += [per_b(lk, w), per_b(lk, w), const(bias_cache.shape)]
    args += [za, yc, x, subln_g, lam_vecs, wo_bf, final_g.reshape(1, d)]
    specs += [per_b(lq, w), per_b(lq, yc.shape[2]), per_b(lq, d), const(subln_g.shape),
              const(lam_vecs.shape), const(wo_bf.shape), const((1, d))]
    return pl.pallas_call(
        functools.partial(_small_attn_kernel, has_cache=has_cache),
        grid=(b,),
        in_specs=specs,
        out_specs=per_b(lq, d),
        out_shape=jax.ShapeDtypeStruct((b, lq, d), f32),
        compiler_params=pltpu.CompilerParams(
            dimension_semantics=("arbitrary",), vmem_limit_bytes=VMEM_LIMIT),
    )(*args)


def kernel(x_prompt, x_sample, cache_k, cache_v, state_conv, meta_tokens, rel_bias, norm_g, w_in,
           conv_w, lambda_q1, lambda_k1, lambda_q2, lambda_k2, subln_g, w_out, final_g):
    bp, sp, d = x_prompt.shape
    bd, sd, _ = x_sample.shape
    depth = w_in.shape[0]
    assert depth == 1, "single-layer step only"
    past = cache_k.shape[2] - N_META
    t = ATT_TILE
    assert sp % t == 0 and t % CHUNK == 0

    w_bf = w_in[0].astype(bf16)
    wo_bf = w_out[0].astype(bf16)
    lam_vecs = jnp.stack([lambda_q1[0], lambda_k1[0], lambda_q2[0], lambda_k2[0]])
    sg = subln_g[0].reshape(1, D_V)
    wg = w_bf.shape[1] // N_SPLIT

    zeros_state = jnp.zeros((1, CONV_W - 1, wg), f32)
    (qm, km, kmb, vm, vmb, zam, ycm, cm) = _in_proj(
        meta_tokens[None], zeros_state, norm_g[0], w_bf, conv_w[0], tm=N_META)
    (qp, kp, kpb, vp, vpb, zap, ycp, cp) = _in_proj(
        x_prompt, jnp.broadcast_to(cm, (bp, CONV_W - 1, wg)), norm_g[0], w_bf, conv_w[0], tm=512)
    (qd, kd, kdb, vd, vdb, zad, ycd, cd) = _in_proj(
        x_sample, state_conv[0], norm_g[0], w_bf, conv_w[0], tm=sd)

    meta_pos = np.arange(-N_META, 0)
    tile_pos = np.arange(t)
    ones = lambda n: np.ones((n,), bool)
    bias_diag = jnp.stack([
        _build_bias(tile_pos + t, tile_pos + t, ones(t), rel_bias),
        _build_bias(tile_pos + t, tile_pos, ones(t), rel_bias)])
    for off in range(2, sp // t):
        _assert_far_past(tile_pos + off * t, tile_pos)
    meta_pad_pos = np.concatenate([meta_pos, np.full((LANES - N_META,), -1)])
    meta_valid = np.arange(LANES) < N_META
    bias_meta = jnp.stack([
        _build_bias(tile_pos, meta_pad_pos, meta_valid, rel_bias),
        _build_bias(tile_pos + t, meta_pad_pos, meta_valid, rel_bias)])
    for off in range(1, sp // t):
        _assert_far_past(tile_pos + off * t, meta_pos)

    pad_rows = lambda a: jnp.pad(a[0], ((0, LANES - N_META), (0, 0)))
    og = _prompt_attn(qp, kpb, vpb, pad_rows(kmb), pad_rows(vmb), bias_diag, bias_meta,
                      zap, sg, lam_vecs)
    y_prompt = _out_proj(og.reshape(bp * sp, W_ATT), ycp.reshape(bp * sp, wg),
                         x_prompt.reshape(bp * sp, d), wo_bf, final_g, tm=512).reshape(bp, sp, d)

    _small_attn(qm, kmb, vmb, _build_bias(meta_pos, meta_pos, ones(N_META), rel_bias),
                zam, ycm, meta_tokens[None], sg, lam_vecs, wo_bf, final_g)

    d_qpos = past + np.arange(sd)
    cache_pos = np.concatenate([meta_pos, np.arange(past)])
    y_sample = _small_attn(
        qd, kdb, vdb, _build_bias(d_qpos, d_qpos, ones(sd), rel_bias),
        zad, ycd, x_sample, sg, lam_vecs, wo_bf, final_g,
        cache_k=cache_k[0].reshape(bd, N_META + past, W_ATT),
        cache_v=cache_v[0].reshape(bd, N_META + past, W_ATT),
        bias_cache=_build_bias(d_qpos, cache_pos, ones(N_META + past), rel_bias))

    k_prompt = jnp.concatenate([jnp.broadcast_to(km, (bp, N_META, wg)), kp], axis=1)
    v_prompt = jnp.concatenate([jnp.broadcast_to(vm, (bp, N_META, wg)), vp], axis=1)
    return (y_prompt, y_sample,
            k_prompt.reshape(1, bp, N_META + sp, H_ATT, D_V),
            v_prompt.reshape(1, bp, N_META + sp, H_ATT, D_V),
            cp[None],
            kd.reshape(1, bd, sd, H_ATT, D_V),
            vd.reshape(1, bd, sd, H_ATT, D_V),
            cd[None])
```

```python
import functools
import math

import numpy as np
import jax
import jax.numpy as jnp
from jax import lax
from jax.experimental import pallas as pl
from jax.experimental.pallas import tpu as pltpu

f32 = jnp.float32
bf16 = jnp.bfloat16

CHUNK = 64
N_META = 16
H_ATT = 4
D_QK = 64
D_V = 2 * D_QK
W_ATT = H_ATT * D_V
CONV_W = 3
N_SPLIT = 8
N_BUCKETS = 32
MAX_DIST = 128
EPS = 1e-6
NEG = -1e30
SCALE = D_QK ** -0.5
LOG2E = math.log2(math.e)
LAM_INIT = 0.8 - 0.6 * math.exp(-0.3 * 0)
FAR_BUCKET = N_BUCKETS // 2 - 1

LANES = 128
ATT_TILE = 256
ATT_SUB = 4
VMEM_LIMIT = 48 * 1024 * 1024


def _silu(z):
    return z * (1.0 / (1.0 + jnp.exp(-z)))


def _rms(x, g):
    return x * lax.rsqrt(jnp.mean(x * x, axis=-1, keepdims=True) + EPS) * g


def _in_proj_kernel(x_ref, cinit_ref, g_ref, w_ref, cw_ref,
                    q_ref, k_ref, kb_ref, v_ref, vb_ref, za_ref, yc_ref, clast_ref,
                    carry_ref, *, tm, wg):
    j = pl.program_id(1)

    @pl.when(j == 0)
    def _():
        carry_ref[...] = cinit_ref[0]

    x = x_ref[0]
    h = _rms(x, g_ref[...]).astype(bf16)

    def proj(g):
        return jnp.dot(h, w_ref[:, g * wg:(g + 1) * wg], preferred_element_type=f32)

    q_ref[0] = (proj(0) * (SCALE * LOG2E)).astype(bf16)
    k = proj(1)
    k_ref[0] = k
    kb_ref[0] = k.astype(bf16)
    v = proj(2)
    v_ref[0] = v
    vb_ref[0] = v.astype(bf16)
    za_ref[0] = proj(3)

    cu = proj(5) * proj(6)
    prev = carry_ref[...]
    row = lax.broadcasted_iota(jnp.int32, cu.shape, 0)
    cu_m1 = jnp.where(row == 0, prev[1:2], pltpu.roll(cu, 1, axis=0))
    cu_m2 = jnp.where(row == 0, prev[0:1],
                      jnp.where(row == 1, prev[1:2], pltpu.roll(cu, 2, axis=0)))
    cw = cw_ref[...]
    conv = cw[0:1] * cu_m2 + cw[1:2] * cu_m1 + cw[2:3] * cu
    yc_ref[0] = (_silu(proj(7)) * proj(4) * conv).astype(bf16)

    last = cu[tm - 2:tm]
    carry_ref[...] = last
    clast_ref[0] = last


def _in_proj(x, conv_init, norm_g, w_bf, conv_w, tm):
    b, l, d = x.shape
    w_all = w_bf.shape[1]
    wg = w_all // N_SPLIT
    nj = l // tm
    row_spec = lambda width: pl.BlockSpec((1, tm, width), lambda i, j: (i, j, 0))
    const = lambda shape: pl.BlockSpec(shape, lambda i, j: (0,) * len(shape))
    state_spec = pl.BlockSpec((1, CONV_W - 1, wg), lambda i, j: (i, 0, 0))
    out_shape = (
        jax.ShapeDtypeStruct((b, l, wg), bf16),
        jax.ShapeDtypeStruct((b, l, wg), f32),
        jax.ShapeDtypeStruct((b, l, wg), bf16),
        jax.ShapeDtypeStruct((b, l, wg), f32),
        jax.ShapeDtypeStruct((b, l, wg), bf16),
        jax.ShapeDtypeStruct((b, l, wg), f32),
        jax.ShapeDtypeStruct((b, l, wg), bf16),
        jax.ShapeDtypeStruct((b, CONV_W - 1, wg), f32),
    )
    return pl.pallas_call(
        functools.partial(_in_proj_kernel, tm=tm, wg=wg),
        grid=(b, nj),
        in_specs=[row_spec(d), state_spec, const((1, d)), const((d, w_all)), const((CONV_W, wg))],
        out_specs=(row_spec(wg),) * 7 + (state_spec,),
        out_shape=out_shape,
        scratch_shapes=[pltpu.VMEM((CONV_W - 1, wg), f32)],
        compiler_params=pltpu.CompilerParams(
            dimension_semantics=("arbitrary", "arbitrary"), vmem_limit_bytes=VMEM_LIMIT),
    )(x, conv_init, norm_g.reshape(1, d), w_bf, conv_w)


def _rel_bucket(rel):
    nb = N_BUCKETS // 2
    ret = jnp.where(rel > 0, nb, 0)
    n = jnp.abs(rel)
    max_exact = nb // 2
    nf = jnp.maximum(n, 1).astype(f32)
    large = max_exact + (jnp.log(nf / max_exact) / math.log(MAX_DIST / max_exact)
                         * (nb - max_exact)).astype(jnp.int32)
    large = jnp.minimum(large, nb - 1)
    return ret + jnp.where(n < max_exact, n, large)


def _chunk_id_np(pos):
    return np.where(pos < 0, -1, pos // CHUNK)


def _assert_far_past(q_pos, k_pos):
    rel = k_pos[None, :] - q_pos[:, None]
    nb = N_BUCKETS // 2
    max_exact = nb // 2
    n = np.abs(rel).astype(np.float64)
    large = max_exact + np.log(n / max_exact) / math.log(MAX_DIST / max_exact) * (nb - max_exact)
    assert np.all(rel < 0) and np.all(large >= nb), "tile is not in the saturated bucket"
    assert np.all(_chunk_id_np(k_pos)[None, :] <= _chunk_id_np(q_pos)[:, None])


def _bias_kernel(bucket_ref, vis_ref, rb_ref, o_ref):
    h = pl.program_id(0)
    bucket = bucket_ref[...]
    far = rb_ref[FAR_BUCKET, h]
    acc = jnp.zeros(bucket.shape, f32)
    for b in range(N_BUCKETS):
        acc = jnp.where(bucket == b, rb_ref[b, h] - far, acc)
    o_ref[0] = jnp.where(vis_ref[...] != 0, acc * LOG2E, NEG)


def _build_bias(q_pos, k_pos, k_valid, rel_bias, keys_major=False):
    rel = k_pos[None, :] - q_pos[:, None]
    vis = (_chunk_id_np(k_pos)[None, :] <= _chunk_id_np(q_pos)[:, None]) & k_valid[None, :]
    if keys_major:
        rel, vis = rel.T, vis.T
    bucket = _rel_bucket(jnp.asarray(rel, jnp.int32))
    r, c = rel.shape
    full = pl.BlockSpec((r, c), lambda h: (0, 0))
    return pl.pallas_call(
        _bias_kernel,
        grid=(H_ATT,),
        in_specs=[full, full, pl.BlockSpec(memory_space=pltpu.SMEM)],
        out_specs=pl.BlockSpec((1, r, c), lambda h: (h, 0, 0)),
        out_shape=jax.ShapeDtypeStruct((H_ATT, r, c), f32),
    )(bucket, jnp.asarray(vis.astype(np.int32)), rel_bias)


def _diff_lambda(lam_ref):
    lv = lam_ref[...]
    s1 = jnp.sum(lv[0:1] * lv[1:2], axis=-1, keepdims=True)
    s2 = jnp.sum(lv[2:3] * lv[3:4], axis=-1, keepdims=True)
    return jnp.exp(s1) - jnp.exp(s2) + LAM_INIT


def _gate_heads(o1, o2, lam, sg, za):
    o = o1 - lam * o2
    return _rms(o, sg) * (1.0 - LAM_INIT) * _silu(za)


def _prompt_attn_kernel(q_ref, k_ref, v_ref, km_ref, vm_ref, bd_ref, bm_ref, za_ref, sg_ref,
                        lam_ref, o_ref, vt_sc, vmt_sc, m_sc, l_sc, acc_sc, *, t, n_sub, nk):
    i = pl.program_id(2)
    tq = n_sub * t

    @pl.when(i == 0)
    def _():
        for j in range(nk):
            vt_sc[j] = v_ref[0, j * t:(j + 1) * t, :].T.astype(bf16)
        vmt_sc[...] = vm_ref[...].T.astype(bf16)

    q = q_ref[0]
    lane = lax.broadcasted_iota(jnp.int32, q.shape, 1)
    zero = jnp.zeros_like(q)
    qs = (jnp.where(lane < D_QK, q, zero), jnp.where(lane >= D_QK, q, zero))

    m_sc[...] = jnp.full(m_sc.shape, NEG, f32)
    l_sc[...] = jnp.zeros(l_sc.shape, f32)
    acc_sc[...] = jnp.zeros(acc_sc.shape, f32)

    def step(kt, vt, c0, biases):
        for c in range(2):
            s = lax.dot_general(kt, qs[c][c0:], (((1,), (1,)), ((), ())),
                                preferred_element_type=f32)
            if biases:
                parts = [s[:, p * t:(p + 1) * t] + b for p, b in enumerate(biases)]
                if len(biases) * t < tq - c0:
                    parts.append(s[:, len(biases) * t:])
                s = jnp.concatenate(parts, axis=1)
            m_prev = m_sc[c, :, c0:]
            m_new = jnp.maximum(m_prev, jnp.max(s, axis=0, keepdims=True))
            alpha = jnp.exp2(m_prev - m_new)
            e = jnp.exp2(s - m_new)
            l_sc[c, :, c0:] = alpha * l_sc[c, :, c0:] + jnp.sum(e, axis=0, keepdims=True)
            acc_sc[c, :, c0:] = alpha * acc_sc[c, :, c0:] + jnp.dot(
                vt, e.astype(bf16), preferred_element_type=f32)
            m_sc[c, :, c0:] = m_new

    def tile(j):
        return k_ref[0, pl.ds(pl.multiple_of(j * t, t), t), :], vt_sc[j]

    step(km_ref[...], vmt_sc[...], 0,
         [bm_ref[jnp.minimum(i, 1), 0]] + [bm_ref[1, 0]] * (n_sub - 1))

    def far_tile(j, carry):
        step(*tile(j), 0, [])
        return carry

    lax.fori_loop(0, jnp.maximum(n_sub * i - 1, 0), far_tile, 0)

    @pl.when(i >= 1)
    def _():
        step(*tile(n_sub * i - 1), 0, [bd_ref[1, 0]])

    for r in range(n_sub):
        step(*tile(n_sub * i + r), r * t, [bd_ref[0, 0]] + [bd_ref[1, 0]] * (r + 1 < n_sub))

    o = acc_sc[0] / l_sc[0] - _diff_lambda(lam_ref) * (acc_sc[1] / l_sc[1])
    o = o * lax.rsqrt(jnp.mean(o * o, axis=0, keepdims=True) + EPS)
    o_ref[0] = (o.T * (sg_ref[...] * (1.0 - LAM_INIT)) * _silu(za_ref[0])).astype(bf16)


def _prompt_attn(q, kb, v, km, vm, bias_diag, bias_meta, za, subln_g, lam_vecs):
    b, l, _ = q.shape
    t = ATT_TILE
    tq = ATT_SUB * t
    q_spec = pl.BlockSpec((1, tq, D_V), lambda bi, h, i: (bi, i, h))
    seq_spec = pl.BlockSpec((1, l, D_V), lambda bi, h, i: (bi, 0, h))
    meta_spec = pl.BlockSpec((LANES, D_V), lambda bi, h, i: (0, h))
    return pl.pallas_call(
        functools.partial(_prompt_attn_kernel, t=t, n_sub=ATT_SUB, nk=l // t),
        grid=(b, H_ATT, l // tq),
        in_specs=[
            q_spec, seq_spec, seq_spec, meta_spec, meta_spec,
            pl.BlockSpec((2, 1, t, t), lambda bi, h, i: (0, h, 0, 0)),
            pl.BlockSpec((2, 1, LANES, t), lambda bi, h, i: (0, h, 0, 0)),
            q_spec,
            pl.BlockSpec((1, D_V), lambda bi, h, i: (0, 0)),
            pl.BlockSpec((4, D_QK), lambda bi, h, i: (0, 0)),
        ],
        out_specs=q_spec,
        out_shape=jax.ShapeDtypeStruct((b, l, W_ATT), bf16),
        scratch_shapes=[pltpu.VMEM((l // t, D_V, t), bf16), pltpu.VMEM((D_V, LANES), bf16),
                        pltpu.VMEM((2, 1, tq), f32), pltpu.VMEM((2, 1, tq), f32),
                        pltpu.VMEM((2, D_V, tq), f32)],
        compiler_params=pltpu.CompilerParams(
            dimension_semantics=("arbitrary", "arbitrary", "arbitrary"),
            vmem_limit_bytes=VMEM_LIMIT),
    )(q, kb, v, km, vm, bias_diag, bias_meta, za, subln_g, lam_vecs)


def _merge(og, yc, x, wo_ref, fg):
    y = x + jnp.dot(og, wo_ref[0:W_ATT, :], preferred_element_type=f32)
    y = y + jnp.dot(yc, wo_ref[W_ATT:, :], preferred_element_type=f32)
    return _rms(y, fg)


def _out_proj_kernel(og_ref, yc_ref, x_ref, wo_ref, fg_ref, y_ref):
    y_ref[...] = _merge(og_ref[...], yc_ref[...], x_ref[...], wo_ref, fg_ref[...])


def _out_proj(og, yc, x, wo_bf, final_g, tm):
    n, d = x.shape
    w_att = og.shape[1]
    w_conv = yc.shape[1]
    row_spec = lambda width: pl.BlockSpec((tm, width), lambda i: (i, 0))
    return pl.pallas_call(
        _out_proj_kernel,
        grid=(n // tm,),
        in_specs=[row_spec(w_att), row_spec(w_conv), row_spec(d),
                  pl.BlockSpec((w_att + w_conv, d), lambda i: (0, 0)),
                  pl.BlockSpec((1, d), lambda i: (0, 0))],
        out_specs=row_spec(d),
        out_shape=jax.ShapeDtypeStruct((n, d), f32),
        compiler_params=pltpu.CompilerParams(
            dimension_semantics=("arbitrary",), vmem_limit_bytes=VMEM_LIMIT),
    )(og, yc, x, wo_bf, final_g.reshape(1, d))


def _small_attn_kernel(*refs, has_cache):
    if has_cache:
        (q_ref, kn_ref, vn_ref, bn_ref, ck_ref, cv_ref, bc_ref,
         za_ref, yc_ref, x_ref, sg_ref, lam_ref, wo_ref, fg_ref, y_ref) = refs
    else:
        (q_ref, kn_ref, vn_ref, bn_ref,
         za_ref, yc_ref, x_ref, sg_ref, lam_ref, wo_ref, fg_ref, y_ref) = refs
    q = q_ref[0]
    lane = lax.broadcasted_iota(jnp.int32, (q.shape[0], D_V), 1)
    lam = _diff_lambda(lam_ref)
    nt = (((1,), (1,)), ((), ()))
    heads = []
    for h in range(H_ATT):
        cols = slice(h * D_V, (h + 1) * D_V)
        qh = q[:, cols]
        zero = jnp.zeros_like(qh)
        kn = kn_ref[0][:, cols]
        vn = vn_ref[0][:, cols]
        if has_cache:
            ck = ck_ref[0][:, cols].astype(bf16)
            cv = cv_ref[0][:, cols].astype(bf16)
        outs = []
        for c in range(2):
            qc = jnp.where((lane < D_QK) if c == 0 else (lane >= D_QK), qh, zero)
            sn = lax.dot_general(qc, kn, nt, preferred_element_type=f32) + bn_ref[h]
            m = jnp.max(sn, axis=-1, keepdims=True)
            if has_cache:
                sc = lax.dot_general(qc, ck, nt, preferred_element_type=f32) + bc_ref[h]
                m = jnp.maximum(m, jnp.max(sc, axis=-1, keepdims=True))
            en = jnp.exp2(sn - m)
            l = jnp.sum(en, axis=-1, keepdims=True)
            acc = jnp.dot(en.astype(bf16), vn, preferred_element_type=f32)
            if has_cache:
                ec = jnp.exp2(sc - m)
                l = l + jnp.sum(ec, axis=-1, keepdims=True)
                acc = acc + jnp.dot(ec.astype(bf16), cv, preferred_element_type=f32)
            outs.append(acc / l)
        heads.append(_gate_heads(outs[0], outs[1], lam, sg_ref[...], za_ref[0][:, cols]))
    og = jnp.concatenate(heads, axis=-1).astype(bf16)
    y_ref[0] = _merge(og, yc_ref[0], x_ref[0], wo_ref, fg_ref[...])


def _small_attn(q, kb, vb, bias_new, za, yc, x, subln_g, lam_vecs, wo_bf, final_g,
                cache_k=None, cache_v=None, bias_cache=None):
    b, lq, d = x.shape
    has_cache = cache_k is not None
    per_b = lambda rows, width: pl.BlockSpec((1, rows, width), lambda i: (i, 0, 0))
    const = lambda shape: pl.BlockSpec(shape, lambda i: (0,) * len(shape))
    w = q.shape[2]
    args = [q, kb, vb, bias_new]
    specs = [per_b(lq, w), per_b(lq, w), per_b(lq, w), const(bias_new.shape)]
    if has_cache:
        lk = cache_k.shape[1]
        args += [cache_k, cache_v, bias_cache]
        specs += [per_b(lk, w), per_b(lk, w), const(bias_cache.shape)]
    args += [za, yc, x, subln_g, lam_vecs, wo_bf, final_g.reshape(1, d)]
    specs += [per_b(lq, w), per_b(lq, yc.shape[2]), per_b(lq, d), const(subln_g.shape),
              const(lam_vecs.shape), const(wo_bf.shape), const((1, d))]
    return pl.pallas_call(
        functools.partial(_small_attn_kernel, has_cache=has_cache),
        grid=(b,),
        in_specs=specs,
        out_specs=per_b(lq, d),
        out_shape=jax.ShapeDtypeStruct((b, lq, d), f32),
        compiler_params=pltpu.CompilerParams(
            dimension_semantics=("arbitrary",), vmem_limit_bytes=VMEM_LIMIT),
    )(*args)


def kernel(x_prompt, x_sample, cache_k, cache_v, state_conv, meta_tokens, rel_bias, norm_g, w_in,
           conv_w, lambda_q1, lambda_k1, lambda_q2, lambda_k2, subln_g, w_out, final_g):
    bp, sp, d = x_prompt.shape
    bd, sd, _ = x_sample.shape
    depth = w_in.shape[0]
    assert depth == 1, "single-layer step only"
    past = cache_k.shape[2] - N_META
    t = ATT_TILE
    assert sp % (ATT_SUB * t) == 0 and t % CHUNK == 0

    w_bf = w_in[0].astype(bf16)
    wo_bf = w_out[0].astype(bf16)
    lam_vecs = jnp.stack([lambda_q1[0], lambda_k1[0], lambda_q2[0], lambda_k2[0]])
    sg = subln_g[0].reshape(1, D_V)
    wg = w_bf.shape[1] // N_SPLIT

    zeros_state = jnp.zeros((1, CONV_W - 1, wg), f32)
    (qm, km, kmb, vm, vmb, zam, ycm, cm) = _in_proj(
        meta_tokens[None], zeros_state, norm_g[0], w_bf, conv_w[0], tm=N_META)
    (qp, kp, kpb, vp, vpb, zap, ycp, cp) = _in_proj(
        x_prompt, jnp.broadcast_to(cm, (bp, CONV_W - 1, wg)), norm_g[0], w_bf, conv_w[0], tm=512)
    (qd, kd, kdb, vd, vdb, zad, ycd, cd) = _in_proj(
        x_sample, state_conv[0], norm_g[0], w_bf, conv_w[0], tm=sd)

    meta_pos = np.arange(-N_META, 0)
    tile_pos = np.arange(t)
    ones = lambda n: np.ones((n,), bool)
    bias_diag = jnp.stack([
        _build_bias(tile_pos + t, tile_pos + t, ones(t), rel_bias, keys_major=True),
        _build_bias(tile_pos + t, tile_pos, ones(t), rel_bias, keys_major=True)])
    for off in range(2, sp // t):
        _assert_far_past(tile_pos + off * t, tile_pos)
    meta_pad_pos = np.concatenate([meta_pos, np.full((LANES - N_META,), -1)])
    meta_valid = np.arange(LANES) < N_META
    bias_meta = jnp.stack([
        _build_bias(tile_pos, meta_pad_pos, meta_valid, rel_bias, keys_major=True),
        _build_bias(tile_pos + t, meta_pad_pos, meta_valid, rel_bias, keys_major=True)])
    for off in range(1, sp // t):
        _assert_far_past(tile_pos + off * t, meta_pos)

    pad_rows = lambda a: jnp.pad(a[0], ((0, LANES - N_META), (0, 0)))
    og = _prompt_attn(qp, kpb, vp, pad_rows(kmb), pad_rows(vm), bias_diag, bias_meta,
                      zap, sg, lam_vecs)
    y_prompt = _out_proj(og.reshape(bp * sp, W_ATT), ycp.reshape(bp * sp, wg),
                         x_prompt.reshape(bp * sp, d), wo_bf, final_g, tm=512).reshape(bp, sp, d)

    _small_attn(qm, kmb, vmb, _build_bias(meta_pos, meta_pos, ones(N_META), rel_bias),
                zam, ycm, meta_tokens[None], sg, lam_vecs, wo_bf, final_g)

    d_qpos = past + np.arange(sd)
    cache_pos = np.concatenate([meta_pos, np.arange(past)])
    y_sample = _small_attn(
        qd, kdb, vdb, _build_bias(d_qpos, d_qpos, ones(sd), rel_bias),
        zad, ycd, x_sample, sg, lam_vecs, wo_bf, final_g,
        cache_k=cache_k[0].reshape(bd, N_META + past, W_ATT),
        cache_v=cache_v[0].reshape(bd, N_META + past, W_ATT),
        bias_cache=_build_bias(d_qpos, cache_pos, ones(N_META + past), rel_bias))

    k_prompt = jnp.concatenate([jnp.broadcast_to(km, (bp, N_META, wg)), kp], axis=1)
    v_prompt = jnp.concatenate([jnp.broadcast_to(vm, (bp, N_META, wg)), vp], axis=1)
    return (y_prompt, y_sample,
            k_prompt.reshape(1, bp, N_META + sp, H_ATT, D_V),
            v_prompt.reshape(1, bp, N_META + sp, H_ATT, D_V),
            cp[None],
            kd.reshape(1, bd, sd, H_ATT, D_V),
            vd.reshape(1, bd, sd, H_ATT, D_V),
            cd[None])
```

```python
import functools
import math

import numpy as np
import jax
import jax.numpy as jnp
from jax import lax
from jax.experimental import pallas as pl
from jax.experimental.pallas import tpu as pltpu

f32 = jnp.float32
bf16 = jnp.bfloat16

CHUNK = 64
N_META = 16
H_ATT = 4
D_QK = 64
D_V = 2 * D_QK
W_ATT = H_ATT * D_V
CONV_W = 3
N_SPLIT = 8
N_BUCKETS = 32
MAX_DIST = 128
EPS = 1e-6
NEG = -1e30
SCALE = D_QK ** -0.5
LOG2E = math.log2(math.e)
LAM_INIT = 0.8 - 0.6 * math.exp(-0.3 * 0)
FAR_BUCKET = N_BUCKETS // 2 - 1

LANES = 128
ATT_TILE = 256
ATT_SUB = 4
VMEM_LIMIT = 48 * 1024 * 1024


def _silu(z):
    return z * (1.0 / (1.0 + jnp.exp(-z)))


def _rms(x, g):
    return x * lax.rsqrt(jnp.mean(x * x, axis=-1, keepdims=True) + EPS) * g


def _in_proj_kernel(x_ref, cinit_ref, g_ref, w_ref, cw_ref,
                    q_ref, k_ref, kb_ref, v_ref, vb_ref, za_ref, yc_ref, clast_ref,
                    carry_ref, *, tm, wg):
    j = pl.program_id(1)

    @pl.when(j == 0)
    def _():
        carry_ref[...] = cinit_ref[0]

    x = x_ref[0]
    h = _rms(x, g_ref[...]).astype(bf16)

    def proj(g):
        return jnp.dot(h, w_ref[:, g * wg:(g + 1) * wg], preferred_element_type=f32)

    q_ref[0] = (proj(0) * (SCALE * LOG2E)).astype(bf16)
    k = proj(1)
    k_ref[0] = k
    kb_ref[0] = k.astype(bf16)
    v = proj(2)
    v_ref[0] = v
    vb_ref[0] = v.astype(bf16)
    za_ref[0] = proj(3)

    cu = proj(5) * proj(6)
    prev = carry_ref[...]
    row = lax.broadcasted_iota(jnp.int32, cu.shape, 0)
    cu_m1 = jnp.where(row == 0, prev[1:2], pltpu.roll(cu, 1, axis=0))
    cu_m2 = jnp.where(row == 0, prev[0:1],
                      jnp.where(row == 1, prev[1:2], pltpu.roll(cu, 2, axis=0)))
    cw = cw_ref[...]
    conv = cw[0:1] * cu_m2 + cw[1:2] * cu_m1 + cw[2:3] * cu
    yc_ref[0] = (_silu(proj(7)) * proj(4) * conv).astype(bf16)

    last = cu[tm - 2:tm]
    carry_ref[...] = last
    clast_ref[0] = last


def _in_proj(x, conv_init, norm_g, w_bf, conv_w, tm):
    b, l, d = x.shape
    w_all = w_bf.shape[1]
    wg = w_all // N_SPLIT
    nj = l // tm
    row_spec = lambda width: pl.BlockSpec((1, tm, width), lambda i, j: (i, j, 0))
    const = lambda shape: pl.BlockSpec(shape, lambda i, j: (0,) * len(shape))
    state_spec = pl.BlockSpec((1, CONV_W - 1, wg), lambda i, j: (i, 0, 0))
    out_shape = (
        jax.ShapeDtypeStruct((b, l, wg), bf16),
        jax.ShapeDtypeStruct((b, l, wg), f32),
        jax.ShapeDtypeStruct((b, l, wg), bf16),
        jax.ShapeDtypeStruct((b, l, wg), f32),
        jax.ShapeDtypeStruct((b, l, wg), bf16),
        jax.ShapeDtypeStruct((b, l, wg), f32),
        jax.ShapeDtypeStruct((b, l, wg), bf16),
        jax.ShapeDtypeStruct((b, CONV_W - 1, wg), f32),
    )
    return pl.pallas_call(
        functools.partial(_in_proj_kernel, tm=tm, wg=wg),
        name="in_proj",
        grid=(b, nj),
        in_specs=[row_spec(d), state_spec, const((1, d)), const((d, w_all)), const((CONV_W, wg))],
        out_specs=(row_spec(wg),) * 7 + (state_spec,),
        out_shape=out_shape,
        scratch_shapes=[pltpu.VMEM((CONV_W - 1, wg), f32)],
        compiler_params=pltpu.CompilerParams(
            dimension_semantics=("arbitrary", "arbitrary"), vmem_limit_bytes=VMEM_LIMIT),
    )(x, conv_init, norm_g.reshape(1, d), w_bf, conv_w)


def _rel_bucket(rel):
    nb = N_BUCKETS // 2
    ret = jnp.where(rel > 0, nb, 0)
    n = jnp.abs(rel)
    max_exact = nb // 2
    nf = jnp.maximum(n, 1).astype(f32)
    large = max_exact + (jnp.log(nf / max_exact) / math.log(MAX_DIST / max_exact)
                         * (nb - max_exact)).astype(jnp.int32)
    large = jnp.minimum(large, nb - 1)
    return ret + jnp.where(n < max_exact, n, large)


def _chunk_id_np(pos):
    return np.where(pos < 0, -1, pos // CHUNK)


def _assert_far_past(q_pos, k_pos):
    rel = k_pos[None, :] - q_pos[:, None]
    nb = N_BUCKETS // 2
    max_exact = nb // 2
    n = np.abs(rel).astype(np.float64)
    large = max_exact + np.log(n / max_exact) / math.log(MAX_DIST / max_exact) * (nb - max_exact)
    assert np.all(rel < 0) and np.all(large >= nb), "tile is not in the saturated bucket"
    assert np.all(_chunk_id_np(k_pos)[None, :] <= _chunk_id_np(q_pos)[:, None])


def _bias_kernel(bucket_ref, vis_ref, rb_ref, o_ref):
    h = pl.program_id(0)
    bucket = bucket_ref[...]
    far = rb_ref[FAR_BUCKET, h]
    acc = jnp.zeros(bucket.shape, f32)
    for b in range(N_BUCKETS):
        acc = jnp.where(bucket == b, rb_ref[b, h] - far, acc)
    o_ref[0] = jnp.where(vis_ref[...] != 0, acc * LOG2E, NEG)


def _build_bias(q_pos, k_pos, k_valid, rel_bias, keys_major=False):
    rel = k_pos[None, :] - q_pos[:, None]
    vis = (_chunk_id_np(k_pos)[None, :] <= _chunk_id_np(q_pos)[:, None]) & k_valid[None, :]
    if keys_major:
        rel, vis = rel.T, vis.T
    bucket = _rel_bucket(jnp.asarray(rel, jnp.int32))
    r, c = rel.shape
    full = pl.BlockSpec((r, c), lambda h: (0, 0))
    return pl.pallas_call(
        _bias_kernel,
        name="bias_tile",
        grid=(H_ATT,),
        in_specs=[full, full, pl.BlockSpec(memory_space=pltpu.SMEM)],
        out_specs=pl.BlockSpec((1, r, c), lambda h: (h, 0, 0)),
        out_shape=jax.ShapeDtypeStruct((H_ATT, r, c), f32),
    )(bucket, jnp.asarray(vis.astype(np.int32)), rel_bias)


def _diff_lambda(lam_ref):
    lv = lam_ref[...]
    s1 = jnp.sum(lv[0:1] * lv[1:2], axis=-1, keepdims=True)
    s2 = jnp.sum(lv[2:3] * lv[3:4], axis=-1, keepdims=True)
    return jnp.exp(s1) - jnp.exp(s2) + LAM_INIT


def _gate_heads(o1, o2, lam, sg, za):
    o = o1 - lam * o2
    return _rms(o, sg) * (1.0 - LAM_INIT) * _silu(za)


def _prompt_attn_kernel(q_ref, k_ref, v_ref, km_ref, vm_ref, bd_ref, bs_ref, bm_ref, za_ref, sg_ref,
                        lam_ref, o_ref, vt_sc, vmt_sc, sa_sc, sb_sc, m_sc, l_sc, acc_sc,
                        *, t, n_sub, nk):
    i = pl.program_id(2)
    tq = n_sub * t
    t2 = 2 * t

    @pl.when(i == 0)
    def _():
        for j in range(nk):
            vt_sc[j // 2, :, (j % 2) * t:(j % 2 + 1) * t] = (
                v_ref[0, j * t:(j + 1) * t, :].T.astype(bf16))
        vmt_sc[...] = vm_ref[...].T.astype(bf16)

    q = q_ref[0]
    lane = lax.broadcasted_iota(jnp.int32, q.shape, 1)
    zero = jnp.zeros_like(q)
    qs = (jnp.where(lane < D_QK, q, zero), jnp.where(lane >= D_QK, q, zero))

    m_sc[...] = jnp.full(m_sc.shape, NEG, f32)
    l_sc[...] = jnp.zeros(l_sc.shape, f32)
    acc_sc[...] = jnp.zeros(acc_sc.shape, f32)

    def qk(kt, c, p):
        return lax.dot_general(kt, qs[c][p * t:(p + 1) * t], (((1,), (1,)), ((), ())),
                               preferred_element_type=f32)

    def update(c, p, s, vt):
        cols = slice(p * t, (p + 1) * t)
        m_prev = m_sc[c, :, cols]
        m_new = jnp.maximum(m_prev, jnp.max(s, axis=0, keepdims=True))
        alpha = jnp.exp2(m_prev - m_new)
        e = jnp.exp2(s - m_new)
        l_sc[c, :, cols] = alpha * l_sc[c, :, cols] + jnp.sum(e, axis=0, keepdims=True)
        acc_sc[c, :, cols] = alpha * acc_sc[c, :, cols] + jnp.dot(
            vt, e.astype(bf16), preferred_element_type=f32)
        m_sc[c, :, cols] = m_new

    def qk_wide(w, s_sc):
        kt = k_ref[0, pl.ds(pl.multiple_of(w * t2, t2), t2), :]
        for c in range(2):
            for p in range(n_sub):
                s_sc[c, :, p * t:(p + 1) * t] = qk(kt, c, p)

    def update_wide(s_sc, w, bias0):
        vt = vt_sc[w]
        for c in range(2):
            for p in range(n_sub):
                s = s_sc[c, :, p * t:(p + 1) * t]
                update(c, p, s + bias0 if p == 0 and bias0 is not None else s, vt)

    n_wide = i * (n_sub // 2)
    km = km_ref[...]
    meta_scores = [[qk(km, c, p) for p in range(n_sub)] for c in range(2)]
    qk_wide(0, sa_sc)
    for c in range(2):
        for p in range(n_sub):
            bias = bm_ref[jnp.minimum(i, 1), 0] if p == 0 else bm_ref[1, 0]
            update(c, p, meta_scores[c][p] + bias, vmt_sc[...])

    def far_pair(pp, carry):
        w = 2 * pp
        qk_wide(w + 1, sb_sc)
        update_wide(sa_sc, w, None)
        qk_wide(w + 2, sa_sc)
        update_wide(sb_sc, w + 1, bs_ref[0, (w + 2 == n_wide).astype(jnp.int32)])
        return carry

    lax.fori_loop(0, i * (n_sub // 4), far_pair, 0)

    for r in range(2, n_sub):
        kt = k_ref[0, pl.ds(pl.multiple_of((n_sub * i + r) * t, t), t), :]
        for c in range(2):
            for p in range(r, n_sub):
                sb_sc[c, (r % 2) * t:(r % 2 + 1) * t, p * t:(p + 1) * t] = qk(kt, c, p)
    for r in range(n_sub):
        s_sc = sa_sc if r < 2 else sb_sc
        rows = slice((r % 2) * t, (r % 2 + 1) * t)
        vt = vt_sc[n_wide + r // 2, :, rows]
        for c in range(2):
            for p in range(r, n_sub):
                s = s_sc[c, rows, p * t:(p + 1) * t]
                if p <= r + 1:
                    s = s + bd_ref[p - r, 0]
                update(c, p, s, vt)

    o = acc_sc[0] / l_sc[0] - _diff_lambda(lam_ref) * (acc_sc[1] / l_sc[1])
    o = o * lax.rsqrt(jnp.mean(o * o, axis=0, keepdims=True) + EPS)
    o_ref[0] = (o.T * (sg_ref[...] * (1.0 - LAM_INIT)) * _silu(za_ref[0])).astype(bf16)


def _prompt_attn(q, kb, v, km, vm, bias_diag, bias_sub, bias_meta, za, subln_g, lam_vecs):
    b, l, _ = q.shape
    t = ATT_TILE
    tq = ATT_SUB * t
    assert ATT_SUB == 4 and l % tq == 0
    q_spec = pl.BlockSpec((1, tq, D_V), lambda bi, h, i: (bi, i, h))
    seq_spec = pl.BlockSpec((1, l, D_V), lambda bi, h, i: (bi, 0, h))
    meta_spec = pl.BlockSpec((LANES, D_V), lambda bi, h, i: (0, h))
    return pl.pallas_call(
        functools.partial(_prompt_attn_kernel, t=t, n_sub=ATT_SUB, nk=l // t),
        name="prompt_attn",
        grid=(b, H_ATT, l // tq),
        in_specs=[
            q_spec, seq_spec, seq_spec, meta_spec, meta_spec,
            pl.BlockSpec((2, 1, t, t), lambda bi, h, i: (0, h, 0, 0)),
            pl.BlockSpec((1, 2, 2 * t, t), lambda bi, h, i: (h, 0, 0, 0)),
            pl.BlockSpec((2, 1, LANES, t), lambda bi, h, i: (0, h, 0, 0)),
            q_spec,
            pl.BlockSpec((1, D_V), lambda bi, h, i: (0, 0)),
            pl.BlockSpec((4, D_QK), lambda bi, h, i: (0, 0)),
        ],
        out_specs=q_spec,
        out_shape=jax.ShapeDtypeStruct((b, l, W_ATT), bf16),
        scratch_shapes=[pltpu.VMEM((l // (2 * t), D_V, 2 * t), bf16),
                        pltpu.VMEM((D_V, LANES), bf16),
                        pltpu.VMEM((2, 2 * t, tq), f32), pltpu.VMEM((2, 2 * t, tq), f32),
                        pltpu.VMEM((2, 1, tq), f32), pltpu.VMEM((2, 1, tq), f32),
                        pltpu.VMEM((2, D_V, tq), f32)],
        compiler_params=pltpu.CompilerParams(
            dimension_semantics=("arbitrary", "arbitrary", "arbitrary"),
            vmem_limit_bytes=VMEM_LIMIT),
    )(q, kb, v, km, vm, bias_diag, bias_sub, bias_meta, za, subln_g, lam_vecs)


def _merge(og, yc, x, wo_ref, fg):
    y = x + jnp.dot(og, wo_ref[0:W_ATT, :], preferred_element_type=f32)
    y = y + jnp.dot(yc, wo_ref[W_ATT:, :], preferred_element_type=f32)
    return _rms(y, fg)


def _out_proj_kernel(og_ref, yc_ref, x_ref, wo_ref, fg_ref, y_ref):
    y_ref[...] = _merge(og_ref[...], yc_ref[...], x_ref[...], wo_ref, fg_ref[...])


def _out_proj(og, yc, x, wo_bf, final_g, tm):
    n, d = x.shape
    w_att = og.shape[1]
    w_conv = yc.shape[1]
    row_spec = lambda width: pl.BlockSpec((tm, width), lambda i: (i, 0))
    return pl.pallas_call(
        _out_proj_kernel,
        name="out_proj",
        grid=(n // tm,),
        in_specs=[row_spec(w_att), row_spec(w_conv), row_spec(d),
                  pl.BlockSpec((w_att + w_conv, d), lambda i: (0, 0)),
                  pl.BlockSpec((1, d), lambda i: (0, 0))],
        out_specs=row_spec(d),
        out_shape=jax.ShapeDtypeStruct((n, d), f32),
        compiler_params=pltpu.CompilerParams(
            dimension_semantics=("arbitrary",), vmem_limit_bytes=VMEM_LIMIT),
    )(og, yc, x, wo_bf, final_g.reshape(1, d))


def _small_attn_kernel(*refs, has_cache):
    if has_cache:
        (q_ref, kn_ref, vn_ref, bn_ref, ck_ref, cv_ref, bc_ref,
         za_ref, yc_ref, x_ref, sg_ref, lam_ref, wo_ref, fg_ref, y_ref) = refs
    else:
        (q_ref, kn_ref, vn_ref, bn_ref,
         za_ref, yc_ref, x_ref, sg_ref, lam_ref, wo_ref, fg_ref, y_ref) = refs
    q = q_ref[0]
    lane = lax.broadcasted_iota(jnp.int32, (q.shape[0], D_V), 1)
    lam = _diff_lambda(lam_ref)
    nt = (((1,), (1,)), ((), ()))
    heads = []
    for h in range(H_ATT):
        cols = slice(h * D_V, (h + 1) * D_V)
        qh = q[:, cols]
        zero = jnp.zeros_like(qh)
        kn = kn_ref[0][:, cols]
        vn = vn_ref[0][:, cols]
        if has_cache:
            ck = ck_ref[0][:, cols].astype(bf16)
            cv = cv_ref[0][:, cols].astype(bf16)
        outs = []
        for c in range(2):
            qc = jnp.where((lane < D_QK) if c == 0 else (lane >= D_QK), qh, zero)
            sn = lax.dot_general(qc, kn, nt, preferred_element_type=f32) + bn_ref[h]
            m = jnp.max(sn, axis=-1, keepdims=True)
            if has_cache:
                sc = lax.dot_general(qc, ck, nt, preferred_element_type=f32) + bc_ref[h]
                m = jnp.maximum(m, jnp.max(sc, axis=-1, keepdims=True))
            en = jnp.exp2(sn - m)
            l = jnp.sum(en, axis=-1, keepdims=True)
            acc = jnp.dot(en.astype(bf16), vn, preferred_element_type=f32)
            if has_cache:
                ec = jnp.exp2(sc - m)
                l = l + jnp.sum(ec, axis=-1, keepdims=True)
                acc = acc + jnp.dot(ec.astype(bf16), cv, preferred_element_type=f32)
            outs.append(acc / l)
        heads.append(_gate_heads(outs[0], outs[1], lam, sg_ref[...], za_ref[0][:, cols]))
    og = jnp.concatenate(heads, axis=-1).astype(bf16)
    y_ref[0] = _merge(og, yc_ref[0], x_ref[0], wo_ref, fg_ref[...])


def _small_attn(q, kb, vb, bias_new, za, yc, x, subln_g, lam_vecs, wo_bf, final_g,
                cache_k=None, cache_v=None, bias_cache=None):
    b, lq, d = x.shape
    has_cache = cache_k is not None
    per_b = lambda rows, width: pl.BlockSpec((1, rows, width), lambda i: (i, 0, 0))
    const = lambda shape: pl.BlockSpec(shape, lambda i: (0,) * len(shape))
    w = q.shape[2]
    args = [q, kb, vb, bias_new]
    specs = [per_b(lq, w), per_b(lq, w), per_b(lq, w), const(bias_new.shape)]
    if has_cache:
        lk = cache_k.shape[1]
        args += [cache_k, cache_v, bias_cache]
        specs += [per_b(lk, w), per_b(lk, w), const(bias_cache.shape)]
    args += [za, yc, x, subln_g, lam_vecs, wo_bf, final_g.reshape(1, d)]
    specs += [per_b(lq, w), per_b(lq, yc.shape[2]), per_b(lq, d), const(subln_g.shape),
              const(lam_vecs.shape), const(wo_bf.shape), const((1, d))]
    return pl.pallas_call(
        functools.partial(_small_attn_kernel, has_cache=has_cache),
        name="small_attn",
        grid=(b,),
        in_specs=specs,
        out_specs=per_b(lq, d),
        out_shape=jax.ShapeDtypeStruct((b, lq, d), f32),
        compiler_params=pltpu.CompilerParams(
            dimension_semantics=("arbitrary",), vmem_limit_bytes=VMEM_LIMIT),
    )(*args)


def kernel(x_prompt, x_sample, cache_k, cache_v, state_conv, meta_tokens, rel_bias, norm_g, w_in,
           conv_w, lambda_q1, lambda_k1, lambda_q2, lambda_k2, subln_g, w_out, final_g):
    bp, sp, d = x_prompt.shape
    bd, sd, _ = x_sample.shape
    depth = w_in.shape[0]
    assert depth == 1, "single-layer step only"
    past = cache_k.shape[2] - N_META
    t = ATT_TILE
    assert sp % (ATT_SUB * t) == 0 and t % CHUNK == 0

    w_bf = w_in[0].astype(bf16)
    wo_bf = w_out[0].astype(bf16)
    lam_vecs = jnp.stack([lambda_q1[0], lambda_k1[0], lambda_q2[0], lambda_k2[0]])
    sg = subln_g[0].reshape(1, D_V)
    wg = w_bf.shape[1] // N_SPLIT

    zeros_state = jnp.zeros((1, CONV_W - 1, wg), f32)
    (qm, km, kmb, vm, vmb, zam, ycm, cm) = _in_proj(
        meta_tokens[None], zeros_state, norm_g[0], w_bf, conv_w[0], tm=N_META)
    (qp, kp, kpb, vp, vpb, zap, ycp, cp) = _in_proj(
        x_prompt, jnp.broadcast_to(cm, (bp, CONV_W - 1, wg)), norm_g[0], w_bf, conv_w[0], tm=512)
    (qd, kd, kdb, vd, vdb, zad, ycd, cd) = _in_proj(
        x_sample, state_conv[0], norm_g[0], w_bf, conv_w[0], tm=sd)

    meta_pos = np.arange(-N_META, 0)
    tile_pos = np.arange(t)
    ones = lambda n: np.ones((n,), bool)
    bias_diag = jnp.stack([
        _build_bias(tile_pos + t, tile_pos + t, ones(t), rel_bias, keys_major=True),
        _build_bias(tile_pos + t, tile_pos, ones(t), rel_bias, keys_major=True)])
    for off in range(2, sp // t):
        _assert_far_past(tile_pos + off * t, tile_pos)
    bias_sub = _build_bias(tile_pos + 2 * t, np.arange(2 * t), ones(2 * t), rel_bias,
                           keys_major=True)
    bias_sub = jnp.stack([jnp.zeros_like(bias_sub), bias_sub], axis=1)
    meta_pad_pos = np.concatenate([meta_pos, np.full((LANES - N_META,), -1)])
    meta_valid = np.arange(LANES) < N_META
    bias_meta = jnp.stack([
        _build_bias(tile_pos, meta_pad_pos, meta_valid, rel_bias, keys_major=True),
        _build_bias(tile_pos + t, meta_pad_pos, meta_valid, rel_bias, keys_major=True)])
    for off in range(1, sp // t):
        _assert_far_past(tile_pos + off * t, meta_pos)

    pad_rows = lambda a: jnp.pad(a[0], ((0, LANES - N_META), (0, 0)))
    og = _prompt_attn(qp, kpb, vp, pad_rows(kmb), pad_rows(vm), bias_diag, bias_sub, bias_meta,
                      zap, sg, lam_vecs)
    y_prompt = _out_proj(og.reshape(bp * sp, W_ATT), ycp.reshape(bp * sp, wg),
                         x_prompt.reshape(bp * sp, d), wo_bf, final_g, tm=512).reshape(bp, sp, d)

    _small_attn(qm, kmb, vmb, _build_bias(meta_pos, meta_pos, ones(N_META), rel_bias),
                zam, ycm, meta_tokens[None], sg, lam_vecs, wo_bf, final_g)

    d_qpos = past + np.arange(sd)
    cache_pos = np.concatenate([meta_pos, np.arange(past)])
    y_sample = _small_attn(
        qd, kdb, vdb, _build_bias(d_qpos, d_qpos, ones(sd), rel_bias),
        zad, ycd, x_sample, sg, lam_vecs, wo_bf, final_g,
        cache_k=cache_k[0].reshape(bd, N_META + past, W_ATT),
        cache_v=cache_v[0].reshape(bd, N_META + past, W_ATT),
        bias_cache=_build_bias(d_qpos, cache_pos, ones(N_META + past), rel_bias))

    k_prompt = jnp.concatenate([jnp.broadcast_to(km, (bp, N_META, wg)), kp], axis=1)
    v_prompt = jnp.concatenate([jnp.broadcast_to(vm, (bp, N_META, wg)), vp], axis=1)
    return (y_prompt, y_sample,
            k_prompt.reshape(1, bp, N_META + sp, H_ATT, D_V),
            v_prompt.reshape(1, bp, N_META + sp, H_ATT, D_V),
            cp[None],
            kd.reshape(1, bd, sd, H_ATT, D_V),
            vd.reshape(1, bd, sd, H_ATT, D_V),
            cd[None])
```

```python
import functools
import math

import numpy as np
import jax
import jax.numpy as jnp
from jax import lax
from jax.experimental import pallas as pl
from jax.experimental.pallas import tpu as pltpu

f32 = jnp.float32
bf16 = jnp.bfloat16

CHUNK = 64
N_META = 16
H_ATT = 4
D_QK = 64
D_V = 2 * D_QK
W_ATT = H_ATT * D_V
CONV_W = 3
N_SPLIT = 8
N_BUCKETS = 32
MAX_DIST = 128
EPS = 1e-6
NEG = -1e30
SCALE = D_QK ** -0.5
LOG2E = math.log2(math.e)
LAM_INIT = 0.8 - 0.6 * math.exp(-0.3 * 0)
FAR_BUCKET = N_BUCKETS // 2 - 1

LANES = 128
ATT_TILE = 256
ATT_SUB = 4
VMEM_LIMIT = 48 * 1024 * 1024


def _silu(z):
    return z * (1.0 / (1.0 + jnp.exp(-z)))


def _rms(x, g):
    return x * lax.rsqrt(jnp.mean(x * x, axis=-1, keepdims=True) + EPS) * g


def _in_proj_kernel(x_ref, cinit_ref, g_ref, w_ref, cw_ref,
                    q_ref, k_ref, kb_ref, v_ref, vb_ref, gate_ref, yc_ref, clast_ref,
                    carry_ref, *, tm, wg, transpose_v):
    j = pl.program_id(1)

    @pl.when(j == 0)
    def _():
        carry_ref[...] = cinit_ref[0]

    x = x_ref[0]
    h = _rms(x, g_ref[...]).astype(bf16)

    def proj(g):
        return jnp.dot(h, w_ref[:, g * wg:(g + 1) * wg], preferred_element_type=f32)

    def store_heads(ref, a):
        for hd in range(H_ATT):
            ref[pl.ds(hd, tm, stride=H_ATT), :] = a[:, hd * D_V:(hd + 1) * D_V]

    q_ref[0] = (proj(0) * (SCALE * LOG2E)).astype(bf16)
    k = proj(1)
    store_heads(k_ref, k)
    kb_ref[0] = k.astype(bf16)
    v = proj(2)
    store_heads(v_ref, v)
    vb_ref[...] = (v.T if transpose_v else v).astype(bf16)
    gate_ref[0] = _silu(proj(3))

    cu = proj(5) * proj(6)
    prev = carry_ref[...]
    row = lax.broadcasted_iota(jnp.int32, cu.shape, 0)
    cu_m1 = jnp.where(row == 0, prev[1:2], pltpu.roll(cu, 1, axis=0))
    cu_m2 = jnp.where(row == 0, prev[0:1],
                      jnp.where(row == 1, prev[1:2], pltpu.roll(cu, 2, axis=0)))
    cw = cw_ref[...]
    conv = cw[0:1] * cu_m2 + cw[1:2] * cu_m1 + cw[2:3] * cu
    yc_ref[0] = (_silu(proj(7)) * proj(4) * conv).astype(bf16)

    last = cu[tm - 2:tm]
    carry_ref[...] = last
    clast_ref[0] = last


def _in_proj(x, conv_init, norm_g, w_bf, conv_w, tm, row_offset=0, transpose_v=False):
    b, l, d = x.shape
    w_all = w_bf.shape[1]
    wg = w_all // N_SPLIT
    assert wg == W_ATT
    nj = l // tm
    row_spec = lambda width: pl.BlockSpec((1, tm, width), lambda i, j: (i, j, 0))
    const = lambda shape: pl.BlockSpec(shape, lambda i, j: (0,) * len(shape))
    state_spec = pl.BlockSpec((1, CONV_W - 1, wg), lambda i, j: (i, 0, 0))
    if row_offset:
        cache_spec = pl.BlockSpec(
            (None, pl.Element(tm * H_ATT), pl.Element(D_V)),
            lambda i, j: (i, pl.multiple_of((row_offset + j * tm) * H_ATT, 8 * H_ATT), 0))
    else:
        cache_spec = pl.BlockSpec((None, tm * H_ATT, D_V), lambda i, j: (i, j, 0))
    if transpose_v:
        vb_spec = pl.BlockSpec((None, None, wg, tm), lambda i, j: (i, j, 0, 0))
        vb_shape = jax.ShapeDtypeStruct((b, nj, wg, tm), bf16)
    else:
        vb_spec = pl.BlockSpec((None, tm, wg), lambda i, j: (i, j, 0))
        vb_shape = jax.ShapeDtypeStruct((b, l, wg), bf16)
    cache_shape = jax.ShapeDtypeStruct((b, (row_offset + l) * H_ATT, D_V), f32)
    out_shape = (
        jax.ShapeDtypeStruct((b, l, wg), bf16),
        cache_shape,
        jax.ShapeDtypeStruct((b, l, wg), bf16),
        cache_shape,
        vb_shape,
        jax.ShapeDtypeStruct((b, l, wg), f32),
        jax.ShapeDtypeStruct((b, l, wg), bf16),
        jax.ShapeDtypeStruct((b, CONV_W - 1, wg), f32),
    )
    return pl.pallas_call(
        functools.partial(_in_proj_kernel, tm=tm, wg=wg, transpose_v=transpose_v),
        name="in_proj",
        grid=(b, nj),
        in_specs=[row_spec(d), state_spec, const((1, d)), const((d, w_all)), const((CONV_W, wg))],
        out_specs=(row_spec(wg), cache_spec, row_spec(wg), cache_spec, vb_spec, row_spec(wg),
                   row_spec(wg), state_spec),
        out_shape=out_shape,
        scratch_shapes=[pltpu.VMEM((CONV_W - 1, wg), f32)],
        compiler_params=pltpu.CompilerParams(
            dimension_semantics=("arbitrary", "arbitrary"), vmem_limit_bytes=VMEM_LIMIT),
    )(x, conv_init, norm_g.reshape(1, d), w_bf, conv_w)


def _fill_rows_kernel(km_ref, vm_ref, k_hbm, v_hbm, ko_ref, vo_ref):
    del k_hbm, v_hbm
    ko_ref[0] = km_ref[...]
    vo_ref[0] = vm_ref[...]


def _fill_meta_rows(k_meta, v_meta, k_all, v_all):
    b = k_all.shape[0]
    rows = k_meta.shape[0]
    small = pl.BlockSpec((rows, D_V), lambda i: (0, 0))
    lead = pl.BlockSpec((1, rows, D_V), lambda i: (i, 0, 0))
    whole = pl.BlockSpec(memory_space=pl.ANY)
    return pl.pallas_call(
        _fill_rows_kernel,
        name="fill_meta_rows",
        grid=(b,),
        in_specs=[small, small, whole, whole],
        out_specs=(lead, lead),
        out_shape=(jax.ShapeDtypeStruct(k_all.shape, k_all.dtype),
                   jax.ShapeDtypeStruct(v_all.shape, v_all.dtype)),
        input_output_aliases={2: 0, 3: 1},
    )(k_meta, v_meta, k_all, v_all)


def _rel_bucket(rel):
    nb = N_BUCKETS // 2
    ret = jnp.where(rel > 0, nb, 0)
    n = jnp.abs(rel)
    max_exact = nb // 2
    nf = jnp.maximum(n, 1).astype(f32)
    large = max_exact + (jnp.log(nf / max_exact) / math.log(MAX_DIST / max_exact)
                         * (nb - max_exact)).astype(jnp.int32)
    large = jnp.minimum(large, nb - 1)
    return ret + jnp.where(n < max_exact, n, large)


def _chunk_id_np(pos):
    return np.where(pos < 0, -1, pos // CHUNK)


def _assert_far_past(q_pos, k_pos):
    rel = k_pos[None, :] - q_pos[:, None]
    nb = N_BUCKETS // 2
    max_exact = nb // 2
    n = np.abs(rel).astype(np.float64)
    large = max_exact + np.log(n / max_exact) / math.log(MAX_DIST / max_exact) * (nb - max_exact)
    assert np.all(rel < 0) and np.all(large >= nb), "tile is not in the saturated bucket"
    assert np.all(_chunk_id_np(k_pos)[None, :] <= _chunk_id_np(q_pos)[:, None])


def _bias_kernel(bucket_ref, vis_ref, rb_ref, o_ref):
    h = pl.program_id(0)
    bucket = bucket_ref[...]
    far = rb_ref[FAR_BUCKET, h]
    acc = jnp.zeros(bucket.shape, f32)
    for b in range(N_BUCKETS):
        acc = jnp.where(bucket == b, rb_ref[b, h] - far, acc)
    o_ref[0] = jnp.where(vis_ref[...] != 0, acc * LOG2E, NEG)


def _build_bias(q_pos, k_pos, k_valid, rel_bias, keys_major=False):
    rel = k_pos[None, :] - q_pos[:, None]
    vis = (_chunk_id_np(k_pos)[None, :] <= _chunk_id_np(q_pos)[:, None]) & k_valid[None, :]
    if keys_major:
        rel, vis = rel.T, vis.T
    bucket = _rel_bucket(jnp.asarray(rel, jnp.int32))
    r, c = rel.shape
    full = pl.BlockSpec((r, c), lambda h: (0, 0))
    return pl.pallas_call(
        _bias_kernel,
        name="bias_tile",
        grid=(H_ATT,),
        in_specs=[full, full, pl.BlockSpec(memory_space=pltpu.SMEM)],
        out_specs=pl.BlockSpec((1, r, c), lambda h: (h, 0, 0)),
        out_shape=jax.ShapeDtypeStruct((H_ATT, r, c), f32),
    )(bucket, jnp.asarray(vis.astype(np.int32)), rel_bias)


def _diff_lambda(lam_ref):
    lv = lam_ref[...]
    s1 = jnp.sum(lv[0:1] * lv[1:2], axis=-1, keepdims=True)
    s2 = jnp.sum(lv[2:3] * lv[3:4], axis=-1, keepdims=True)
    return jnp.exp(s1) - jnp.exp(s2) + LAM_INIT


def _gate_heads(o1, o2, lam, sg, gate):
    o = o1 - lam * o2
    return _rms(o, sg) * (1.0 - LAM_INIT) * gate


def _prompt_attn_kernel(q_ref, k_ref, vt_ref, km_ref, vmt_ref, bd_ref, bs_ref, bm_ref, gate_ref,
                        sg_ref, lam_ref, o_ref, sa_sc, sb_sc, m_sc, l_sc, acc_sc, *, t, n_sub):
    i = pl.program_id(2)
    tq = n_sub * t
    t2 = 2 * t

    q = q_ref[0]
    lane = lax.broadcasted_iota(jnp.int32, q.shape, 1)
    zero = jnp.zeros_like(q)
    qs = (jnp.where(lane < D_QK, q, zero), jnp.where(lane >= D_QK, q, zero))

    m_sc[...] = jnp.full(m_sc.shape, NEG, f32)
    l_sc[...] = jnp.zeros(l_sc.shape, f32)
    acc_sc[...] = jnp.zeros(acc_sc.shape, f32)

    def qk(kt, c, p):
        return lax.dot_general(kt, qs[c][p * t:(p + 1) * t], (((1,), (1,)), ((), ())),
                               preferred_element_type=f32)

    def update(c, p, s, vt):
        cols = slice(p * t, (p + 1) * t)
        m_prev = m_sc[c, :, cols]
        m_new = jnp.maximum(m_prev, jnp.max(s, axis=0, keepdims=True))
        alpha = jnp.exp2(m_prev - m_new)
        e = jnp.exp2(s - m_new)
        l_sc[c, :, cols] = alpha * l_sc[c, :, cols] + jnp.sum(e, axis=0, keepdims=True)
        acc_sc[c, :, cols] = alpha * acc_sc[c, :, cols] + jnp.dot(
            vt, e.astype(bf16), preferred_element_type=f32)
        m_sc[c, :, cols] = m_new

    def qk_wide(w, s_sc):
        kt = k_ref[0, pl.ds(pl.multiple_of(w * t2, t2), t2), :]
        for c in range(2):
            for p in range(n_sub):
                s_sc[c, :, p * t:(p + 1) * t] = qk(kt, c, p)

    def update_wide(s_sc, w, bias0):
        vt = vt_ref[0, w]
        for c in range(2):
            for p in range(n_sub):
                s = s_sc[c, :, p * t:(p + 1) * t]
                update(c, p, s + bias0 if p == 0 and bias0 is not None else s, vt)

    n_wide = i * (n_sub // 2)
    km = km_ref[...]
    meta_scores = [[qk(km, c, p) for p in range(n_sub)] for c in range(2)]
    qk_wide(0, sa_sc)
    for c in range(2):
        for p in range(n_sub):
            bias = bm_ref[jnp.minimum(i, 1), 0] if p == 0 else bm_ref[1, 0]
            update(c, p, meta_scores[c][p] + bias, vmt_ref[...])

    def far_pair(pp, carry):
        w = 2 * pp
        qk_wide(w + 1, sb_sc)
        update_wide(sa_sc, w, None)
        qk_wide(w + 2, sa_sc)
        update_wide(sb_sc, w + 1, bs_ref[0, (w + 2 == n_wide).astype(jnp.int32)])
        return carry

    lax.fori_loop(0, i * (n_sub // 4), far_pair, 0)

    for r in range(2, n_sub):
        kt = k_ref[0, pl.ds(pl.multiple_of((n_sub * i + r) * t, t), t), :]
        for c in range(2):
            for p in range(r, n_sub):
                sb_sc[c, (r % 2) * t:(r % 2 + 1) * t, p * t:(p + 1) * t] = qk(kt, c, p)
    for r in range(n_sub):
        s_sc = sa_sc if r < 2 else sb_sc
        rows = slice((r % 2) * t, (r % 2 + 1) * t)
        vt = vt_ref[0, n_wide + r // 2, :, rows]
        for c in range(2):
            for p in range(r, n_sub):
                s = s_sc[c, rows, p * t:(p + 1) * t]
                if p <= r + 1:
                    s = s + bd_ref[p - r, 0]
                update(c, p, s, vt)

    o = acc_sc[0] / l_sc[0] - _diff_lambda(lam_ref) * (acc_sc[1] / l_sc[1])
    o = o * lax.rsqrt(jnp.mean(o * o, axis=0, keepdims=True) + EPS)
    o_ref[0] = (o.T * (sg_ref[...] * (1.0 - LAM_INIT)) * gate_ref[0]).astype(bf16)


def _prompt_attn(q, kb, vt, km, vmt, bias_diag, bias_sub, bias_meta, gate, subln_g, lam_vecs):
    b, l, _ = q.shape
    t = ATT_TILE
    tq = ATT_SUB * t
    assert ATT_SUB == 4 and l % tq == 0
    assert vt.shape == (b, l // (2 * t), W_ATT, 2 * t)
    q_spec = pl.BlockSpec((1, tq, D_V), lambda bi, h, i: (bi, i, h))
    seq_spec = pl.BlockSpec((1, l, D_V), lambda bi, h, i: (bi, 0, h))
    vt_spec = pl.BlockSpec((1, l // (2 * t), D_V, 2 * t), lambda bi, h, i: (bi, 0, h, 0))
    meta_spec = pl.BlockSpec((LANES, D_V), lambda bi, h, i: (0, h))
    vmt_spec = pl.BlockSpec((D_V, LANES), lambda bi, h, i: (h, 0))
    return pl.pallas_call(
        functools.partial(_prompt_attn_kernel, t=t, n_sub=ATT_SUB),
        name="prompt_attn",
        grid=(b, H_ATT, l // tq),
        in_specs=[
            q_spec, seq_spec, vt_spec, meta_spec, vmt_spec,
            pl.BlockSpec((2, 1, t, t), lambda bi, h, i: (0, h, 0, 0)),
            pl.BlockSpec((1, 2, 2 * t, t), lambda bi, h, i: (h, 0, 0, 0)),
            pl.BlockSpec((2, 1, LANES, t), lambda bi, h, i: (0, h, 0, 0)),
            q_spec,
            pl.BlockSpec((1, D_V), lambda bi, h, i: (0, 0)),
            pl.BlockSpec((4, D_QK), lambda bi, h, i: (0, 0)),
        ],
        out_specs=q_spec,
        out_shape=jax.ShapeDtypeStruct((b, l, W_ATT), bf16),
        scratch_shapes=[pltpu.VMEM((2, 2 * t, tq), f32), pltpu.VMEM((2, 2 * t, tq), f32),
                        pltpu.VMEM((2, 1, tq), f32), pltpu.VMEM((2, 1, tq), f32),
                        pltpu.VMEM((2, D_V, tq), f32)],
        compiler_params=pltpu.CompilerParams(
            dimension_semantics=("arbitrary", "arbitrary", "arbitrary"),
            vmem_limit_bytes=VMEM_LIMIT),
    )(q, kb, vt, km, vmt, bias_diag, bias_sub, bias_meta, gate, subln_g, lam_vecs)


def _merge(og, yc, x, wo_ref, fg):
    y = x + jnp.dot(og, wo_ref[0:W_ATT, :], preferred_element_type=f32)
    y = y + jnp.dot(yc, wo_ref[W_ATT:, :], preferred_element_type=f32)
    return _rms(y, fg)


def _out_proj_kernel(og_ref, yc_ref, x_ref, wo_ref, fg_ref, y_ref):
    y_ref[...] = _merge(og_ref[...], yc_ref[...], x_ref[...], wo_ref, fg_ref[...])


def _out_proj(og, yc, x, wo_bf, final_g, tm):
    n, d = x.shape
    w_att = og.shape[1]
    w_conv = yc.shape[1]
    row_spec = lambda width: pl.BlockSpec((tm, width), lambda i: (i, 0))
    return pl.pallas_call(
        _out_proj_kernel,
        name="out_proj",
        grid=(n // tm,),
        in_specs=[row_spec(w_att), row_spec(w_conv), row_spec(d),
                  pl.BlockSpec((w_att + w_conv, d), lambda i: (0, 0)),
                  pl.BlockSpec((1, d), lambda i: (0, 0))],
        out_specs=row_spec(d),
        out_shape=jax.ShapeDtypeStruct((n, d), f32),
        compiler_params=pltpu.CompilerParams(
            dimension_semantics=("arbitrary",), vmem_limit_bytes=VMEM_LIMIT),
    )(og, yc, x, wo_bf, final_g.reshape(1, d))


def _small_attn_kernel(*refs, has_cache):
    if has_cache:
        (q_ref, kn_ref, vn_ref, bn_ref, ck_ref, cv_ref, bc_ref,
         gate_ref, yc_ref, x_ref, sg_ref, lam_ref, wo_ref, fg_ref, y_ref) = refs
    else:
        (q_ref, kn_ref, vn_ref, bn_ref,
         gate_ref, yc_ref, x_ref, sg_ref, lam_ref, wo_ref, fg_ref, y_ref) = refs
    q = q_ref[0]
    lane = lax.broadcasted_iota(jnp.int32, (q.shape[0], D_V), 1)
    lam = _diff_lambda(lam_ref)
    nt = (((1,), (1,)), ((), ()))
    heads = []
    for h in range(H_ATT):
        cols = slice(h * D_V, (h + 1) * D_V)
        qh = q[:, cols]
        zero = jnp.zeros_like(qh)
        kn = kn_ref[0][:, cols]
        vn = vn_ref[0][:, cols]
        if has_cache:
            n_rows = ck_ref.shape[1] // H_ATT
            ck = ck_ref[0, pl.ds(h, n_rows, stride=H_ATT), :].astype(bf16)
            cv = cv_ref[0, pl.ds(h, n_rows, stride=H_ATT), :].astype(bf16)
        outs = []
        for c in range(2):
            qc = jnp.where((lane < D_QK) if c == 0 else (lane >= D_QK), qh, zero)
            sn = lax.dot_general(qc, kn, nt, preferred_element_type=f32) + bn_ref[h]
            m = jnp.max(sn, axis=-1, keepdims=True)
            if has_cache:
                sc = lax.dot_general(qc, ck, nt, preferred_element_type=f32) + bc_ref[h]
                m = jnp.maximum(m, jnp.max(sc, axis=-1, keepdims=True))
            en = jnp.exp2(sn - m)
            l = jnp.sum(en, axis=-1, keepdims=True)
            acc = jnp.dot(en.astype(bf16), vn, preferred_element_type=f32)
            if has_cache:
                ec = jnp.exp2(sc - m)
                l = l + jnp.sum(ec, axis=-1, keepdims=True)
                acc = acc + jnp.dot(ec.astype(bf16), cv, preferred_element_type=f32)
            outs.append(acc / l)
        heads.append(_gate_heads(outs[0], outs[1], lam, sg_ref[...], gate_ref[0][:, cols]))
    og = jnp.concatenate(heads, axis=-1).astype(bf16)
    y_ref[0] = _merge(og, yc_ref[0], x_ref[0], wo_ref, fg_ref[...])


def _small_attn(q, kb, vb, bias_new, gate, yc, x, subln_g, lam_vecs, wo_bf, final_g,
                cache_k=None, cache_v=None, bias_cache=None):
    b, lq, d = x.shape
    has_cache = cache_k is not None
    per_b = lambda rows, width: pl.BlockSpec((1, rows, width), lambda i: (i, 0, 0))
    const = lambda shape: pl.BlockSpec(shape, lambda i: (0,) * len(shape))
    w = q.shape[2]
    args = [q, kb, vb, bias_new]
    specs = [per_b(lq, w), per_b(lq, w), per_b(lq, w), const(bias_new.shape)]
    if has_cache:
        args += [cache_k, cache_v, bias_cache]
        specs += [per_b(*cache_k.shape[1:]), per_b(*cache_v.shape[1:]), const(bias_cache.shape)]
    args += [gate, yc, x, subln_g, lam_vecs, wo_bf, final_g.reshape(1, d)]
    specs += [per_b(lq, w), per_b(lq, yc.shape[2]), per_b(lq, d), const(subln_g.shape),
              const(lam_vecs.shape), const(wo_bf.shape), const((1, d))]
    return pl.pallas_call(
        functools.partial(_small_attn_kernel, has_cache=has_cache),
        name="small_attn",
        grid=(b,),
        in_specs=specs,
        out_specs=per_b(lq, d),
        out_shape=jax.ShapeDtypeStruct((b, lq, d), f32),
        compiler_params=pltpu.CompilerParams(
            dimension_semantics=("arbitrary",), vmem_limit_bytes=VMEM_LIMIT),
    )(*args)


def kernel(x_prompt, x_sample, cache_k, cache_v, state_conv, meta_tokens, rel_bias, norm_g, w_in,
           conv_w, lambda_q1, lambda_k1, lambda_q2, lambda_k2, subln_g, w_out, final_g):
    bp, sp, d = x_prompt.shape
    bd, sd, _ = x_sample.shape
    depth = w_in.shape[0]
    assert depth == 1, "single-layer step only"
    past = cache_k.shape[2] - N_META
    t = ATT_TILE
    assert sp % (ATT_SUB * t) == 0 and t % CHUNK == 0

    w_bf = w_in[0].astype(bf16)
    wo_bf = w_out[0].astype(bf16)
    lam_vecs = jnp.stack([lambda_q1[0], lambda_k1[0], lambda_q2[0], lambda_k2[0]])
    sg = subln_g[0].reshape(1, D_V)
    wg = w_bf.shape[1] // N_SPLIT

    zeros_state = jnp.zeros((1, CONV_W - 1, wg), f32)
    (qm, km, kmb, vm, vmb, gm, ycm, cm) = _in_proj(
        meta_tokens[None], zeros_state, norm_g[0], w_bf, conv_w[0], tm=N_META)
    (qp, kp, kpb, vp, vpt, gp, ycp, cp) = _in_proj(
        x_prompt, jnp.broadcast_to(cm, (bp, CONV_W - 1, wg)), norm_g[0], w_bf, conv_w[0],
        tm=2 * t, row_offset=N_META, transpose_v=True)
    kp, vp = _fill_meta_rows(km[0], vm[0], kp, vp)
    (qd, kd, kdb, vd, vdb, gd, ycd, cd) = _in_proj(
        x_sample, state_conv[0], norm_g[0], w_bf, conv_w[0], tm=sd)

    meta_pos = np.arange(-N_META, 0)
    tile_pos = np.arange(t)
    ones = lambda n: np.ones((n,), bool)
    bias_diag = jnp.stack([
        _build_bias(tile_pos + t, tile_pos + t, ones(t), rel_bias, keys_major=True),
        _build_bias(tile_pos + t, tile_pos, ones(t), rel_bias, keys_major=True)])
    for off in range(2, sp // t):
        _assert_far_past(tile_pos + off * t, tile_pos)
    bias_sub = _build_bias(tile_pos + 2 * t, np.arange(2 * t), ones(2 * t), rel_bias,
                           keys_major=True)
    bias_sub = jnp.stack([jnp.zeros_like(bias_sub), bias_sub], axis=1)
    meta_pad_pos = np.concatenate([meta_pos, np.full((LANES - N_META,), -1)])
    meta_valid = np.arange(LANES) < N_META
    bias_meta = jnp.stack([
        _build_bias(tile_pos, meta_pad_pos, meta_valid, rel_bias, keys_major=True),
        _build_bias(tile_pos + t, meta_pad_pos, meta_valid, rel_bias, keys_major=True)])
    for off in range(1, sp // t):
        _assert_far_past(tile_pos + off * t, meta_pos)

    km_pad = jnp.pad(kmb[0], ((0, LANES - N_META), (0, 0)))
    vmt_pad = jnp.pad(vmb[0].T, ((0, 0), (0, LANES - N_META)))
    og = _prompt_attn(qp, kpb, vpt, km_pad, vmt_pad, bias_diag, bias_sub, bias_meta,
                      gp, sg, lam_vecs)
    y_prompt = _out_proj(og.reshape(bp * sp, W_ATT), ycp.reshape(bp * sp, wg),
                         x_prompt.reshape(bp * sp, d), wo_bf, final_g, tm=512).reshape(bp, sp, d)

    _small_attn(qm, kmb, vmb, _build_bias(meta_pos, meta_pos, ones(N_META), rel_bias),
                gm, ycm, meta_tokens[None], sg, lam_vecs, wo_bf, final_g)

    d_qpos = past + np.arange(sd)
    cache_pos = np.concatenate([meta_pos, np.arange(past)])
    y_sample = _small_attn(
        qd, kdb, vdb, _build_bias(d_qpos, d_qpos, ones(sd), rel_bias),
        gd, ycd, x_sample, sg, lam_vecs, wo_bf, final_g,
        cache_k=cache_k[0].reshape(bd, (N_META + past) * H_ATT, D_V),
        cache_v=cache_v[0].reshape(bd, (N_META + past) * H_ATT, D_V),
        bias_cache=_build_bias(d_qpos, cache_pos, ones(N_META + past), rel_bias))

    return (y_prompt, y_sample,
            kp.reshape(1, bp, N_META + sp, H_ATT, D_V),
            vp.reshape(1, bp, N_META + sp, H_ATT, D_V),
            cp[None],
            kd.reshape(1, bd, sd, H_ATT, D_V),
            vd.reshape(1, bd, sd, H_ATT, D_V),
            cd[None])
```

```python
import functools
import math

import numpy as np
import jax
import jax.numpy as jnp
from jax import lax
from jax.experimental import pallas as pl
from jax.experimental.pallas import tpu as pltpu

f32 = jnp.float32
bf16 = jnp.bfloat16

CHUNK = 64
N_META = 16
H_ATT = 4
D_QK = 64
D_V = 2 * D_QK
W_ATT = H_ATT * D_V
CONV_W = 3
N_SPLIT = 8
N_BUCKETS = 32
MAX_DIST = 128
EPS = 1e-6
NEG = -1e30
SCALE = D_QK ** -0.5
LOG2E = math.log2(math.e)
LAM_INIT = 0.8 - 0.6 * math.exp(-0.3 * 0)
FAR_BUCKET = N_BUCKETS // 2 - 1

LANES = 128
ATT_TILE = 256
ATT_SUB = 4
VMEM_LIMIT = 48 * 1024 * 1024


def _silu(z):
    return z * (1.0 / (1.0 + jnp.exp(-z)))


def _rms(x, g):
    return x * lax.rsqrt(jnp.mean(x * x, axis=-1, keepdims=True) + EPS) * g


def _in_proj_kernel(x_ref, cinit_ref, g_ref, w_ref, cw_ref,
                    q_ref, k_ref, kb_ref, v_ref, vb_ref, gate_ref, yc_ref, clast_ref,
                    carry_ref, *, tm, wg, transpose_v):
    j = pl.program_id(1)

    @pl.when(j == 0)
    def _():
        carry_ref[...] = cinit_ref[0]

    x = x_ref[0]
    h = _rms(x, g_ref[...]).astype(bf16)

    def proj(g):
        return jnp.dot(h, w_ref[:, g * wg:(g + 1) * wg], preferred_element_type=f32)

    def store_heads(ref, a):
        for hd in range(H_ATT):
            ref[pl.ds(hd, tm, stride=H_ATT), :] = a[:, hd * D_V:(hd + 1) * D_V]

    q_ref[0] = (proj(0) * (SCALE * LOG2E)).astype(bf16)
    k = proj(1)
    store_heads(k_ref, k)
    kb_ref[0] = k.astype(bf16)
    v = proj(2)
    store_heads(v_ref, v)
    vb_ref[...] = (v.T if transpose_v else v).astype(bf16)
    gate_ref[0] = _silu(proj(3))

    cu = proj(5) * proj(6)
    prev = carry_ref[...]
    row = lax.broadcasted_iota(jnp.int32, cu.shape, 0)
    cu_m1 = jnp.where(row == 0, prev[1:2], pltpu.roll(cu, 1, axis=0))
    cu_m2 = jnp.where(row == 0, prev[0:1],
                      jnp.where(row == 1, prev[1:2], pltpu.roll(cu, 2, axis=0)))
    cw = cw_ref[...]
    conv = cw[0:1] * cu_m2 + cw[1:2] * cu_m1 + cw[2:3] * cu
    yc_ref[0] = (_silu(proj(7)) * proj(4) * conv).astype(bf16)

    last = cu[tm - 2:tm]
    carry_ref[...] = last
    clast_ref[0] = last


def _in_proj(x, conv_init, norm_g, w_bf, conv_w, tm, row_offset=0, transpose_v=False):
    b, l, d = x.shape
    w_all = w_bf.shape[1]
    wg = w_all // N_SPLIT
    assert wg == W_ATT
    nj = l // tm
    row_spec = lambda width: pl.BlockSpec((1, tm, width), lambda i, j: (i, j, 0))
    const = lambda shape: pl.BlockSpec(shape, lambda i, j: (0,) * len(shape))
    state_spec = pl.BlockSpec((1, CONV_W - 1, wg), lambda i, j: (i, 0, 0))
    if row_offset:
        cache_spec = pl.BlockSpec(
            (None, pl.Element(tm * H_ATT), pl.Element(D_V)),
            lambda i, j: (i, pl.multiple_of((row_offset + j * tm) * H_ATT, 8 * H_ATT), 0))
    else:
        cache_spec = pl.BlockSpec((None, tm * H_ATT, D_V), lambda i, j: (i, j, 0))
    if transpose_v:
        vb_spec = pl.BlockSpec((None, None, wg, tm), lambda i, j: (i, j, 0, 0))
        vb_shape = jax.ShapeDtypeStruct((b, nj, wg, tm), bf16)
    else:
        vb_spec = pl.BlockSpec((None, tm, wg), lambda i, j: (i, j, 0))
        vb_shape = jax.ShapeDtypeStruct((b, l, wg), bf16)
    cache_shape = jax.ShapeDtypeStruct((b, (row_offset + l) * H_ATT, D_V), f32)
    out_shape = (
        jax.ShapeDtypeStruct((b, l, wg), bf16),
        cache_shape,
        jax.ShapeDtypeStruct((b, l, wg), bf16),
        cache_shape,
        vb_shape,
        jax.ShapeDtypeStruct((b, l, wg), f32),
        jax.ShapeDtypeStruct((b, l, wg), bf16),
        jax.ShapeDtypeStruct((b, CONV_W - 1, wg), f32),
    )
    return pl.pallas_call(
        functools.partial(_in_proj_kernel, tm=tm, wg=wg, transpose_v=transpose_v),
        name="in_proj",
        grid=(b, nj),
        in_specs=[row_spec(d), state_spec, const((1, d)), const((d, w_all)), const((CONV_W, wg))],
        out_specs=(row_spec(wg), cache_spec, row_spec(wg), cache_spec, vb_spec, row_spec(wg),
                   row_spec(wg), state_spec),
        out_shape=out_shape,
        scratch_shapes=[pltpu.VMEM((CONV_W - 1, wg), f32)],
        compiler_params=pltpu.CompilerParams(
            dimension_semantics=("arbitrary", "arbitrary"), vmem_limit_bytes=VMEM_LIMIT),
    )(x, conv_init, norm_g.reshape(1, d), w_bf, conv_w)


def _fill_rows_kernel(km_ref, vm_ref, k_hbm, v_hbm, ko_ref, vo_ref):
    del k_hbm, v_hbm
    ko_ref[0] = km_ref[...]
    vo_ref[0] = vm_ref[...]


def _fill_meta_rows(k_meta, v_meta, k_all, v_all):
    b = k_all.shape[0]
    rows = k_meta.shape[0]
    small = pl.BlockSpec((rows, D_V), lambda i: (0, 0))
    lead = pl.BlockSpec((1, rows, D_V), lambda i: (i, 0, 0))
    whole = pl.BlockSpec(memory_space=pl.ANY)
    return pl.pallas_call(
        _fill_rows_kernel,
        name="fill_meta_rows",
        grid=(b,),
        in_specs=[small, small, whole, whole],
        out_specs=(lead, lead),
        out_shape=(jax.ShapeDtypeStruct(k_all.shape, k_all.dtype),
                   jax.ShapeDtypeStruct(v_all.shape, v_all.dtype)),
        input_output_aliases={2: 0, 3: 1},
    )(k_meta, v_meta, k_all, v_all)


def _rel_bucket(rel):
    nb = N_BUCKETS // 2
    ret = jnp.where(rel > 0, nb, 0)
    n = jnp.abs(rel)
    max_exact = nb // 2
    nf = jnp.maximum(n, 1).astype(f32)
    large = max_exact + (jnp.log(nf / max_exact) / math.log(MAX_DIST / max_exact)
                         * (nb - max_exact)).astype(jnp.int32)
    large = jnp.minimum(large, nb - 1)
    return ret + jnp.where(n < max_exact, n, large)


def _chunk_id_np(pos):
    return np.where(pos < 0, -1, pos // CHUNK)


def _assert_far_past(q_pos, k_pos):
    rel = k_pos[None, :] - q_pos[:, None]
    nb = N_BUCKETS // 2
    max_exact = nb // 2
    n = np.abs(rel).astype(np.float64)
    large = max_exact + np.log(n / max_exact) / math.log(MAX_DIST / max_exact) * (nb - max_exact)
    assert np.all(rel < 0) and np.all(large >= nb), "tile is not in the saturated bucket"
    assert np.all(_chunk_id_np(k_pos)[None, :] <= _chunk_id_np(q_pos)[:, None])


def _bias_kernel(bucket_ref, vis_ref, rb_ref, o_ref):
    h = pl.program_id(0)
    bucket = bucket_ref[...]
    far = rb_ref[FAR_BUCKET, h]
    acc = jnp.zeros(bucket.shape, f32)
    for b in range(N_BUCKETS):
        acc = jnp.where(bucket == b, rb_ref[b, h] - far, acc)
    o_ref[0] = jnp.where(vis_ref[...] != 0, acc * LOG2E, NEG)


def _build_bias(q_pos, k_pos, k_valid, rel_bias, keys_major=False):
    rel = k_pos[None, :] - q_pos[:, None]
    vis = (_chunk_id_np(k_pos)[None, :] <= _chunk_id_np(q_pos)[:, None]) & k_valid[None, :]
    if keys_major:
        rel, vis = rel.T, vis.T
    bucket = _rel_bucket(jnp.asarray(rel, jnp.int32))
    r, c = rel.shape
    full = pl.BlockSpec((r, c), lambda h: (0, 0))
    return pl.pallas_call(
        _bias_kernel,
        name="bias_tile",
        grid=(H_ATT,),
        in_specs=[full, full, pl.BlockSpec(memory_space=pltpu.SMEM)],
        out_specs=pl.BlockSpec((1, r, c), lambda h: (h, 0, 0)),
        out_shape=jax.ShapeDtypeStruct((H_ATT, r, c), f32),
    )(bucket, jnp.asarray(vis.astype(np.int32)), rel_bias)


def _diff_lambda(lam_ref):
    lv = lam_ref[...]
    s1 = jnp.sum(lv[0:1] * lv[1:2], axis=-1, keepdims=True)
    s2 = jnp.sum(lv[2:3] * lv[3:4], axis=-1, keepdims=True)
    return jnp.exp(s1) - jnp.exp(s2) + LAM_INIT


def _gate_heads(o1, o2, lam, sg, gate):
    o = o1 - lam * o2
    return _rms(o, sg) * (1.0 - LAM_INIT) * gate


def _prompt_attn_kernel(q_ref, k_ref, vt_ref, km_ref, vmt_ref, bd_ref, bs_ref, bm_ref, gate_ref,
                        sg_ref, lam_ref, o_ref, sa_sc, sb_sc, mx_sc, m_sc, l_sc, acc_sc,
                        *, t, n_sub):
    i = pl.program_id(2)
    tq = n_sub * t
    t2 = 2 * t

    q = q_ref[0]
    lane = lax.broadcasted_iota(jnp.int32, q.shape, 1)
    zero = jnp.zeros_like(q)
    qs = (jnp.where(lane < D_QK, q, zero), jnp.where(lane >= D_QK, q, zero))

    m_sc[...] = jnp.full(m_sc.shape, NEG, f32)
    l_sc[...] = jnp.zeros(l_sc.shape, f32)
    acc_sc[...] = jnp.zeros(acc_sc.shape, f32)

    def qk(kt, c, p):
        return lax.dot_general(kt, qs[c][p * t:(p + 1) * t], (((1,), (1,)), ((), ())),
                               preferred_element_type=f32)

    def update(c, p, s, vt, s_max=None):
        cols = slice(p * t, (p + 1) * t)
        if s_max is None:
            s_max = jnp.max(s, axis=0, keepdims=True)
        m_prev = m_sc[c, :, cols]
        m_new = jnp.maximum(m_prev, s_max)
        alpha = jnp.exp2(m_prev - m_new)
        e = jnp.exp2(s - m_new)
        l_sc[c, :, cols] = alpha * l_sc[c, :, cols] + jnp.sum(e, axis=0, keepdims=True)
        acc_sc[c, :, cols] = alpha * acc_sc[c, :, cols] + jnp.dot(
            vt, e.astype(bf16), preferred_element_type=f32)
        m_sc[c, :, cols] = m_new

    def refill(kt, c, p, bias=None):
        cols = slice(p * t, (p + 1) * t)
        s = qk(kt, c, p)
        if bias is not None:
            s = s + bias
        sa_sc[c, :, cols] = s
        mx_sc[c, :, cols] = jnp.max(s, axis=0, keepdims=True)

    def wide_keys(w):
        return k_ref[0, pl.ds(pl.multiple_of(w * t2, t2), t2), :]

    n_wide = i * (n_sub // 2)
    km = km_ref[...]
    meta_scores = [[qk(km, c, p) for p in range(n_sub)] for c in range(2)]
    kt0 = wide_keys(0)
    for c in range(2):
        for p in range(n_sub):
            refill(kt0, c, p)
    for c in range(2):
        for p in range(n_sub):
            bias = bm_ref[jnp.minimum(i, 1), 0] if p == 0 else bm_ref[1, 0]
            update(c, p, meta_scores[c][p] + bias, vmt_ref[...])

    def far_tile(w, next_may_be_last):
        vt = vt_ref[0, w]
        kt = wide_keys(w + 1)
        for c in range(2):
            for p in range(n_sub):
                cols = slice(p * t, (p + 1) * t)
                update(c, p, sa_sc[c, :, cols], vt, mx_sc[c, :, cols])
                bias = None
                if p == 0 and next_may_be_last:
                    bias = bs_ref[0, (w + 2 == n_wide).astype(jnp.int32)]
                refill(kt, c, p, bias)

    def far_pair(pp, carry):
        far_tile(2 * pp, True)
        far_tile(2 * pp + 1, False)
        return carry

    lax.fori_loop(0, i * (n_sub // 4), far_pair, 0)

    for r in range(2, n_sub):
        kt = k_ref[0, pl.ds(pl.multiple_of((n_sub * i + r) * t, t), t), :]
        for c in range(2):
            for p in range(r, n_sub):
                sb_sc[c, (r % 2) * t:(r % 2 + 1) * t, p * t:(p + 1) * t] = qk(kt, c, p)
    for r in range(n_sub):
        s_sc = sa_sc if r < 2 else sb_sc
        rows = slice((r % 2) * t, (r % 2 + 1) * t)
        vt = vt_ref[0, n_wide + r // 2, :, rows]
        for c in range(2):
            for p in range(r, n_sub):
                s = s_sc[c, rows, p * t:(p + 1) * t]
                if p <= r + 1:
                    s = s + bd_ref[p - r, 0]
                update(c, p, s, vt)

    o = acc_sc[0] / l_sc[0] - _diff_lambda(lam_ref) * (acc_sc[1] / l_sc[1])
    o = o * lax.rsqrt(jnp.mean(o * o, axis=0, keepdims=True) + EPS)
    o_ref[0] = (o.T * (sg_ref[...] * (1.0 - LAM_INIT)) * gate_ref[0]).astype(bf16)


def _prompt_attn(q, kb, vt, km, vmt, bias_diag, bias_sub, bias_meta, gate, subln_g, lam_vecs):
    b, l, _ = q.shape
    t = ATT_TILE
    tq = ATT_SUB * t
    assert ATT_SUB == 4 and l % tq == 0
    assert vt.shape == (b, l // (2 * t), W_ATT, 2 * t)
    q_spec = pl.BlockSpec((1, tq, D_V), lambda bi, h, i: (bi, i, h))
    seq_spec = pl.BlockSpec((1, l, D_V), lambda bi, h, i: (bi, 0, h))
    vt_spec = pl.BlockSpec((1, l // (2 * t), D_V, 2 * t), lambda bi, h, i: (bi, 0, h, 0))
    meta_spec = pl.BlockSpec((LANES, D_V), lambda bi, h, i: (0, h))
    vmt_spec = pl.BlockSpec((D_V, LANES), lambda bi, h, i: (h, 0))
    return pl.pallas_call(
        functools.partial(_prompt_attn_kernel, t=t, n_sub=ATT_SUB),
        name="prompt_attn",
        grid=(b, H_ATT, l // tq),
        in_specs=[
            q_spec, seq_spec, vt_spec, meta_spec, vmt_spec,
            pl.BlockSpec((2, 1, t, t), lambda bi, h, i: (0, h, 0, 0)),
            pl.BlockSpec((1, 2, 2 * t, t), lambda bi, h, i: (h, 0, 0, 0)),
            pl.BlockSpec((2, 1, LANES, t), lambda bi, h, i: (0, h, 0, 0)),
            q_spec,
            pl.BlockSpec((1, D_V), lambda bi, h, i: (0, 0)),
            pl.BlockSpec((4, D_QK), lambda bi, h, i: (0, 0)),
        ],
        out_specs=q_spec,
        out_shape=jax.ShapeDtypeStruct((b, l, W_ATT), bf16),
        scratch_shapes=[pltpu.VMEM((2, 2 * t, tq), f32), pltpu.VMEM((2, 2 * t, tq), f32),
                        pltpu.VMEM((2, 1, tq), f32),
                        pltpu.VMEM((2, 1, tq), f32), pltpu.VMEM((2, 1, tq), f32),
                        pltpu.VMEM((2, D_V, tq), f32)],
        compiler_params=pltpu.CompilerParams(
            dimension_semantics=("arbitrary", "arbitrary", "arbitrary"),
            vmem_limit_bytes=VMEM_LIMIT),
    )(q, kb, vt, km, vmt, bias_diag, bias_sub, bias_meta, gate, subln_g, lam_vecs)


def _merge(og, yc, x, wo_ref, fg):
    y = x + jnp.dot(og, wo_ref[0:W_ATT, :], preferred_element_type=f32)
    y = y + jnp.dot(yc, wo_ref[W_ATT:, :], preferred_element_type=f32)
    return _rms(y, fg)


def _out_proj_kernel(og_ref, yc_ref, x_ref, wo_ref, fg_ref, y_ref):
    y_ref[...] = _merge(og_ref[...], yc_ref[...], x_ref[...], wo_ref, fg_ref[...])


def _out_proj(og, yc, x, wo_bf, final_g, tm):
    n, d = x.shape
    w_att = og.shape[1]
    w_conv = yc.shape[1]
    row_spec = lambda width: pl.BlockSpec((tm, width), lambda i: (i, 0))
    return pl.pallas_call(
        _out_proj_kernel,
        name="out_proj",
        grid=(n // tm,),
        in_specs=[row_spec(w_att), row_spec(w_conv), row_spec(d),
                  pl.BlockSpec((w_att + w_conv, d), lambda i: (0, 0)),
                  pl.BlockSpec((1, d), lambda i: (0, 0))],
        out_specs=row_spec(d),
        out_shape=jax.ShapeDtypeStruct((n, d), f32),
        compiler_params=pltpu.CompilerParams(
            dimension_semantics=("arbitrary",), vmem_limit_bytes=VMEM_LIMIT),
    )(og, yc, x, wo_bf, final_g.reshape(1, d))


def _small_attn_kernel(*refs, has_cache):
    if has_cache:
        (q_ref, kn_ref, vn_ref, bn_ref, ck_ref, cv_ref, bc_ref,
         gate_ref, yc_ref, x_ref, sg_ref, lam_ref, wo_ref, fg_ref, y_ref) = refs
    else:
        (q_ref, kn_ref, vn_ref, bn_ref,
         gate_ref, yc_ref, x_ref, sg_ref, lam_ref, wo_ref, fg_ref, y_ref) = refs
    q = q_ref[0]
    lane = lax.broadcasted_iota(jnp.int32, (q.shape[0], D_V), 1)
    lam = _diff_lambda(lam_ref)
    nt = (((1,), (1,)), ((), ()))
    heads = []
    for h in range(H_ATT):
        cols = slice(h * D_V, (h + 1) * D_V)
        qh = q[:, cols]
        zero = jnp.zeros_like(qh)
        kn = kn_ref[0][:, cols]
        vn = vn_ref[0][:, cols]
        if has_cache:
            n_rows = ck_ref.shape[1] // H_ATT
            ck = ck_ref[0, pl.ds(h, n_rows, stride=H_ATT), :].astype(bf16)
            cv = cv_ref[0, pl.ds(h, n_rows, stride=H_ATT), :].astype(bf16)
        outs = []
        for c in range(2):
            qc = jnp.where((lane < D_QK) if c == 0 else (lane >= D_QK), qh, zero)
            sn = lax.dot_general(qc, kn, nt, preferred_element_type=f32) + bn_ref[h]
            m = jnp.max(sn, axis=-1, keepdims=True)
            if has_cache:
                sc = lax.dot_general(qc, ck, nt, preferred_element_type=f32) + bc_ref[h]
                m = jnp.maximum(m, jnp.max(sc, axis=-1, keepdims=True))
            en = jnp.exp2(sn - m)
            l = jnp.sum(en, axis=-1, keepdims=True)
            acc = jnp.dot(en.astype(bf16), vn, preferred_element_type=f32)
            if has_cache:
                ec = jnp.exp2(sc - m)
                l = l + jnp.sum(ec, axis=-1, keepdims=True)
                acc = acc + jnp.dot(ec.astype(bf16), cv, preferred_element_type=f32)
            outs.append(acc / l)
        heads.append(_gate_heads(outs[0], outs[1], lam, sg_ref[...], gate_ref[0][:, cols]))
    og = jnp.concatenate(heads, axis=-1).astype(bf16)
    y_ref[0] = _merge(og, yc_ref[0], x_ref[0], wo_ref, fg_ref[...])


def _small_attn(q, kb, vb, bias_new, gate, yc, x, subln_g, lam_vecs, wo_bf, final_g,
                cache_k=None, cache_v=None, bias_cache=None):
    b, lq, d = x.shape
    has_cache = cache_k is not None
    per_b = lambda rows, width: pl.BlockSpec((1, rows, width), lambda i: (i, 0, 0))
    const = lambda shape: pl.BlockSpec(shape, lambda i: (0,) * len(shape))
    w = q.shape[2]
    args = [q, kb, vb, bias_new]
    specs = [per_b(lq, w), per_b(lq, w), per_b(lq, w), const(bias_new.shape)]
    if has_cache:
        args += [cache_k, cache_v, bias_cache]
        specs += [per_b(*cache_k.shape[1:]), per_b(*cache_v.shape[1:]), const(bias_cache.shape)]
    args += [gate, yc, x, subln_g, lam_vecs, wo_bf, final_g.reshape(1, d)]
    specs += [per_b(lq, w), per_b(lq, yc.shape[2]), per_b(lq, d), const(subln_g.shape),
              const(lam_vecs.shape), const(wo_bf.shape), const((1, d))]
    return pl.pallas_call(
        functools.partial(_small_attn_kernel, has_cache=has_cache),
        name="small_attn",
        grid=(b,),
        in_specs=specs,
        out_specs=per_b(lq, d),
        out_shape=jax.ShapeDtypeStruct((b, lq, d), f32),
        compiler_params=pltpu.CompilerParams(
            dimension_semantics=("arbitrary",), vmem_limit_bytes=VMEM_LIMIT),
    )(*args)


def kernel(x_prompt, x_sample, cache_k, cache_v, state_conv, meta_tokens, rel_bias, norm_g, w_in,
           conv_w, lambda_q1, lambda_k1, lambda_q2, lambda_k2, subln_g, w_out, final_g):
    bp, sp, d = x_prompt.shape
    bd, sd, _ = x_sample.shape
    depth = w_in.shape[0]
    assert depth == 1, "single-layer step only"
    past = cache_k.shape[2] - N_META
    t = ATT_TILE
    assert sp % (ATT_SUB * t) == 0 and t % CHUNK == 0

    w_bf = w_in[0].astype(bf16)
    wo_bf = w_out[0].astype(bf16)
    lam_vecs = jnp.stack([lambda_q1[0], lambda_k1[0], lambda_q2[0], lambda_k2[0]])
    sg = subln_g[0].reshape(1, D_V)
    wg = w_bf.shape[1] // N_SPLIT

    zeros_state = jnp.zeros((1, CONV_W - 1, wg), f32)
    (qm, km, kmb, vm, vmb, gm, ycm, cm) = _in_proj(
        meta_tokens[None], zeros_state, norm_g[0], w_bf, conv_w[0], tm=N_META)
    (qp, kp, kpb, vp, vpt, gp, ycp, cp) = _in_proj(
        x_prompt, jnp.broadcast_to(cm, (bp, CONV_W - 1, wg)), norm_g[0], w_bf, conv_w[0],
        tm=2 * t, row_offset=N_META, transpose_v=True)
    kp, vp = _fill_meta_rows(km[0], vm[0], kp, vp)
    (qd, kd, kdb, vd, vdb, gd, ycd, cd) = _in_proj(
        x_sample, state_conv[0], norm_g[0], w_bf, conv_w[0], tm=sd)

    meta_pos = np.arange(-N_META, 0)
    tile_pos = np.arange(t)
    ones = lambda n: np.ones((n,), bool)
    bias_diag = jnp.stack([
        _build_bias(tile_pos + t, tile_pos + t, ones(t), rel_bias, keys_major=True),
        _build_bias(tile_pos + t, tile_pos, ones(t), rel_bias, keys_major=True)])
    for off in range(2, sp // t):
        _assert_far_past(tile_pos + off * t, tile_pos)
    bias_sub = _build_bias(tile_pos + 2 * t, np.arange(2 * t), ones(2 * t), rel_bias,
                           keys_major=True)
    bias_sub = jnp.stack([jnp.zeros_like(bias_sub), bias_sub], axis=1)
    meta_pad_pos = np.concatenate([meta_pos, np.full((LANES - N_META,), -1)])
    meta_valid = np.arange(LANES) < N_META
    bias_meta = jnp.stack([
        _build_bias(tile_pos, meta_pad_pos, meta_valid, rel_bias, keys_major=True),
        _build_bias(tile_pos + t, meta_pad_pos, meta_valid, rel_bias, keys_major=True)])
    for off in range(1, sp // t):
        _assert_far_past(tile_pos + off * t, meta_pos)

    km_pad = jnp.pad(kmb[0], ((0, LANES - N_META), (0, 0)))
    vmt_pad = jnp.pad(vmb[0].T, ((0, 0), (0, LANES - N_META)))
    og = _prompt_attn(qp, kpb, vpt, km_pad, vmt_pad, bias_diag, bias_sub, bias_meta,
                      gp, sg, lam_vecs)
    y_prompt = _out_proj(og.reshape(bp * sp, W_ATT), ycp.reshape(bp * sp, wg),
                         x_prompt.reshape(bp * sp, d), wo_bf, final_g, tm=512).reshape(bp, sp, d)

    _small_attn(qm, kmb, vmb, _build_bias(meta_pos, meta_pos, ones(N_META), rel_bias),
                gm, ycm, meta_tokens[None], sg, lam_vecs, wo_bf, final_g)

    d_qpos = past + np.arange(sd)
    cache_pos = np.concatenate([meta_pos, np.arange(past)])
    y_sample = _small_attn(
        qd, kdb, vdb, _build_bias(d_qpos, d_qpos, ones(sd), rel_bias),
        gd, ycd, x_sample, sg, lam_vecs, wo_bf, final_g,
        cache_k=cache_k[0].reshape(bd, (N_META + past) * H_ATT, D_V),
        cache_v=cache_v[0].reshape(bd, (N_META + past) * H_ATT, D_V),
        bias_cache=_build_bias(d_qpos, cache_pos, ones(N_META + past), rel_bias))

    return (y_prompt, y_sample,
            kp.reshape(1, bp, N_META + sp, H_ATT, D_V),
            vp.reshape(1, bp, N_META + sp, H_ATT, D_V),
            cp[None],
            kd.reshape(1, bd, sd, H_ATT, D_V),
            vd.reshape(1, bd, sd, H_ATT, D_V),
            cd[None])
```

```python
import functools
import math

import numpy as np
import jax
import jax.numpy as jnp
from jax import lax
from jax.experimental import pallas as pl
from jax.experimental.pallas import tpu as pltpu

f32 = jnp.float32
bf16 = jnp.bfloat16

CHUNK = 64
N_META = 16
H_ATT = 4
D_QK = 64
D_V = 2 * D_QK
W_ATT = H_ATT * D_V
CONV_W = 3
N_SPLIT = 8
N_BUCKETS = 32
MAX_DIST = 128
EPS = 1e-6
NEG = -1e30
SCALE = D_QK ** -0.5
LOG2E = math.log2(math.e)
LAM_INIT = 0.8 - 0.6 * math.exp(-0.3 * 0)
FAR_BUCKET = N_BUCKETS // 2 - 1

LANES = 128
ATT_TILE = 256
ATT_SUB = 4
VMEM_LIMIT = 48 * 1024 * 1024


def _silu(z):
    return z * (1.0 / (1.0 + jnp.exp(-z)))


def _rms(x, g):
    return x * lax.rsqrt(jnp.mean(x * x, axis=-1, keepdims=True) + EPS) * g


def _in_proj_kernel(x_ref, cinit_ref, g_ref, w_ref, cw_ref,
                    q_ref, k_ref, kb_ref, v_ref, vb_ref, gate_ref, yc_ref, clast_ref,
                    carry_ref, *, tm, wg, transpose_v):
    j = pl.program_id(1)

    @pl.when(j == 0)
    def _():
        carry_ref[...] = cinit_ref[0]

    x = x_ref[0]
    h = _rms(x, g_ref[...]).astype(bf16)

    def proj(g):
        return jnp.dot(h, w_ref[:, g * wg:(g + 1) * wg], preferred_element_type=f32)

    def store_heads(ref, a):
        for hd in range(H_ATT):
            ref[pl.ds(hd, tm, stride=H_ATT), :] = a[:, hd * D_V:(hd + 1) * D_V]

    q_ref[0] = (proj(0) * (SCALE * LOG2E)).astype(bf16)
    k = proj(1)
    store_heads(k_ref, k)
    kb_ref[0] = k.astype(bf16)
    v = proj(2)
    store_heads(v_ref, v)
    vb_ref[...] = (v.T if transpose_v else v).astype(bf16)
    gate_ref[0] = _silu(proj(3))

    cu = proj(5) * proj(6)
    prev = carry_ref[...]
    row = lax.broadcasted_iota(jnp.int32, cu.shape, 0)
    cu_m1 = jnp.where(row == 0, prev[1:2], pltpu.roll(cu, 1, axis=0))
    cu_m2 = jnp.where(row == 0, prev[0:1],
                      jnp.where(row == 1, prev[1:2], pltpu.roll(cu, 2, axis=0)))
    cw = cw_ref[...]
    conv = cw[0:1] * cu_m2 + cw[1:2] * cu_m1 + cw[2:3] * cu
    yc_ref[0] = (_silu(proj(7)) * proj(4) * conv).astype(bf16)

    last = cu[tm - 2:tm]
    carry_ref[...] = last
    clast_ref[0] = last


def _in_proj(x, conv_init, norm_g, w_bf, conv_w, tm, row_offset=0, transpose_v=False):
    b, l, d = x.shape
    w_all = w_bf.shape[1]
    wg = w_all // N_SPLIT
    assert wg == W_ATT
    nj = l // tm
    row_spec = lambda width: pl.BlockSpec((1, tm, width), lambda i, j: (i, j, 0))
    const = lambda shape: pl.BlockSpec(shape, lambda i, j: (0,) * len(shape))
    state_spec = pl.BlockSpec((1, CONV_W - 1, wg), lambda i, j: (i, 0, 0))
    if row_offset:
        cache_spec = pl.BlockSpec(
            (None, pl.Element(tm * H_ATT), pl.Element(D_V)),
            lambda i, j: (i, pl.multiple_of((row_offset + j * tm) * H_ATT, 8 * H_ATT), 0))
    else:
        cache_spec = pl.BlockSpec((None, tm * H_ATT, D_V), lambda i, j: (i, j, 0))
    if transpose_v:
        vb_spec = pl.BlockSpec((None, None, wg, tm), lambda i, j: (i, j, 0, 0))
        vb_shape = jax.ShapeDtypeStruct((b, nj, wg, tm), bf16)
    else:
        vb_spec = pl.BlockSpec((None, tm, wg), lambda i, j: (i, j, 0))
        vb_shape = jax.ShapeDtypeStruct((b, l, wg), bf16)
    cache_shape = jax.ShapeDtypeStruct((b, (row_offset + l) * H_ATT, D_V), f32)
    out_shape = (
        jax.ShapeDtypeStruct((b, l, wg), bf16),
        cache_shape,
        jax.ShapeDtypeStruct((b, l, wg), bf16),
        cache_shape,
        vb_shape,
        jax.ShapeDtypeStruct((b, l, wg), f32),
        jax.ShapeDtypeStruct((b, l, wg), bf16),
        jax.ShapeDtypeStruct((b, CONV_W - 1, wg), f32),
    )
    return pl.pallas_call(
        functools.partial(_in_proj_kernel, tm=tm, wg=wg, transpose_v=transpose_v),
        name="in_proj",
        grid=(b, nj),
        in_specs=[row_spec(d), state_spec, const((1, d)), const((d, w_all)), const((CONV_W, wg))],
        out_specs=(row_spec(wg), cache_spec, row_spec(wg), cache_spec, vb_spec, row_spec(wg),
                   row_spec(wg), state_spec),
        out_shape=out_shape,
        scratch_shapes=[pltpu.VMEM((CONV_W - 1, wg), f32)],
        compiler_params=pltpu.CompilerParams(
            dimension_semantics=("arbitrary", "arbitrary"), vmem_limit_bytes=VMEM_LIMIT),
    )(x, conv_init, norm_g.reshape(1, d), w_bf, conv_w)


def _fill_rows_kernel(km_ref, vm_ref, k_hbm, v_hbm, ko_ref, vo_ref):
    del k_hbm, v_hbm
    ko_ref[0] = km_ref[...]
    vo_ref[0] = vm_ref[...]


def _fill_meta_rows(k_meta, v_meta, k_all, v_all):
    b = k_all.shape[0]
    rows = k_meta.shape[0]
    small = pl.BlockSpec((rows, D_V), lambda i: (0, 0))
    lead = pl.BlockSpec((1, rows, D_V), lambda i: (i, 0, 0))
    whole = pl.BlockSpec(memory_space=pl.ANY)
    return pl.pallas_call(
        _fill_rows_kernel,
        name="fill_meta_rows",
        grid=(b,),
        in_specs=[small, small, whole, whole],
        out_specs=(lead, lead),
        out_shape=(jax.ShapeDtypeStruct(k_all.shape, k_all.dtype),
                   jax.ShapeDtypeStruct(v_all.shape, v_all.dtype)),
        input_output_aliases={2: 0, 3: 1},
    )(k_meta, v_meta, k_all, v_all)


def _rel_bucket(rel):
    nb = N_BUCKETS // 2
    ret = np.where(rel > 0, nb, 0)
    n = np.abs(rel)
    max_exact = nb // 2
    nf = np.maximum(n, 1).astype(np.float32)
    large = max_exact + (np.log(nf / np.float32(max_exact)) / np.float32(math.log(MAX_DIST / max_exact))
                         * np.float32(nb - max_exact)).astype(np.int32)
    large = np.minimum(large, nb - 1)
    return (ret + np.where(n < max_exact, n, large)).astype(np.int32)


def _chunk_id_np(pos):
    return np.where(pos < 0, -1, pos // CHUNK)


def _assert_far_past(q_pos, k_pos):
    rel = k_pos[None, :] - q_pos[:, None]
    nb = N_BUCKETS // 2
    max_exact = nb // 2
    n = np.abs(rel).astype(np.float64)
    large = max_exact + np.log(n / max_exact) / math.log(MAX_DIST / max_exact) * (nb - max_exact)
    assert np.all(rel < 0) and np.all(large >= nb), "tile is not in the saturated bucket"
    assert np.all(_chunk_id_np(k_pos)[None, :] <= _chunk_id_np(q_pos)[:, None])


def _bias_kernel(bucket_ref, vis_ref, rb_ref, o_ref):
    h = pl.program_id(0)
    bucket = bucket_ref[...]
    far = rb_ref[FAR_BUCKET, h]
    acc = jnp.zeros(bucket.shape, f32)
    for b in range(N_BUCKETS):
        acc = jnp.where(bucket == b, rb_ref[b, h] - far, acc)
    o_ref[0] = jnp.where(vis_ref[...] != 0, acc * LOG2E, NEG)


def _build_bias(q_pos, k_pos, k_valid, rel_bias, keys_major=False):
    rel = k_pos[None, :] - q_pos[:, None]
    vis = (_chunk_id_np(k_pos)[None, :] <= _chunk_id_np(q_pos)[:, None]) & k_valid[None, :]
    if keys_major:
        rel, vis = rel.T, vis.T
    bucket = jnp.asarray(_rel_bucket(rel))
    r, c = rel.shape
    full = pl.BlockSpec((r, c), lambda h: (0, 0))
    return pl.pallas_call(
        _bias_kernel,
        name="bias_tile",
        grid=(H_ATT,),
        in_specs=[full, full, pl.BlockSpec(memory_space=pltpu.SMEM)],
        out_specs=pl.BlockSpec((1, r, c), lambda h: (h, 0, 0)),
        out_shape=jax.ShapeDtypeStruct((H_ATT, r, c), f32),
    )(bucket, jnp.asarray(vis.astype(np.int32)), rel_bias)


def _diff_lambda(lam_ref):
    lv = lam_ref[...]
    s1 = jnp.sum(lv[0:1] * lv[1:2], axis=-1, keepdims=True)
    s2 = jnp.sum(lv[2:3] * lv[3:4], axis=-1, keepdims=True)
    return jnp.exp(s1) - jnp.exp(s2) + LAM_INIT


def _gate_heads(o1, o2, lam, sg, gate):
    o = o1 - lam * o2
    return _rms(o, sg) * (1.0 - LAM_INIT) * gate


def _prompt_attn_kernel(q_ref, k_ref, vt_ref, km_ref, vmt_ref, bd_ref, b10_ref, bs_ref, bm_ref,
                        gate_ref, sg_ref, lam_ref, o_ref, sa_sc, mx_sc, m_sc, l_sc, acc_sc,
                        *, t, n_sub, n_q):
    i = pl.program_id(2)
    t2 = 2 * t
    n_wide = i * (n_sub // 2)
    slots = [(c, p) for c in range(2) for p in range(n_sub)]

    def masked(q):
        lane = lax.broadcasted_iota(jnp.int32, q.shape, 1)
        zero = jnp.zeros_like(q)
        return jnp.where(lane < D_QK, q, zero), jnp.where(lane >= D_QK, q, zero)

    def q_tile(j):
        return masked(q_ref[0, pl.ds(pl.multiple_of(j * (n_sub * t), n_sub * t), n_sub * t), :])

    def keys(start, rows):
        return k_ref[0, pl.ds(pl.multiple_of(start, t), rows), :]

    def cols(p):
        return slice(p * t, (p + 1) * t)

    def qk(kt, q_pair, c, p):
        return lax.dot_general(kt, q_pair[c][cols(p)], (((1,), (1,)), ((), ())),
                               preferred_element_type=f32)

    def col_max(*parts):
        m = jnp.max(parts[0], axis=0, keepdims=True)
        for part in parts[1:]:
            m = jnp.maximum(m, jnp.max(part, axis=0, keepdims=True))
        return m

    def update(c, p, s_parts, vt_parts):
        m_prev = m_sc[c, :, cols(p)]
        m_new = jnp.maximum(m_prev, mx_sc[c, :, cols(p)])
        alpha = jnp.exp2(m_prev - m_new)
        e_sum = pv = None
        for s, vt in zip(s_parts, vt_parts):
            e = jnp.exp2(s - m_new)
            e_k = jnp.sum(e, axis=0, keepdims=True)
            pv_k = jnp.dot(vt, e.astype(bf16), preferred_element_type=f32)
            e_sum = e_k if e_sum is None else e_sum + e_k
            pv = pv_k if pv is None else pv + pv_k
        l_sc[c, :, cols(p)] = alpha * l_sc[c, :, cols(p)] + e_sum
        acc_sc[c, :, cols(p)] = alpha * acc_sc[c, :, cols(p)] + pv
        m_sc[c, :, cols(p)] = m_new

    def far_refill(kt, q_pair, c, p, bias=None):
        s = qk(kt, q_pair, c, p)
        if bias is not None:
            s = s + bias
        sa_sc[c, 0:t2, cols(p)] = s
        mx_sc[c, :, cols(p)] = col_max(s)

    def far_update(c, p, vt):
        update(c, p, [sa_sc[c, 0:t2, cols(p)]], [vt])

    def far_tile(w, refill):
        vt = vt_ref[0, w]
        for c, p in slots:
            far_update(c, p, vt)
            refill(c, p)

    def own_a_rows(p):
        return t if p == 0 else t2

    def own_a_refill(q_pair, c, p, meta_bias=None):
        rows = own_a_rows(p)
        s = qk(keys(n_wide * t2, rows), q_pair, c, p)
        if p < 3:
            s = s + (bd_ref[0], b10_ref[0], bs_ref[0])[p]
        sm = qk(km_ref[...], q_pair, c, p)
        if meta_bias is not None:
            sm = sm + meta_bias
        sa_sc[c, 0:rows, cols(p)] = s
        sa_sc[c, t2:t2 + N_META, cols(p)] = sm
        mx_sc[c, :, cols(p)] = col_max(s, sm)

    def own_a_update(c, p):
        rows = own_a_rows(p)
        update(c, p, [sa_sc[c, 0:rows, cols(p)], sa_sc[c, t2:t2 + N_META, cols(p)]],
               [vt_ref[0, n_wide, :, 0:rows], vmt_ref[...]])

    def own_b_rows(p):
        return t if p == 2 else t2

    def own_b_refill(q_pair, c, p):
        rows = own_b_rows(p)
        s = qk(keys((n_wide + 1) * t2, rows), q_pair, c, p) + (bd_ref[0], b10_ref[0])[p - 2]
        sa_sc[c, 0:rows, cols(p)] = s
        mx_sc[c, :, cols(p)] = col_max(s)

    def own_b_update(c, p):
        rows = own_b_rows(p)
        update(c, p, [sa_sc[c, 0:rows, cols(p)]], [vt_ref[0, n_wide + 1, :, 0:rows]])

    qs = q_tile(i)
    m_sc[...] = jnp.full(m_sc.shape, NEG, f32)
    l_sc[...] = jnp.zeros(l_sc.shape, f32)
    acc_sc[...] = jnp.zeros(acc_sc.shape, f32)

    @pl.when(i == 0)
    def _():
        for c, p in slots:
            own_a_refill(qs, c, p, bm_ref[0] if p == 0 else None)

    def far_pair(pp, carry):
        w = 2 * pp
        kt1, kt2 = keys((w + 1) * t2, t2), keys((w + 2) * t2, t2)
        far_tile(w, lambda c, p: far_refill(kt1, qs, c, p))
        far_tile(w + 1, lambda c, p: far_refill(kt2, qs, c, p))
        return carry

    lax.fori_loop(0, jnp.maximum(i - 1, 0), far_pair, 0)

    @pl.when(i >= 1)
    def _():
        kt1 = keys((n_wide - 1) * t2, t2)
        far_tile(n_wide - 2,
                 lambda c, p: far_refill(kt1, qs, c, p, bs_ref[0] if p == 0 else None))
        far_tile(n_wide - 1, lambda c, p: own_a_refill(qs, c, p))

    q_next = q_tile(jnp.minimum(i + 1, n_q - 1))
    kt0 = keys(0, t2)
    for c, p in slots:
        own_a_update(c, p)
        if p < 2:
            far_refill(kt0, q_next, c, p)
        else:
            own_b_refill(qs, c, p)
    for c, p in slots:
        if p >= 2:
            own_b_update(c, p)
            far_refill(kt0, q_next, c, p)

    o = acc_sc[0] / l_sc[0] - _diff_lambda(lam_ref) * (acc_sc[1] / l_sc[1])
    o = o * lax.rsqrt(jnp.mean(o * o, axis=0, keepdims=True) + EPS)
    o_ref[0] = (o.T * (sg_ref[...] * (1.0 - LAM_INIT)) * gate_ref[0]).astype(bf16)


def _prompt_attn(q, kb, vt, km, vmt, bias_d0, bias_d10, bias_sub, bias_meta, gate, subln_g,
                 lam_vecs):
    b, l, _ = q.shape
    t = ATT_TILE
    tq = ATT_SUB * t
    assert ATT_SUB == 4 and l % tq == 0
    assert vt.shape == (b, l // (2 * t), W_ATT, 2 * t)
    tile_spec = pl.BlockSpec((1, tq, D_V), lambda bi, h, i: (bi, i, h))
    seq_spec = pl.BlockSpec((1, l, D_V), lambda bi, h, i: (bi, 0, h))
    vt_spec = pl.BlockSpec((1, l // (2 * t), D_V, 2 * t), lambda bi, h, i: (bi, 0, h, 0))
    per_head = lambda rows: pl.BlockSpec((1, rows, t), lambda bi, h, i: (h, 0, 0))
    return pl.pallas_call(
        functools.partial(_prompt_attn_kernel, t=t, n_sub=ATT_SUB, n_q=l // tq),
        name="prompt_attn",
        grid=(b, H_ATT, l // tq),
        in_specs=[
            seq_spec, seq_spec, vt_spec,
            pl.BlockSpec((N_META, D_V), lambda bi, h, i: (0, h)),
            pl.BlockSpec((D_V, N_META), lambda bi, h, i: (h, 0)),
            per_head(t), per_head(2 * t), per_head(2 * t), per_head(N_META),
            tile_spec,
            pl.BlockSpec((1, D_V), lambda bi, h, i: (0, 0)),
            pl.BlockSpec((4, D_QK), lambda bi, h, i: (0, 0)),
        ],
        out_specs=tile_spec,
        out_shape=jax.ShapeDtypeStruct((b, l, W_ATT), bf16),
        scratch_shapes=[pltpu.VMEM((2, 2 * t + N_META, tq), f32),
                        pltpu.VMEM((2, 1, tq), f32),
                        pltpu.VMEM((2, 1, tq), f32), pltpu.VMEM((2, 1, tq), f32),
                        pltpu.VMEM((2, D_V, tq), f32)],
        compiler_params=pltpu.CompilerParams(
            dimension_semantics=("arbitrary", "arbitrary", "arbitrary"),
            vmem_limit_bytes=VMEM_LIMIT),
    )(q, kb, vt, km, vmt, bias_d0, bias_d10, bias_sub, bias_meta, gate, subln_g, lam_vecs)


def _merge(og, yc, x, wo_ref, fg):
    y = x + jnp.dot(og, wo_ref[0:W_ATT, :], preferred_element_type=f32)
    y = y + jnp.dot(yc, wo_ref[W_ATT:, :], preferred_element_type=f32)
    return _rms(y, fg)


def _out_proj_kernel(og_ref, yc_ref, x_ref, wo_ref, fg_ref, y_ref):
    y_ref[...] = _merge(og_ref[...], yc_ref[...], x_ref[...], wo_ref, fg_ref[...])


def _out_proj(og, yc, x, wo_bf, final_g, tm):
    n, d = x.shape
    w_att = og.shape[1]
    w_conv = yc.shape[1]
    row_spec = lambda width: pl.BlockSpec((tm, width), lambda i: (i, 0))
    return pl.pallas_call(
        _out_proj_kernel,
        name="out_proj",
        grid=(n // tm,),
        in_specs=[row_spec(w_att), row_spec(w_conv), row_spec(d),
                  pl.BlockSpec((w_att + w_conv, d), lambda i: (0, 0)),
                  pl.BlockSpec((1, d), lambda i: (0, 0))],
        out_specs=row_spec(d),
        out_shape=jax.ShapeDtypeStruct((n, d), f32),
        compiler_params=pltpu.CompilerParams(
            dimension_semantics=("arbitrary",), vmem_limit_bytes=VMEM_LIMIT),
    )(og, yc, x, wo_bf, final_g.reshape(1, d))


def _small_attn_kernel(*refs, has_cache):
    if has_cache:
        (q_ref, kn_ref, vn_ref, bn_ref, ck_ref, cv_ref, bc_ref,
         gate_ref, yc_ref, x_ref, sg_ref, lam_ref, wo_ref, fg_ref, y_ref) = refs
    else:
        (q_ref, kn_ref, vn_ref, bn_ref,
         gate_ref, yc_ref, x_ref, sg_ref, lam_ref, wo_ref, fg_ref, y_ref) = refs
    q = q_ref[0]
    lane = lax.broadcasted_iota(jnp.int32, (q.shape[0], D_V), 1)
    lam = _diff_lambda(lam_ref)
    nt = (((1,), (1,)), ((), ()))
    heads = []
    for h in range(H_ATT):
        cols = slice(h * D_V, (h + 1) * D_V)
        qh = q[:, cols]
        zero = jnp.zeros_like(qh)
        kn = kn_ref[0][:, cols]
        vn = vn_ref[0][:, cols]
        if has_cache:
            n_rows = ck_ref.shape[1] // H_ATT
            ck = ck_ref[0, pl.ds(h, n_rows, stride=H_ATT), :].astype(bf16)
            cv = cv_ref[0, pl.ds(h, n_rows, stride=H_ATT), :].astype(bf16)
        outs = []
        for c in range(2):
            qc = jnp.where((lane < D_QK) if c == 0 else (lane >= D_QK), qh, zero)
            sn = lax.dot_general(qc, kn, nt, preferred_element_type=f32) + bn_ref[h]
            m = jnp.max(sn, axis=-1, keepdims=True)
            if has_cache:
                sc = lax.dot_general(qc, ck, nt, preferred_element_type=f32) + bc_ref[h]
                m = jnp.maximum(m, jnp.max(sc, axis=-1, keepdims=True))
            en = jnp.exp2(sn - m)
            l = jnp.sum(en, axis=-1, keepdims=True)
            acc = jnp.dot(en.astype(bf16), vn, preferred_element_type=f32)
            if has_cache:
                ec = jnp.exp2(sc - m)
                l = l + jnp.sum(ec, axis=-1, keepdims=True)
                acc = acc + jnp.dot(ec.astype(bf16), cv, preferred_element_type=f32)
            outs.append(acc / l)
        heads.append(_gate_heads(outs[0], outs[1], lam, sg_ref[...], gate_ref[0][:, cols]))
    og = jnp.concatenate(heads, axis=-1).astype(bf16)
    y_ref[0] = _merge(og, yc_ref[0], x_ref[0], wo_ref, fg_ref[...])


def _small_attn(q, kb, vb, bias_new, gate, yc, x, subln_g, lam_vecs, wo_bf, final_g,
                cache_k=None, cache_v=None, bias_cache=None):
    b, lq, d = x.shape
    has_cache = cache_k is not None
    per_b = lambda rows, width: pl.BlockSpec((1, rows, width), lambda i: (i, 0, 0))
    const = lambda shape: pl.BlockSpec(shape, lambda i: (0,) * len(shape))
    w = q.shape[2]
    args = [q, kb, vb, bias_new]
    specs = [per_b(lq, w), per_b(lq, w), per_b(lq, w), const(bias_new.shape)]
    if has_cache:
        args += [cache_k, cache_v, bias_cache]
        specs += [per_b(*cache_k.shape[1:]), per_b(*cache_v.shape[1:]), const(bias_cache.shape)]
    args += [gate, yc, x, subln_g, lam_vecs, wo_bf, final_g.reshape(1, d)]
    specs += [per_b(lq, w), per_b(lq, yc.shape[2]), per_b(lq, d), const(subln_g.shape),
              const(lam_vecs.shape), const(wo_bf.shape), const((1, d))]
    return pl.pallas_call(
        functools.partial(_small_attn_kernel, has_cache=has_cache),
        name="small_attn",
        grid=(b,),
        in_specs=specs,
        out_specs=per_b(lq, d),
        out_shape=jax.ShapeDtypeStruct((b, lq, d), f32),
        compiler_params=pltpu.CompilerParams(
            dimension_semantics=("arbitrary",), vmem_limit_bytes=VMEM_LIMIT),
    )(*args)


def kernel(x_prompt, x_sample, cache_k, cache_v, state_conv, meta_tokens, rel_bias, norm_g, w_in,
           conv_w, lambda_q1, lambda_k1, lambda_q2, lambda_k2, subln_g, w_out, final_g):
    bp, sp, d = x_prompt.shape
    bd, sd, _ = x_sample.shape
    depth = w_in.shape[0]
    assert depth == 1, "single-layer step only"
    past = cache_k.shape[2] - N_META
    t = ATT_TILE
    assert sp % (ATT_SUB * t) == 0 and t % CHUNK == 0

    w_bf = w_in[0].astype(bf16)
    wo_bf = w_out[0].astype(bf16)
    lam_vecs = jnp.stack([lambda_q1[0], lambda_k1[0], lambda_q2[0], lambda_k2[0]])
    sg = subln_g[0].reshape(1, D_V)
    wg = w_bf.shape[1] // N_SPLIT

    zeros_state = jnp.zeros((1, CONV_W - 1, wg), f32)
    (qm, km, kmb, vm, vmb, gm, ycm, cm) = _in_proj(
        meta_tokens[None], zeros_state, norm_g[0], w_bf, conv_w[0], tm=N_META)
    (qp, kp, kpb, vp, vpt, gp, ycp, cp) = _in_proj(
        x_prompt, jnp.broadcast_to(cm, (bp, CONV_W - 1, wg)), norm_g[0], w_bf, conv_w[0],
        tm=2 * t, row_offset=N_META, transpose_v=True)
    kp, vp = _fill_meta_rows(km[0], vm[0], kp, vp)
    (qd, kd, kdb, vd, vdb, gd, ycd, cd) = _in_proj(
        x_sample, state_conv[0], norm_g[0], w_bf, conv_w[0], tm=sd)

    meta_pos = np.arange(-N_META, 0)
    tile_pos = np.arange(t)
    ones = lambda n: np.ones((n,), bool)
    near = lambda k_pos: _build_bias(tile_pos + 2 * t, k_pos, ones(len(k_pos)), rel_bias,
                                     keys_major=True)
    bias_d0 = near(np.arange(2 * t, 3 * t))
    bias_d10 = near(np.arange(t, 3 * t))
    bias_sub = near(np.arange(0, 2 * t))
    for off in range(2, sp // t):
        _assert_far_past(tile_pos + off * t, tile_pos)
    bias_meta = _build_bias(tile_pos, meta_pos, ones(N_META), rel_bias, keys_major=True)
    for off in range(1, sp // t):
        _assert_far_past(tile_pos + off * t, meta_pos)

    og = _prompt_attn(qp, kpb, vpt, kmb[0], vmb[0].T, bias_d0, bias_d10, bias_sub, bias_meta,
                      gp, sg, lam_vecs)
    y_prompt = _out_proj(og.reshape(bp * sp, W_ATT), ycp.reshape(bp * sp, wg),
                         x_prompt.reshape(bp * sp, d), wo_bf, final_g, tm=512).reshape(bp, sp, d)

    _small_attn(qm, kmb, vmb, _build_bias(meta_pos, meta_pos, ones(N_META), rel_bias),
                gm, ycm, meta_tokens[None], sg, lam_vecs, wo_bf, final_g)

    d_qpos = past + np.arange(sd)
    cache_pos = np.concatenate([meta_pos, np.arange(past)])
    y_sample = _small_attn(
        qd, kdb, vdb, _build_bias(d_qpos, d_qpos, ones(sd), rel_bias),
        gd, ycd, x_sample, sg, lam_vecs, wo_bf, final_g,
        cache_k=cache_k[0].reshape(bd, (N_META + past) * H_ATT, D_V),
        cache_v=cache_v[0].reshape(bd, (N_META + past) * H_ATT, D_V),
        bias_cache=_build_bias(d_qpos, cache_pos, ones(N_META + past), rel_bias))

    return (y_prompt, y_sample,
            kp.reshape(1, bp, N_META + sp, H_ATT, D_V),
            vp.reshape(1, bp, N_META + sp, H_ATT, D_V),
            cp[None],
            kd.reshape(1, bd, sd, H_ATT, D_V),
            vd.reshape(1, bd, sd, H_ATT, D_V),
            cd[None])
```

```python
import functools
import math

import numpy as np
import jax
import jax.numpy as jnp
from jax import lax
from jax.experimental import pallas as pl
from jax.experimental.pallas import tpu as pltpu

f32 = jnp.float32
bf16 = jnp.bfloat16

CHUNK = 64
N_META = 16
H_ATT = 4
D_QK = 64
D_V = 2 * D_QK
W_ATT = H_ATT * D_V
CONV_W = 3
N_SPLIT = 8
N_BUCKETS = 32
MAX_DIST = 128
EPS = 1e-6
NEG = -1e30
SCALE = D_QK ** -0.5
LOG2E = math.log2(math.e)
LAM_INIT = 0.8 - 0.6 * math.exp(-0.3 * 0)
FAR_BUCKET = N_BUCKETS // 2 - 1

LANES = 128
ATT_TILE = 256
ATT_SUB = 4
VMEM_LIMIT = 48 * 1024 * 1024


def _silu(z):
    return z * (1.0 / (1.0 + jnp.exp(-z)))


def _rms(x, g):
    return x * lax.rsqrt(jnp.mean(x * x, axis=-1, keepdims=True) + EPS) * g


def _in_proj_kernel(x_ref, cinit_ref, g_ref, w_ref, cw_ref,
                    qa_ref, qb_ref, k_ref, kb_ref, v_ref, vb_ref, gate_ref, yc_ref, clast_ref,
                    carry_ref, *, tm, wg, transpose_v):
    j = pl.program_id(1)

    @pl.when(j == 0)
    def _():
        carry_ref[...] = cinit_ref[0]

    x = x_ref[0]
    h = _rms(x, g_ref[...]).astype(bf16)

    def proj(g):
        return jnp.dot(h, w_ref[:, g * wg:(g + 1) * wg], preferred_element_type=f32)

    def store_heads(ref, a):
        for hd in range(H_ATT):
            ref[pl.ds(hd, tm, stride=H_ATT), :] = a[:, hd * D_V:(hd + 1) * D_V]

    q = (proj(0) * (SCALE * LOG2E)).astype(bf16)
    first_half = lax.broadcasted_iota(jnp.int32, q.shape, 1) % D_V < D_QK
    zero = jnp.zeros_like(q)
    qa_ref[0] = jnp.where(first_half, q, zero)
    qb_ref[0] = jnp.where(first_half, zero, q)
    k = proj(1)
    store_heads(k_ref, k)
    kb_ref[0] = k.astype(bf16)
    v = proj(2)
    store_heads(v_ref, v)
    vb_ref[...] = (v.T if transpose_v else v).astype(bf16)
    gate_ref[0] = _silu(proj(3))

    cu = proj(5) * proj(6)
    prev = carry_ref[...]
    row = lax.broadcasted_iota(jnp.int32, cu.shape, 0)
    cu_m1 = jnp.where(row == 0, prev[1:2], pltpu.roll(cu, 1, axis=0))
    cu_m2 = jnp.where(row == 0, prev[0:1],
                      jnp.where(row == 1, prev[1:2], pltpu.roll(cu, 2, axis=0)))
    cw = cw_ref[...]
    conv = cw[0:1] * cu_m2 + cw[1:2] * cu_m1 + cw[2:3] * cu
    yc_ref[0] = (_silu(proj(7)) * proj(4) * conv).astype(bf16)

    last = cu[tm - 2:tm]
    carry_ref[...] = last
    clast_ref[0] = last


def _in_proj(x, conv_init, norm_g, w_bf, conv_w, tm, row_offset=0, transpose_v=False):
    b, l, d = x.shape
    w_all = w_bf.shape[1]
    wg = w_all // N_SPLIT
    assert wg == W_ATT
    nj = l // tm
    row_spec = lambda width: pl.BlockSpec((1, tm, width), lambda i, j: (i, j, 0))
    const = lambda shape: pl.BlockSpec(shape, lambda i, j: (0,) * len(shape))
    state_spec = pl.BlockSpec((1, CONV_W - 1, wg), lambda i, j: (i, 0, 0))
    if row_offset:
        cache_spec = pl.BlockSpec(
            (None, pl.Element(tm * H_ATT), pl.Element(D_V)),
            lambda i, j: (i, pl.multiple_of((row_offset + j * tm) * H_ATT, 8 * H_ATT), 0))
    else:
        cache_spec = pl.BlockSpec((None, tm * H_ATT, D_V), lambda i, j: (i, j, 0))
    if transpose_v:
        vb_spec = pl.BlockSpec((None, None, wg, tm), lambda i, j: (i, j, 0, 0))
        vb_shape = jax.ShapeDtypeStruct((b, nj, wg, tm), bf16)
    else:
        vb_spec = pl.BlockSpec((None, tm, wg), lambda i, j: (i, j, 0))
        vb_shape = jax.ShapeDtypeStruct((b, l, wg), bf16)
    cache_shape = jax.ShapeDtypeStruct((b, (row_offset + l) * H_ATT, D_V), f32)
    out_shape = (
        jax.ShapeDtypeStruct((b, l, wg), bf16),
        jax.ShapeDtypeStruct((b, l, wg), bf16),
        cache_shape,
        jax.ShapeDtypeStruct((b, l, wg), bf16),
        cache_shape,
        vb_shape,
        jax.ShapeDtypeStruct((b, l, wg), f32),
        jax.ShapeDtypeStruct((b, l, wg), bf16),
        jax.ShapeDtypeStruct((b, CONV_W - 1, wg), f32),
    )
    return pl.pallas_call(
        functools.partial(_in_proj_kernel, tm=tm, wg=wg, transpose_v=transpose_v),
        name="in_proj",
        grid=(b, nj),
        in_specs=[row_spec(d), state_spec, const((1, d)), const((d, w_all)), const((CONV_W, wg))],
        out_specs=(row_spec(wg), row_spec(wg), cache_spec, row_spec(wg), cache_spec, vb_spec,
                   row_spec(wg), row_spec(wg), state_spec),
        out_shape=out_shape,
        scratch_shapes=[pltpu.VMEM((CONV_W - 1, wg), f32)],
        compiler_params=pltpu.CompilerParams(
            dimension_semantics=("arbitrary", "arbitrary"), vmem_limit_bytes=VMEM_LIMIT),
    )(x, conv_init, norm_g.reshape(1, d), w_bf, conv_w)


def _fill_rows_kernel(km_ref, vm_ref, k_hbm, v_hbm, ko_ref, vo_ref):
    del k_hbm, v_hbm
    ko_ref[0] = km_ref[...]
    vo_ref[0] = vm_ref[...]


def _fill_meta_rows(k_meta, v_meta, k_all, v_all):
    b = k_all.shape[0]
    rows = k_meta.shape[0]
    small = pl.BlockSpec((rows, D_V), lambda i: (0, 0))
    lead = pl.BlockSpec((1, rows, D_V), lambda i: (i, 0, 0))
    whole = pl.BlockSpec(memory_space=pl.ANY)
    return pl.pallas_call(
        _fill_rows_kernel,
        name="fill_meta_rows",
        grid=(b,),
        in_specs=[small, small, whole, whole],
        out_specs=(lead, lead),
        out_shape=(jax.ShapeDtypeStruct(k_all.shape, k_all.dtype),
                   jax.ShapeDtypeStruct(v_all.shape, v_all.dtype)),
        input_output_aliases={2: 0, 3: 1},
    )(k_meta, v_meta, k_all, v_all)


def _rel_bucket(rel):
    nb = N_BUCKETS // 2
    ret = np.where(rel > 0, nb, 0)
    n = np.abs(rel)
    max_exact = nb // 2
    nf = np.maximum(n, 1).astype(np.float32)
    large = max_exact + (np.log(nf / np.float32(max_exact)) / np.float32(math.log(MAX_DIST / max_exact))
                         * np.float32(nb - max_exact)).astype(np.int32)
    large = np.minimum(large, nb - 1)
    return (ret + np.where(n < max_exact, n, large)).astype(np.int32)


def _chunk_id_np(pos):
    return np.where(pos < 0, -1, pos // CHUNK)


def _assert_far_past(q_pos, k_pos):
    rel = k_pos[None, :] - q_pos[:, None]
    nb = N_BUCKETS // 2
    max_exact = nb // 2
    n = np.abs(rel).astype(np.float64)
    large = max_exact + np.log(n / max_exact) / math.log(MAX_DIST / max_exact) * (nb - max_exact)
    assert np.all(rel < 0) and np.all(large >= nb), "tile is not in the saturated bucket"
    assert np.all(_chunk_id_np(k_pos)[None, :] <= _chunk_id_np(q_pos)[:, None])


def _bias_kernel(rb_ref, *refs):
    n = len(refs) // 3
    h = pl.program_id(0)
    far = rb_ref[FAR_BUCKET, h]
    shifted = [rb_ref[b, h] - far for b in range(N_BUCKETS)]
    for j in range(n):
        bucket = refs[2 * j][...]
        acc = jnp.zeros(bucket.shape, f32)
        for b in range(N_BUCKETS):
            acc = jnp.where(bucket == b, shifted[b], acc)
        refs[2 * n + j][0] = jnp.where(refs[2 * j + 1][...] != 0, acc * LOG2E, NEG)


def _build_biases(tiles, rel_bias):
    args, specs, out_specs, out_shapes = [], [], [], []
    for q_pos, k_pos, keys_major in tiles:
        rel = k_pos[None, :] - q_pos[:, None]
        vis = _chunk_id_np(k_pos)[None, :] <= _chunk_id_np(q_pos)[:, None]
        if keys_major:
            rel, vis = rel.T, vis.T
        r, c = rel.shape
        args += [jnp.asarray(_rel_bucket(rel)), jnp.asarray(vis.astype(np.int32))]
        specs += [pl.BlockSpec((r, c), lambda h: (0, 0))] * 2
        out_specs.append(pl.BlockSpec((1, r, c), lambda h: (h, 0, 0)))
        out_shapes.append(jax.ShapeDtypeStruct((H_ATT, r, c), f32))
    return pl.pallas_call(
        _bias_kernel,
        name="bias_tiles",
        grid=(H_ATT,),
        in_specs=[pl.BlockSpec(memory_space=pltpu.SMEM)] + specs,
        out_specs=tuple(out_specs),
        out_shape=tuple(out_shapes),
    )(rel_bias, *args)


def _diff_lambda(lam_ref):
    lv = lam_ref[...]
    s1 = jnp.sum(lv[0:1] * lv[1:2], axis=-1, keepdims=True)
    s2 = jnp.sum(lv[2:3] * lv[3:4], axis=-1, keepdims=True)
    return jnp.exp(s1) - jnp.exp(s2) + LAM_INIT


def _gate_heads(o1, o2, lam, sg, gate):
    o = o1 - lam * o2
    return _rms(o, sg) * (1.0 - LAM_INIT) * gate


def _prompt_attn_kernel(qa_ref, qb_ref, k_ref, vt_ref, km_ref, vmt_ref, bd_ref, b10_ref, bs_ref,
                        bm_ref, gate_ref, sg_ref, lam_ref, o_ref, sa_sc, mx_sc, m_sc, l_sc,
                        acc_sc, *, t, n_sub, n_q):
    i = pl.program_id(2)
    t2 = 2 * t
    n_wide = i * (n_sub // 2)
    slots = [(c, p) for c in range(2) for p in range(n_sub)]

    def q_tile(j):
        rows = pl.ds(pl.multiple_of(j * (n_sub * t), n_sub * t), n_sub * t)
        return qa_ref[0, rows, :], qb_ref[0, rows, :]

    def keys(start, rows):
        return k_ref[0, pl.ds(pl.multiple_of(start, t), rows), :]

    def cols(p):
        return slice(p * t, (p + 1) * t)

    def qk(kt, q_pair, c, p):
        return lax.dot_general(kt, q_pair[c][cols(p)], (((1,), (1,)), ((), ())),
                               preferred_element_type=f32)

    def col_max(*parts):
        m = jnp.max(parts[0], axis=0, keepdims=True)
        for part in parts[1:]:
            m = jnp.maximum(m, jnp.max(part, axis=0, keepdims=True))
        return m

    def update(c, p, s_parts, vt_parts):
        m_prev = m_sc[c, :, cols(p)]
        m_new = jnp.maximum(m_prev, mx_sc[c, :, cols(p)])
        alpha = jnp.exp2(m_prev - m_new)
        e_sum = pv = None
        for s, vt in zip(s_parts, vt_parts):
            e = jnp.exp2(s - m_new)
            e_k = jnp.sum(e, axis=0, keepdims=True)
            pv_k = jnp.dot(vt, e.astype(bf16), preferred_element_type=f32)
            e_sum = e_k if e_sum is None else e_sum + e_k
            pv = pv_k if pv is None else pv + pv_k
        l_sc[c, :, cols(p)] = alpha * l_sc[c, :, cols(p)] + e_sum
        acc_sc[c, :, cols(p)] = alpha * acc_sc[c, :, cols(p)] + pv
        m_sc[c, :, cols(p)] = m_new

    def far_refill(kt, q_pair, c, p, bias=None):
        s = qk(kt, q_pair, c, p)
        if bias is not None:
            s = s + bias
        sa_sc[c, 0:t2, cols(p)] = s
        mx_sc[c, :, cols(p)] = col_max(s)

    def far_update(c, p, vt):
        update(c, p, [sa_sc[c, 0:t2, cols(p)]], [vt])

    def far_tile(w, refill):
        vt = vt_ref[0, w]
        for c, p in slots:
            far_update(c, p, vt)
            refill(c, p)

    def own_a_rows(p):
        return t if p == 0 else t2

    def own_a_refill(q_pair, c, p, meta_bias=None):
        rows = own_a_rows(p)
        s = qk(keys(n_wide * t2, rows), q_pair, c, p)
        if p < 3:
            s = s + (bd_ref[0], b10_ref[0], bs_ref[0])[p]
        sm = qk(km_ref[...], q_pair, c, p)
        if meta_bias is not None:
            sm = sm + meta_bias
        sa_sc[c, 0:rows, cols(p)] = s
        sa_sc[c, t2:t2 + N_META, cols(p)] = sm
        mx_sc[c, :, cols(p)] = col_max(s, sm)

    def own_a_update(c, p):
        rows = own_a_rows(p)
        update(c, p, [sa_sc[c, 0:rows, cols(p)], sa_sc[c, t2:t2 + N_META, cols(p)]],
               [vt_ref[0, n_wide, :, 0:rows], vmt_ref[...]])

    def own_b_rows(p):
        return t if p == 2 else t2

    def own_b_refill(q_pair, c, p):
        rows = own_b_rows(p)
        s = qk(keys((n_wide + 1) * t2, rows), q_pair, c, p) + (bd_ref[0], b10_ref[0])[p - 2]
        sa_sc[c, 0:rows, cols(p)] = s
        mx_sc[c, :, cols(p)] = col_max(s)

    def own_b_update(c, p):
        rows = own_b_rows(p)
        update(c, p, [sa_sc[c, 0:rows, cols(p)]], [vt_ref[0, n_wide + 1, :, 0:rows]])

    qs = q_tile(i)
    m_sc[...] = jnp.full(m_sc.shape, NEG, f32)
    l_sc[...] = jnp.zeros(l_sc.shape, f32)
    acc_sc[...] = jnp.zeros(acc_sc.shape, f32)

    @pl.when(i == 0)
    def _():
        for c, p in slots:
            own_a_refill(qs, c, p, bm_ref[0] if p == 0 else None)

    def far_pair(pp, carry):
        w = 2 * pp
        kt1, kt2 = keys((w + 1) * t2, t2), keys((w + 2) * t2, t2)
        far_tile(w, lambda c, p: far_refill(kt1, qs, c, p))
        far_tile(w + 1, lambda c, p: far_refill(kt2, qs, c, p))
        return carry

    lax.fori_loop(0, jnp.maximum(i - 1, 0), far_pair, 0)

    @pl.when(i >= 1)
    def _():
        kt1 = keys((n_wide - 1) * t2, t2)
        far_tile(n_wide - 2,
                 lambda c, p: far_refill(kt1, qs, c, p, bs_ref[0] if p == 0 else None))
        far_tile(n_wide - 1, lambda c, p: own_a_refill(qs, c, p))

    q_next = q_tile(jnp.minimum(i + 1, n_q - 1))
    kt0 = keys(0, t2)
    for c, p in slots:
        own_a_update(c, p)
        if p < 2:
            far_refill(kt0, q_next, c, p)
        else:
            own_b_refill(qs, c, p)
    for c, p in slots:
        if p >= 2:
            own_b_update(c, p)
            far_refill(kt0, q_next, c, p)

    o = acc_sc[0] / l_sc[0] - _diff_lambda(lam_ref) * (acc_sc[1] / l_sc[1])
    o = o * lax.rsqrt(jnp.mean(o * o, axis=0, keepdims=True) + EPS)
    o_ref[0] = (o.T * (sg_ref[...] * (1.0 - LAM_INIT)) * gate_ref[0]).astype(bf16)


def _prompt_attn(qa, qb, kb, vt, km, vmt, bias_d0, bias_d10, bias_sub, bias_meta, gate, subln_g,
                 lam_vecs):
    b, l, _ = qa.shape
    t = ATT_TILE
    tq = ATT_SUB * t
    assert ATT_SUB == 4 and l % tq == 0
    assert vt.shape == (b, l // (2 * t), W_ATT, 2 * t)
    tile_spec = pl.BlockSpec((1, tq, D_V), lambda bi, h, i: (bi, i, h))
    seq_spec = pl.BlockSpec((1, l, D_V), lambda bi, h, i: (bi, 0, h))
    vt_spec = pl.BlockSpec((1, l // (2 * t), D_V, 2 * t), lambda bi, h, i: (bi, 0, h, 0))
    per_head = lambda rows: pl.BlockSpec((1, rows, t), lambda bi, h, i: (h, 0, 0))
    return pl.pallas_call(
        functools.partial(_prompt_attn_kernel, t=t, n_sub=ATT_SUB, n_q=l // tq),
        name="prompt_attn",
        grid=(b, H_ATT, l // tq),
        in_specs=[
            seq_spec, seq_spec, seq_spec, vt_spec,
            pl.BlockSpec((N_META, D_V), lambda bi, h, i: (0, h)),
            pl.BlockSpec((D_V, N_META), lambda bi, h, i: (h, 0)),
            per_head(t), per_head(2 * t), per_head(2 * t), per_head(N_META),
            tile_spec,
            pl.BlockSpec((1, D_V), lambda bi, h, i: (0, 0)),
            pl.BlockSpec((4, D_QK), lambda bi, h, i: (0, 0)),
        ],
        out_specs=tile_spec,
        out_shape=jax.ShapeDtypeStruct((b, l, W_ATT), bf16),
        scratch_shapes=[pltpu.VMEM((2, 2 * t + N_META, tq), f32),
                        pltpu.VMEM((2, 1, tq), f32),
                        pltpu.VMEM((2, 1, tq), f32), pltpu.VMEM((2, 1, tq), f32),
                        pltpu.VMEM((2, D_V, tq), f32)],
        compiler_params=pltpu.CompilerParams(
            dimension_semantics=("arbitrary", "arbitrary", "arbitrary"),
            vmem_limit_bytes=VMEM_LIMIT),
    )(qa, qb, kb, vt, km, vmt, bias_d0, bias_d10, bias_sub, bias_meta, gate, subln_g, lam_vecs)


def _merge(og, yc, x, wo_ref, fg):
    y = x + jnp.dot(og, wo_ref[0:W_ATT, :], preferred_element_type=f32)
    y = y + jnp.dot(yc, wo_ref[W_ATT:, :], preferred_element_type=f32)
    return _rms(y, fg)


def _out_proj_kernel(og_ref, yc_ref, x_ref, wo_ref, fg_ref, y_ref):
    y_ref[...] = _merge(og_ref[...], yc_ref[...], x_ref[...], wo_ref, fg_ref[...])


def _out_proj(og, yc, x, wo_bf, final_g, tm):
    n, d = x.shape
    w_att = og.shape[1]
    w_conv = yc.shape[1]
    row_spec = lambda width: pl.BlockSpec((tm, width), lambda i: (i, 0))
    return pl.pallas_call(
        _out_proj_kernel,
        name="out_proj",
        grid=(n // tm,),
        in_specs=[row_spec(w_att), row_spec(w_conv), row_spec(d),
                  pl.BlockSpec((w_att + w_conv, d), lambda i: (0, 0)),
                  pl.BlockSpec((1, d), lambda i: (0, 0))],
        out_specs=row_spec(d),
        out_shape=jax.ShapeDtypeStruct((n, d), f32),
        compiler_params=pltpu.CompilerParams(
            dimension_semantics=("arbitrary",), vmem_limit_bytes=VMEM_LIMIT),
    )(og, yc, x, wo_bf, final_g.reshape(1, d))


def _small_attn_kernel(*refs, has_cache):
    if has_cache:
        (qa_ref, qb_ref, kn_ref, vn_ref, bn_ref, ck_ref, cv_ref, bc_ref,
         gate_ref, yc_ref, x_ref, sg_ref, lam_ref, wo_ref, fg_ref, y_ref) = refs
    else:
        (qa_ref, qb_ref, kn_ref, vn_ref, bn_ref,
         gate_ref, yc_ref, x_ref, sg_ref, lam_ref, wo_ref, fg_ref, y_ref) = refs
    qs = (qa_ref[0], qb_ref[0])
    lam = _diff_lambda(lam_ref)
    nt = (((1,), (1,)), ((), ()))
    heads = []
    for h in range(H_ATT):
        cols = slice(h * D_V, (h + 1) * D_V)
        kn = kn_ref[0][:, cols]
        vn = vn_ref[0][:, cols]
        if has_cache:
            n_rows = ck_ref.shape[1] // H_ATT
            ck = ck_ref[0, pl.ds(h, n_rows, stride=H_ATT), :].astype(bf16)
            cv = cv_ref[0, pl.ds(h, n_rows, stride=H_ATT), :].astype(bf16)
        outs = []
        for c in range(2):
            qc = qs[c][:, cols]
            sn = lax.dot_general(qc, kn, nt, preferred_element_type=f32) + bn_ref[h]
            m = jnp.max(sn, axis=-1, keepdims=True)
            if has_cache:
                sc = lax.dot_general(qc, ck, nt, preferred_element_type=f32) + bc_ref[h]
                m = jnp.maximum(m, jnp.max(sc, axis=-1, keepdims=True))
            en = jnp.exp2(sn - m)
            l = jnp.sum(en, axis=-1, keepdims=True)
            acc = jnp.dot(en.astype(bf16), vn, preferred_element_type=f32)
            if has_cache:
                ec = jnp.exp2(sc - m)
                l = l + jnp.sum(ec, axis=-1, keepdims=True)
                acc = acc + jnp.dot(ec.astype(bf16), cv, preferred_element_type=f32)
            outs.append(acc / l)
        heads.append(_gate_heads(outs[0], outs[1], lam, sg_ref[...], gate_ref[0][:, cols]))
    og = jnp.concatenate(heads, axis=-1).astype(bf16)
    y_ref[0] = _merge(og, yc_ref[0], x_ref[0], wo_ref, fg_ref[...])


def _small_attn(qa, qb, kb, vb, bias_new, gate, yc, x, subln_g, lam_vecs, wo_bf, final_g,
                cache_k=None, cache_v=None, bias_cache=None):
    b, lq, d = x.shape
    has_cache = cache_k is not None
    per_b = lambda rows, width: pl.BlockSpec((1, rows, width), lambda i: (i, 0, 0))
    const = lambda shape: pl.BlockSpec(shape, lambda i: (0,) * len(shape))
    w = qa.shape[2]
    args = [qa, qb, kb, vb, bias_new]
    specs = [per_b(lq, w)] * 4 + [const(bias_new.shape)]
    if has_cache:
        args += [cache_k, cache_v, bias_cache]
        specs += [per_b(*cache_k.shape[1:]), per_b(*cache_v.shape[1:]), const(bias_cache.shape)]
    args += [gate, yc, x, subln_g, lam_vecs, wo_bf, final_g.reshape(1, d)]
    specs += [per_b(lq, w), per_b(lq, yc.shape[2]), per_b(lq, d), const(subln_g.shape),
              const(lam_vecs.shape), const(wo_bf.shape), const((1, d))]
    return pl.pallas_call(
        functools.partial(_small_attn_kernel, has_cache=has_cache),
        name="small_attn",
        grid=(b,),
        in_specs=specs,
        out_specs=per_b(lq, d),
        out_shape=jax.ShapeDtypeStruct((b, lq, d), f32),
        compiler_params=pltpu.CompilerParams(
            dimension_semantics=("arbitrary",), vmem_limit_bytes=VMEM_LIMIT),
    )(*args)


def kernel(x_prompt, x_sample, cache_k, cache_v, state_conv, meta_tokens, rel_bias, norm_g, w_in,
           conv_w, lambda_q1, lambda_k1, lambda_q2, lambda_k2, subln_g, w_out, final_g):
    bp, sp, d = x_prompt.shape
    bd, sd, _ = x_sample.shape
    depth = w_in.shape[0]
    assert depth == 1, "single-layer step only"
    past = cache_k.shape[2] - N_META
    t = ATT_TILE
    assert sp % (ATT_SUB * t) == 0 and t % CHUNK == 0

    w_bf = w_in[0].astype(bf16)
    wo_bf = w_out[0].astype(bf16)
    lam_vecs = jnp.stack([lambda_q1[0], lambda_k1[0], lambda_q2[0], lambda_k2[0]])
    sg = subln_g[0].reshape(1, D_V)
    wg = w_bf.shape[1] // N_SPLIT

    zeros_state = jnp.zeros((1, CONV_W - 1, wg), f32)
    (qma, qmb, km, kmb, vm, vmb, gm, ycm, cm) = _in_proj(
        meta_tokens[None], zeros_state, norm_g[0], w_bf, conv_w[0], tm=N_META)
    (qpa, qpb, kp, kpb, vp, vpt, gp, ycp, cp) = _in_proj(
        x_prompt, jnp.broadcast_to(cm, (bp, CONV_W - 1, wg)), norm_g[0], w_bf, conv_w[0],
        tm=2 * t, row_offset=N_META, transpose_v=True)
    kp, vp = _fill_meta_rows(km[0], vm[0], kp, vp)
    (qda, qdb, kd, kdb, vd, vdb, gd, ycd, cd) = _in_proj(
        x_sample, state_conv[0], norm_g[0], w_bf, conv_w[0], tm=sd)

    meta_pos = np.arange(-N_META, 0)
    tile_pos = np.arange(t)
    d_qpos = past + np.arange(sd)
    cache_pos = np.concatenate([meta_pos, np.arange(past)])
    near = tile_pos + 2 * t
    (bias_d0, bias_d10, bias_sub, bias_meta, bias_meta_self, bias_dec_new, bias_dec_cache) = (
        _build_biases([
            (near, np.arange(2 * t, 3 * t), True),
            (near, np.arange(t, 3 * t), True),
            (near, np.arange(0, 2 * t), True),
            (tile_pos, meta_pos, True),
            (meta_pos, meta_pos, False),
            (d_qpos, d_qpos, False),
            (d_qpos, cache_pos, False),
        ], rel_bias))
    for off in range(2, sp // t):
        _assert_far_past(tile_pos + off * t, tile_pos)
    for off in range(1, sp // t):
        _assert_far_past(tile_pos + off * t, meta_pos)

    og = _prompt_attn(qpa, qpb, kpb, vpt, kmb[0], vmb[0].T, bias_d0, bias_d10, bias_sub,
                      bias_meta, gp, sg, lam_vecs)
    y_prompt = _out_proj(og.reshape(bp * sp, W_ATT), ycp.reshape(bp * sp, wg),
                         x_prompt.reshape(bp * sp, d), wo_bf, final_g, tm=1024).reshape(bp, sp, d)

    _small_attn(qma, qmb, kmb, vmb, bias_meta_self, gm, ycm, meta_tokens[None], sg, lam_vecs,
                wo_bf, final_g)

    y_sample = _small_attn(
        qda, qdb, kdb, vdb, bias_dec_new, gd, ycd, x_sample, sg, lam_vecs, wo_bf, final_g,
        cache_k=cache_k[0].reshape(bd, (N_META + past) * H_ATT, D_V),
        cache_v=cache_v[0].reshape(bd, (N_META + past) * H_ATT, D_V),
        bias_cache=bias_dec_cache)

    return (y_prompt, y_sample,
            kp.reshape(1, bp, N_META + sp, H_ATT, D_V),
            vp.reshape(1, bp, N_META + sp, H_ATT, D_V),
            cp[None],
            kd.reshape(1, bd, sd, H_ATT, D_V),
            vd.reshape(1, bd, sd, H_ATT, D_V),
            cd[None])
```

```python
import functools
import math

import numpy as np
import jax
import jax.numpy as jnp
from jax import lax
from jax.experimental import pallas as pl
from jax.experimental.pallas import tpu as pltpu

f32 = jnp.float32
bf16 = jnp.bfloat16

CHUNK = 64
N_META = 16
H_ATT = 4
D_QK = 64
D_V = 2 * D_QK
W_ATT = H_ATT * D_V
CONV_W = 3
N_SPLIT = 8
N_BUCKETS = 32
MAX_DIST = 128
EPS = 1e-6
NEG = -1e30
SCALE = D_QK ** -0.5
LOG2E = math.log2(math.e)
LAM_INIT = 0.8 - 0.6 * math.exp(-0.3 * 0)
FAR_BUCKET = N_BUCKETS // 2 - 1

LANES = 128
ATT_TILE = 256
ATT_SUB = 4
VMEM_LIMIT = 48 * 1024 * 1024
PROJ_ROWS = 1024


def _silu(z):
    return z * (1.0 / (1.0 + jnp.exp(-z)))


def _rms(x, g):
    return x * lax.rsqrt(jnp.mean(x * x, axis=-1, keepdims=True) + EPS) * g


def _in_proj_kernel(x_ref, cinit_ref, g_ref, w_ref, cw_ref,
                    qa_ref, qb_ref, k_ref, kb_ref, v_ref, vb_ref, gate_ref, yc_ref, clast_ref,
                    carry_ref, *, tm, wg, transpose_v):
    j = pl.program_id(1)

    @pl.when(j == 0)
    def _():
        carry_ref[...] = cinit_ref[0]

    x = x_ref[0]
    h = _rms(x, g_ref[...]).astype(bf16)

    def proj(g):
        return jnp.dot(h, w_ref[:, g * wg:(g + 1) * wg], preferred_element_type=f32)

    def store_heads(ref, a):
        for hd in range(H_ATT):
            ref[pl.ds(hd, tm, stride=H_ATT), :] = a[:, hd * D_V:(hd + 1) * D_V]

    q = (proj(0) * (SCALE * LOG2E)).astype(bf16)
    first_half = lax.broadcasted_iota(jnp.int32, q.shape, 1) % D_V < D_QK
    zero = jnp.zeros_like(q)
    qa_ref[0] = jnp.where(first_half, q, zero)
    qb_ref[0] = jnp.where(first_half, zero, q)
    k = proj(1)
    store_heads(k_ref, k)
    kb_ref[0] = k.astype(bf16)
    v = proj(2)
    store_heads(v_ref, v)
    if transpose_v:
        for ch in range(tm // transpose_v):
            vb_ref[ch] = v[ch * transpose_v:(ch + 1) * transpose_v].T.astype(bf16)
    else:
        vb_ref[...] = v.astype(bf16)
    gate_ref[0] = _silu(proj(3))

    cu = proj(5) * proj(6)
    prev = carry_ref[...]
    row = lax.broadcasted_iota(jnp.int32, cu.shape, 0)
    cu_m1 = jnp.where(row == 0, prev[1:2], pltpu.roll(cu, 1, axis=0))
    cu_m2 = jnp.where(row == 0, prev[0:1],
                      jnp.where(row == 1, prev[1:2], pltpu.roll(cu, 2, axis=0)))
    cw = cw_ref[...]
    conv = cw[0:1] * cu_m2 + cw[1:2] * cu_m1 + cw[2:3] * cu
    yc_ref[0] = (_silu(proj(7)) * proj(4) * conv).astype(bf16)

    last = cu[tm - 2:tm]
    carry_ref[...] = last
    clast_ref[0] = last


def _in_proj(x, conv_init, norm_g, w_bf, conv_w, tm, row_offset=0, transpose_v=0):
    b, l, d = x.shape
    w_all = w_bf.shape[1]
    wg = w_all // N_SPLIT
    assert wg == W_ATT
    nj = l // tm
    row_spec = lambda width: pl.BlockSpec((1, tm, width), lambda i, j: (i, j, 0))
    const = lambda shape: pl.BlockSpec(shape, lambda i, j: (0,) * len(shape))
    state_spec = pl.BlockSpec((1, CONV_W - 1, wg), lambda i, j: (i, 0, 0))
    if row_offset:
        cache_spec = pl.BlockSpec(
            (None, pl.Element(tm * H_ATT), pl.Element(D_V)),
            lambda i, j: (i, pl.multiple_of((row_offset + j * tm) * H_ATT, 8 * H_ATT), 0))
    else:
        cache_spec = pl.BlockSpec((None, tm * H_ATT, D_V), lambda i, j: (i, j, 0))
    if transpose_v:
        assert tm % transpose_v == 0
        vb_spec = pl.BlockSpec((None, tm // transpose_v, wg, transpose_v),
                               lambda i, j: (i, j, 0, 0))
        vb_shape = jax.ShapeDtypeStruct((b, l // transpose_v, wg, transpose_v), bf16)
    else:
        vb_spec = pl.BlockSpec((None, tm, wg), lambda i, j: (i, j, 0))
        vb_shape = jax.ShapeDtypeStruct((b, l, wg), bf16)
    cache_shape = jax.ShapeDtypeStruct((b, (row_offset + l) * H_ATT, D_V), f32)
    out_shape = (
        jax.ShapeDtypeStruct((b, l, wg), bf16),
        jax.ShapeDtypeStruct((b, l, wg), bf16),
        cache_shape,
        jax.ShapeDtypeStruct((b, l, wg), bf16),
        cache_shape,
        vb_shape,
        jax.ShapeDtypeStruct((b, l, wg), f32),
        jax.ShapeDtypeStruct((b, l, wg), bf16),
        jax.ShapeDtypeStruct((b, CONV_W - 1, wg), f32),
    )
    return pl.pallas_call(
        functools.partial(_in_proj_kernel, tm=tm, wg=wg, transpose_v=transpose_v),
        name="in_proj",
        grid=(b, nj),
        in_specs=[row_spec(d), state_spec, const((1, d)),
                  pl.BlockSpec((d, w_all), lambda i, j: (0, 0), pipeline_mode=pl.Buffered(1)),
                  const((CONV_W, wg))],
        out_specs=(row_spec(wg), row_spec(wg), cache_spec, row_spec(wg), cache_spec, vb_spec,
                   row_spec(wg), row_spec(wg), state_spec),
        out_shape=out_shape,
        scratch_shapes=[pltpu.VMEM((CONV_W - 1, wg), f32)],
        compiler_params=pltpu.CompilerParams(
            dimension_semantics=("arbitrary", "arbitrary"), vmem_limit_bytes=VMEM_LIMIT),
    )(x, conv_init, norm_g.reshape(1, d), w_bf, conv_w)


def _fill_rows_kernel(km_ref, vm_ref, k_hbm, v_hbm, ko_ref, vo_ref):
    del k_hbm, v_hbm
    ko_ref[0] = km_ref[...]
    vo_ref[0] = vm_ref[...]


def _fill_meta_rows(k_meta, v_meta, k_all, v_all):
    b = k_all.shape[0]
    rows = k_meta.shape[0]
    small = pl.BlockSpec((rows, D_V), lambda i: (0, 0))
    lead = pl.BlockSpec((1, rows, D_V), lambda i: (i, 0, 0))
    whole = pl.BlockSpec(memory_space=pl.ANY)
    return pl.pallas_call(
        _fill_rows_kernel,
        name="fill_meta_rows",
        grid=(b,),
        in_specs=[small, small, whole, whole],
        out_specs=(lead, lead),
        out_shape=(jax.ShapeDtypeStruct(k_all.shape, k_all.dtype),
                   jax.ShapeDtypeStruct(v_all.shape, v_all.dtype)),
        input_output_aliases={2: 0, 3: 1},
    )(k_meta, v_meta, k_all, v_all)


def _rel_bucket(rel):
    nb = N_BUCKETS // 2
    ret = np.where(rel > 0, nb, 0)
    n = np.abs(rel)
    max_exact = nb // 2
    nf = np.maximum(n, 1).astype(np.float32)
    large = max_exact + (np.log(nf / np.float32(max_exact)) / np.float32(math.log(MAX_DIST / max_exact))
                         * np.float32(nb - max_exact)).astype(np.int32)
    large = np.minimum(large, nb - 1)
    return (ret + np.where(n < max_exact, n, large)).astype(np.int32)


def _chunk_id_np(pos):
    return np.where(pos < 0, -1, pos // CHUNK)


def _assert_far_past(q_pos, k_pos):
    rel = k_pos[None, :] - q_pos[:, None]
    nb = N_BUCKETS // 2
    max_exact = nb // 2
    n = np.abs(rel).astype(np.float64)
    large = max_exact + np.log(n / max_exact) / math.log(MAX_DIST / max_exact) * (nb - max_exact)
    assert np.all(rel < 0) and np.all(large >= nb), "tile is not in the saturated bucket"
    assert np.all(_chunk_id_np(k_pos)[None, :] <= _chunk_id_np(q_pos)[:, None])


def _bias_kernel(rb_ref, *refs):
    n = len(refs) // 3
    h = pl.program_id(0)
    far = rb_ref[FAR_BUCKET, h]
    shifted = [rb_ref[b, h] - far for b in range(N_BUCKETS)]
    for j in range(n):
        bucket = refs[2 * j][...]
        acc = jnp.zeros(bucket.shape, f32)
        for b in range(N_BUCKETS):
            acc = jnp.where(bucket == b, shifted[b], acc)
        refs[2 * n + j][0] = jnp.where(refs[2 * j + 1][...] != 0, acc * LOG2E, NEG)


def _build_biases(tiles, rel_bias):
    args, specs, out_specs, out_shapes = [], [], [], []
    for q_pos, k_pos, keys_major in tiles:
        rel = k_pos[None, :] - q_pos[:, None]
        vis = _chunk_id_np(k_pos)[None, :] <= _chunk_id_np(q_pos)[:, None]
        if keys_major:
            rel, vis = rel.T, vis.T
        r, c = rel.shape
        args += [jnp.asarray(_rel_bucket(rel)), jnp.asarray(vis.astype(np.int32))]
        specs += [pl.BlockSpec((r, c), lambda h: (0, 0))] * 2
        out_specs.append(pl.BlockSpec((1, r, c), lambda h: (h, 0, 0)))
        out_shapes.append(jax.ShapeDtypeStruct((H_ATT, r, c), f32))
    return pl.pallas_call(
        _bias_kernel,
        name="bias_tiles",
        grid=(H_ATT,),
        in_specs=[pl.BlockSpec(memory_space=pltpu.SMEM)] + specs,
        out_specs=tuple(out_specs),
        out_shape=tuple(out_shapes),
    )(rel_bias, *args)


def _diff_lambda(lam_ref):
    lv = lam_ref[...]
    s1 = jnp.sum(lv[0:1] * lv[1:2], axis=-1, keepdims=True)
    s2 = jnp.sum(lv[2:3] * lv[3:4], axis=-1, keepdims=True)
    return jnp.exp(s1) - jnp.exp(s2) + LAM_INIT


def _gate_heads(o1, o2, lam, sg, gate):
    o = o1 - lam * o2
    return _rms(o, sg) * (1.0 - LAM_INIT) * gate


def _prompt_attn_kernel(qa_ref, qb_ref, k_ref, vt_ref, km_ref, vmt_ref, bd_ref, b10_ref, bs_ref,
                        bm_ref, gate_ref, sg_ref, lam_ref, o_ref, sa_sc, mx_sc, m_sc, l_sc,
                        acc_sc, *, t, n_sub, n_q):
    i = pl.program_id(2)
    t2 = 2 * t
    n_wide = i * (n_sub // 2)
    slots = [(c, p) for c in range(2) for p in range(n_sub)]

    def keys(start, rows):
        return k_ref[0, pl.ds(pl.multiple_of(start, t), rows), :]

    def cols(p):
        return slice(p * t, (p + 1) * t)

    def qk(kt, j, c, p):
        rows = pl.ds(pl.multiple_of((j * n_sub + p) * t, t), t)
        return lax.dot_general(kt, (qa_ref, qb_ref)[c][0, rows, :], (((1,), (1,)), ((), ())),
                               preferred_element_type=f32)

    def col_max(*parts):
        m = jnp.max(parts[0], axis=0, keepdims=True)
        for part in parts[1:]:
            m = jnp.maximum(m, jnp.max(part, axis=0, keepdims=True))
        return m

    def update(c, p, s_parts, vt_parts):
        m_prev = m_sc[c, :, cols(p)]
        m_new = jnp.maximum(m_prev, mx_sc[c, :, cols(p)])
        alpha = jnp.exp2(m_prev - m_new)
        e_sum = pv = None
        for s, vt in zip(s_parts, vt_parts):
            e = jnp.exp2(s - m_new)
            e_k = jnp.sum(e, axis=0, keepdims=True)
            pv_k = jnp.dot(vt, e.astype(bf16), preferred_element_type=f32)
            e_sum = e_k if e_sum is None else e_sum + e_k
            pv = pv_k if pv is None else pv + pv_k
        l_sc[c, :, cols(p)] = alpha * l_sc[c, :, cols(p)] + e_sum
        acc_sc[c, :, cols(p)] = alpha * acc_sc[c, :, cols(p)] + pv
        m_sc[c, :, cols(p)] = m_new

    def far_refill(kt, j, c, p, bias=None):
        s = qk(kt, j, c, p)
        if bias is not None:
            s = s + bias
        sa_sc[c, 0:t2, cols(p)] = s
        mx_sc[c, :, cols(p)] = col_max(s)

    def far_update(c, p, vt):
        update(c, p, [sa_sc[c, 0:t2, cols(p)]], [vt])

    def far_tile(w, refill):
        vt = vt_ref[0, w]
        for c, p in slots:
            far_update(c, p, vt)
            refill(c, p)

    def own_a_rows(p):
        return t if p == 0 else t2

    def own_a_refill(j, c, p, meta_bias=None):
        rows = own_a_rows(p)
        s = qk(keys(n_wide * t2, rows), j, c, p)
        if p < 3:
            s = s + (bd_ref[0], b10_ref[0], bs_ref[0])[p]
        sm = qk(km_ref[...], j, c, p)
        if meta_bias is not None:
            sm = sm + meta_bias
        sa_sc[c, 0:rows, cols(p)] = s
        sa_sc[c, t2:t2 + N_META, cols(p)] = sm
        mx_sc[c, :, cols(p)] = col_max(s, sm)

    def own_a_update(c, p):
        rows = own_a_rows(p)
        update(c, p, [sa_sc[c, 0:rows, cols(p)], sa_sc[c, t2:t2 + N_META, cols(p)]],
               [vt_ref[0, n_wide, :, 0:rows], vmt_ref[...]])

    def own_b_rows(p):
        return t if p == 2 else t2

    def own_b_refill(j, c, p):
        rows = own_b_rows(p)
        s = qk(keys((n_wide + 1) * t2, rows), j, c, p) + (bd_ref[0], b10_ref[0])[p - 2]
        sa_sc[c, 0:rows, cols(p)] = s
        mx_sc[c, :, cols(p)] = col_max(s)

    def own_b_update(c, p):
        rows = own_b_rows(p)
        update(c, p, [sa_sc[c, 0:rows, cols(p)]], [vt_ref[0, n_wide + 1, :, 0:rows]])

    m_sc[...] = jnp.full(m_sc.shape, NEG, f32)
    l_sc[...] = jnp.zeros(l_sc.shape, f32)
    acc_sc[...] = jnp.zeros(acc_sc.shape, f32)

    def far_pair(pp, carry):
        w = 2 * pp
        kt1, kt2 = keys((w + 1) * t2, t2), keys((w + 2) * t2, t2)
        far_tile(w, lambda c, p: far_refill(kt1, i, c, p))
        far_tile(w + 1, lambda c, p: far_refill(kt2, i, c, p))
        return carry

    def own_keys_and_finish():
        j_next = jnp.minimum(i + 1, n_q - 1)
        kt0 = keys(0, t2)
        for c, p in slots:
            own_a_update(c, p)
            if p < 2:
                far_refill(kt0, j_next, c, p)
            else:
                own_b_refill(i, c, p)
        for c, p in slots:
            if p >= 2:
                own_b_update(c, p)
                far_refill(kt0, j_next, c, p)
        o = acc_sc[0] / l_sc[0] - _diff_lambda(lam_ref) * (acc_sc[1] / l_sc[1])
        o = o * lax.rsqrt(jnp.mean(o * o, axis=0, keepdims=True) + EPS)
        o_ref[0] = (o.T * (sg_ref[...] * (1.0 - LAM_INIT)) * gate_ref[0]).astype(bf16)

    @pl.when(i == 0)
    def _():
        for c, p in slots:
            own_a_refill(i, c, p, bm_ref[0] if p == 0 else None)
        own_keys_and_finish()

    @pl.when(i >= 1)
    def _():
        lax.fori_loop(0, i - 1, far_pair, 0)
        kt1 = keys((n_wide - 1) * t2, t2)
        far_tile(n_wide - 2,
                 lambda c, p: far_refill(kt1, i, c, p, bs_ref[0] if p == 0 else None))
        far_tile(n_wide - 1, lambda c, p: own_a_refill(i, c, p))
        own_keys_and_finish()


def _prompt_attn(qa, qb, kb, vt, km, vmt, bias_d0, bias_d10, bias_sub, bias_meta, gate, subln_g,
                 lam_vecs):
    b, l, _ = qa.shape
    t = ATT_TILE
    tq = ATT_SUB * t
    assert ATT_SUB == 4 and l % tq == 0
    assert vt.shape == (b, l // (2 * t), W_ATT, 2 * t)
    tile_spec = pl.BlockSpec((1, tq, D_V), lambda bi, h, i: (bi, i, h))
    seq_spec = pl.BlockSpec((1, l, D_V), lambda bi, h, i: (bi, 0, h))
    vt_spec = pl.BlockSpec((1, l // (2 * t), D_V, 2 * t), lambda bi, h, i: (bi, 0, h, 0))
    per_head = lambda rows: pl.BlockSpec((1, rows, t), lambda bi, h, i: (h, 0, 0))
    return pl.pallas_call(
        functools.partial(_prompt_attn_kernel, t=t, n_sub=ATT_SUB, n_q=l // tq),
        name="prompt_attn",
        grid=(b, H_ATT, l // tq),
        in_specs=[
            seq_spec, seq_spec, seq_spec, vt_spec,
            pl.BlockSpec((N_META, D_V), lambda bi, h, i: (0, h)),
            pl.BlockSpec((D_V, N_META), lambda bi, h, i: (h, 0)),
            per_head(t), per_head(2 * t), per_head(2 * t), per_head(N_META),
            tile_spec,
            pl.BlockSpec((1, D_V), lambda bi, h, i: (0, 0)),
            pl.BlockSpec((4, D_QK), lambda bi, h, i: (0, 0)),
        ],
        out_specs=tile_spec,
        out_shape=jax.ShapeDtypeStruct((b, l, W_ATT), bf16),
        scratch_shapes=[pltpu.VMEM((2, 2 * t + N_META, tq), f32),
                        pltpu.VMEM((2, 1, tq), f32),
                        pltpu.VMEM((2, 1, tq), f32), pltpu.VMEM((2, 1, tq), f32),
                        pltpu.VMEM((2, D_V, tq), f32)],
        compiler_params=pltpu.CompilerParams(
            dimension_semantics=("arbitrary", "arbitrary", "arbitrary"),
            vmem_limit_bytes=VMEM_LIMIT),
    )(qa, qb, kb, vt, km, vmt, bias_d0, bias_d10, bias_sub, bias_meta, gate, subln_g, lam_vecs)


def _merge(og, yc, x, wo_ref, fg):
    y = x + jnp.dot(og, wo_ref[0:W_ATT, :], preferred_element_type=f32)
    y = y + jnp.dot(yc, wo_ref[W_ATT:, :], preferred_element_type=f32)
    return _rms(y, fg)


def _out_proj_kernel(og_ref, yc_ref, x_ref, wo_ref, fg_ref, y_ref):
    y_ref[...] = _merge(og_ref[...], yc_ref[...], x_ref[...], wo_ref, fg_ref[...])


def _out_proj(og, yc, x, wo_bf, final_g, tm):
    n, d = x.shape
    w_att = og.shape[1]
    w_conv = yc.shape[1]
    row_spec = lambda width: pl.BlockSpec((tm, width), lambda i: (i, 0))
    return pl.pallas_call(
        _out_proj_kernel,
        name="out_proj",
        grid=(n // tm,),
        in_specs=[row_spec(w_att), row_spec(w_conv), row_spec(d),
                  pl.BlockSpec((w_att + w_conv, d), lambda i: (0, 0)),
                  pl.BlockSpec((1, d), lambda i: (0, 0))],
        out_specs=row_spec(d),
        out_shape=jax.ShapeDtypeStruct((n, d), f32),
        compiler_params=pltpu.CompilerParams(
            dimension_semantics=("arbitrary",), vmem_limit_bytes=VMEM_LIMIT),
    )(og, yc, x, wo_bf, final_g.reshape(1, d))


def _small_attn_kernel(*refs, has_cache):
    if has_cache:
        (qa_ref, qb_ref, kn_ref, vn_ref, bn_ref, ck_ref, cv_ref, bc_ref,
         gate_ref, yc_ref, x_ref, sg_ref, lam_ref, wo_ref, fg_ref, y_ref) = refs
    else:
        (qa_ref, qb_ref, kn_ref, vn_ref, bn_ref,
         gate_ref, yc_ref, x_ref, sg_ref, lam_ref, wo_ref, fg_ref, y_ref) = refs
    qs = (qa_ref[0], qb_ref[0])
    lam = _diff_lambda(lam_ref)
    nt = (((1,), (1,)), ((), ()))
    heads = []
    for h in range(H_ATT):
        cols = slice(h * D_V, (h + 1) * D_V)
        kn = kn_ref[0][:, cols]
        vn = vn_ref[0][:, cols]
        if has_cache:
            n_rows = ck_ref.shape[1] // H_ATT
            ck = ck_ref[0, pl.ds(h, n_rows, stride=H_ATT), :].astype(bf16)
            cv = cv_ref[0, pl.ds(h, n_rows, stride=H_ATT), :].astype(bf16)
        outs = []
        for c in range(2):
            qc = qs[c][:, cols]
            sn = lax.dot_general(qc, kn, nt, preferred_element_type=f32) + bn_ref[h]
            m = jnp.max(sn, axis=-1, keepdims=True)
            if has_cache:
                sc = lax.dot_general(qc, ck, nt, preferred_element_type=f32) + bc_ref[h]
                m = jnp.maximum(m, jnp.max(sc, axis=-1, keepdims=True))
            en = jnp.exp2(sn - m)
            l = jnp.sum(en, axis=-1, keepdims=True)
            acc = jnp.dot(en.astype(bf16), vn, preferred_element_type=f32)
            if has_cache:
                ec = jnp.exp2(sc - m)
                l = l + jnp.sum(ec, axis=-1, keepdims=True)
                acc = acc + jnp.dot(ec.astype(bf16), cv, preferred_element_type=f32)
            outs.append(acc / l)
        heads.append(_gate_heads(outs[0], outs[1], lam, sg_ref[...], gate_ref[0][:, cols]))
    og = jnp.concatenate(heads, axis=-1).astype(bf16)
    y_ref[0] = _merge(og, yc_ref[0], x_ref[0], wo_ref, fg_ref[...])


def _small_attn(qa, qb, kb, vb, bias_new, gate, yc, x, subln_g, lam_vecs, wo_bf, final_g,
                cache_k=None, cache_v=None, bias_cache=None):
    b, lq, d = x.shape
    has_cache = cache_k is not None
    per_b = lambda rows, width: pl.BlockSpec((1, rows, width), lambda i: (i, 0, 0))
    const = lambda shape: pl.BlockSpec(shape, lambda i: (0,) * len(shape))
    w = qa.shape[2]
    args = [qa, qb, kb, vb, bias_new]
    specs = [per_b(lq, w)] * 4 + [const(bias_new.shape)]
    if has_cache:
        args += [cache_k, cache_v, bias_cache]
        specs += [per_b(*cache_k.shape[1:]), per_b(*cache_v.shape[1:]), const(bias_cache.shape)]
    args += [gate, yc, x, subln_g, lam_vecs, wo_bf, final_g.reshape(1, d)]
    specs += [per_b(lq, w), per_b(lq, yc.shape[2]), per_b(lq, d), const(subln_g.shape),
              const(lam_vecs.shape), const(wo_bf.shape), const((1, d))]
    return pl.pallas_call(
        functools.partial(_small_attn_kernel, has_cache=has_cache),
        name="small_attn",
        grid=(b,),
        in_specs=specs,
        out_specs=per_b(lq, d),
        out_shape=jax.ShapeDtypeStruct((b, lq, d), f32),
        compiler_params=pltpu.CompilerParams(
            dimension_semantics=("arbitrary",), vmem_limit_bytes=VMEM_LIMIT),
    )(*args)


def kernel(x_prompt, x_sample, cache_k, cache_v, state_conv, meta_tokens, rel_bias, norm_g, w_in,
           conv_w, lambda_q1, lambda_k1, lambda_q2, lambda_k2, subln_g, w_out, final_g):
    bp, sp, d = x_prompt.shape
    bd, sd, _ = x_sample.shape
    depth = w_in.shape[0]
    assert depth == 1, "single-layer step only"
    past = cache_k.shape[2] - N_META
    t = ATT_TILE
    assert sp % (ATT_SUB * t) == 0 and t % CHUNK == 0

    w_bf = w_in[0].astype(bf16)
    wo_bf = w_out[0].astype(bf16)
    lam_vecs = jnp.stack([lambda_q1[0], lambda_k1[0], lambda_q2[0], lambda_k2[0]])
    sg = subln_g[0].reshape(1, D_V)
    wg = w_bf.shape[1] // N_SPLIT

    zeros_state = jnp.zeros((1, CONV_W - 1, wg), f32)
    (qma, qmb, km, kmb, vm, vmb, gm, ycm, cm) = _in_proj(
        meta_tokens[None], zeros_state, norm_g[0], w_bf, conv_w[0], tm=N_META)
    (qpa, qpb, kp, kpb, vp, vpt, gp, ycp, cp) = _in_proj(
        x_prompt, jnp.broadcast_to(cm, (bp, CONV_W - 1, wg)), norm_g[0], w_bf, conv_w[0],
        tm=PROJ_ROWS, row_offset=N_META, transpose_v=2 * t)
    kp, vp = _fill_meta_rows(km[0], vm[0], kp, vp)
    (qda, qdb, kd, kdb, vd, vdb, gd, ycd, cd) = _in_proj(
        x_sample, state_conv[0], norm_g[0], w_bf, conv_w[0], tm=sd)

    meta_pos = np.arange(-N_META, 0)
    tile_pos = np.arange(t)
    d_qpos = past + np.arange(sd)
    cache_pos = np.concatenate([meta_pos, np.arange(past)])
    near = tile_pos + 2 * t
    (bias_d0, bias_d10, bias_sub, bias_meta, bias_meta_self, bias_dec_new, bias_dec_cache) = (
        _build_biases([
            (near, np.arange(2 * t, 3 * t), True),
            (near, np.arange(t, 3 * t), True),
            (near, np.arange(0, 2 * t), True),
            (tile_pos, meta_pos, True),
            (meta_pos, meta_pos, False),
            (d_qpos, d_qpos, False),
            (d_qpos, cache_pos, False),
        ], rel_bias))
    for off in range(2, sp // t):
        _assert_far_past(tile_pos + off * t, tile_pos)
    for off in range(1, sp // t):
        _assert_far_past(tile_pos + off * t, meta_pos)

    og = _prompt_attn(qpa, qpb, kpb, vpt, kmb[0], vmb[0].T, bias_d0, bias_d10, bias_sub,
                      bias_meta, gp, sg, lam_vecs)
    y_prompt = _out_proj(og.reshape(bp * sp, W_ATT), ycp.reshape(bp * sp, wg),
                         x_prompt.reshape(bp * sp, d), wo_bf, final_g, tm=1024).reshape(bp, sp, d)

    _small_attn(qma, qmb, kmb, vmb, bias_meta_self, gm, ycm, meta_tokens[None], sg, lam_vecs,
                wo_bf, final_g)

    y_sample = _small_attn(
        qda, qdb, kdb, vdb, bias_dec_new, gd, ycd, x_sample, sg, lam_vecs, wo_bf, final_g,
        cache_k=cache_k[0].reshape(bd, (N_META + past) * H_ATT, D_V),
        cache_v=cache_v[0].reshape(bd, (N_META + past) * H_ATT, D_V),
        bias_cache=bias_dec_cache)

    return (y_prompt, y_sample,
            kp.reshape(1, bp, N_META + sp, H_ATT, D_V),
            vp.reshape(1, bp, N_META + sp, H_ATT, D_V),
            cp[None],
            kd.reshape(1, bd, sd, H_ATT, D_V),
            vd.reshape(1, bd, sd, H_ATT, D_V),
            cd[None])
```

```python
import functools
import math

import numpy as np
import jax
import jax.numpy as jnp
from jax import lax
from jax.experimental import pallas as pl
from jax.experimental.pallas import tpu as pltpu

f32 = jnp.float32
bf16 = jnp.bfloat16

CHUNK = 64
N_META = 16
H_ATT = 4
D_QK = 64
D_V = 2 * D_QK
W_ATT = H_ATT * D_V
CONV_W = 3
N_SPLIT = 8
N_BUCKETS = 32
MAX_DIST = 128
EPS = 1e-6
NEG = -1e30
SCALE = D_QK ** -0.5
LOG2E = math.log2(math.e)
LAM_INIT = 0.8 - 0.6 * math.exp(-0.3 * 0)
FAR_BUCKET = N_BUCKETS // 2 - 1

LANES = 128
ATT_TILE = 256
ATT_SUB = 4
VMEM_LIMIT = 48 * 1024 * 1024
PROJ_ROWS = 1024


def _silu(z):
    return z * (1.0 / (1.0 + jnp.exp(-z)))


def _rms(x, g):
    return x * lax.rsqrt(jnp.mean(x * x, axis=-1, keepdims=True) + EPS) * g


def _in_proj_kernel(x_ref, cinit_ref, g_ref, w_ref, cw_ref,
                    qa_ref, qb_ref, k_ref, kb_ref, v_ref, vb_ref, gate_ref, yc_ref, clast_ref,
                    carry_ref, *, tm, wg, n_seg, transpose_v):
    j = pl.program_id(1)

    @pl.when(j == 0)
    def _():
        carry_ref[...] = cinit_ref[0]

    x = x_ref[0]
    h = _rms(x, g_ref[...]).astype(bf16)

    def proj(g):
        return jnp.dot(h, w_ref[:, g * wg:(g + 1) * wg], preferred_element_type=f32)

    def store_heads(ref, a):
        for hd in range(H_ATT):
            ref[pl.ds(hd, tm, stride=H_ATT), :] = a[:, hd * D_V:(hd + 1) * D_V]

    q = (proj(0) * (SCALE * LOG2E)).astype(bf16)
    first_half = lax.broadcasted_iota(jnp.int32, q.shape, 1) % D_V < D_QK
    zero = jnp.zeros_like(q)
    qa_ref[0] = jnp.where(first_half, q, zero)
    qb_ref[0] = jnp.where(first_half, zero, q)
    k = proj(1)
    store_heads(k_ref, k)
    kb_ref[0] = k.astype(bf16)
    v = proj(2)
    store_heads(v_ref, v)
    if transpose_v:
        for ch in range(tm // transpose_v):
            vb_ref[ch] = v[ch * transpose_v:(ch + 1) * transpose_v].T.astype(bf16)
    else:
        vb_ref[...] = v.astype(bf16)
    gate_ref[0] = _silu(proj(3))

    cu = proj(5) * proj(6)
    prev = carry_ref[...]
    seg = tm // n_seg
    row = lax.broadcasted_iota(jnp.int32, cu.shape, 0) % seg

    def before(k):
        if n_seg == 1:
            return prev[0, k:k + 1]
        return jnp.concatenate(
            [jnp.broadcast_to(prev[sg, k:k + 1], (seg, wg)) for sg in range(n_seg)], axis=0)

    cu_m1 = jnp.where(row == 0, before(1), pltpu.roll(cu, 1, axis=0))
    cu_m2 = jnp.where(row == 0, before(0),
                      jnp.where(row == 1, before(1), pltpu.roll(cu, 2, axis=0)))
    cw = cw_ref[...]
    conv = cw[0:1] * cu_m2 + cw[1:2] * cu_m1 + cw[2:3] * cu
    yc_ref[0] = (_silu(proj(7)) * proj(4) * conv).astype(bf16)

    for sg in range(n_seg):
        last = cu[(sg + 1) * seg - 2:(sg + 1) * seg]
        carry_ref[sg] = last
        clast_ref[0, sg] = last


def _in_proj(x, conv_init, norm_g, w_bf, conv_w, tm, row_offset=0, transpose_v=0):
    b, l, d = x.shape
    w_all = w_bf.shape[1]
    wg = w_all // N_SPLIT
    assert wg == W_ATT
    nj = l // tm
    n_seg = conv_init.shape[1]
    assert n_seg == 1 or nj == 1
    row_spec = lambda width: pl.BlockSpec((1, tm, width), lambda i, j: (i, j, 0))
    const = lambda shape: pl.BlockSpec(shape, lambda i, j: (0,) * len(shape))
    state_spec = pl.BlockSpec((1, n_seg, CONV_W - 1, wg), lambda i, j: (i, 0, 0, 0))
    if row_offset:
        cache_spec = pl.BlockSpec(
            (None, pl.Element(tm * H_ATT), pl.Element(D_V)),
            lambda i, j: (i, pl.multiple_of((row_offset + j * tm) * H_ATT, 8 * H_ATT), 0))
    else:
        cache_spec = pl.BlockSpec((None, tm * H_ATT, D_V), lambda i, j: (i, j, 0))
    if transpose_v:
        assert tm % transpose_v == 0
        vb_spec = pl.BlockSpec((None, tm // transpose_v, wg, transpose_v),
                               lambda i, j: (i, j, 0, 0))
        vb_shape = jax.ShapeDtypeStruct((b, l // transpose_v, wg, transpose_v), bf16)
    else:
        vb_spec = pl.BlockSpec((None, tm, wg), lambda i, j: (i, j, 0))
        vb_shape = jax.ShapeDtypeStruct((b, l, wg), bf16)
    cache_shape = jax.ShapeDtypeStruct((b, (row_offset + l) * H_ATT, D_V), f32)
    out_shape = (
        jax.ShapeDtypeStruct((b, l, wg), bf16),
        jax.ShapeDtypeStruct((b, l, wg), bf16),
        cache_shape,
        jax.ShapeDtypeStruct((b, l, wg), bf16),
        cache_shape,
        vb_shape,
        jax.ShapeDtypeStruct((b, l, wg), f32),
        jax.ShapeDtypeStruct((b, l, wg), bf16),
        jax.ShapeDtypeStruct((b, n_seg, CONV_W - 1, wg), f32),
    )
    return pl.pallas_call(
        functools.partial(_in_proj_kernel, tm=tm, wg=wg, n_seg=n_seg, transpose_v=transpose_v),
        name="in_proj",
        grid=(b, nj),
        in_specs=[row_spec(d), state_spec, const((1, d)),
                  pl.BlockSpec((d, w_all), lambda i, j: (0, 0), pipeline_mode=pl.Buffered(1)),
                  const((CONV_W, wg))],
        out_specs=(row_spec(wg), row_spec(wg), cache_spec, row_spec(wg), cache_spec, vb_spec,
                   row_spec(wg), row_spec(wg), state_spec),
        out_shape=out_shape,
        scratch_shapes=[pltpu.VMEM((n_seg, CONV_W - 1, wg), f32)],
        compiler_params=pltpu.CompilerParams(
            dimension_semantics=("arbitrary", "arbitrary"), vmem_limit_bytes=VMEM_LIMIT),
    )(x, conv_init, norm_g.reshape(1, d), w_bf, conv_w)


def _fill_rows_kernel(km_ref, vm_ref, k_hbm, v_hbm, ko_ref, vo_ref):
    del k_hbm, v_hbm
    ko_ref[0] = km_ref[...]
    vo_ref[0] = vm_ref[...]


def _fill_meta_rows(k_meta, v_meta, k_all, v_all):
    b = k_all.shape[0]
    rows = k_meta.shape[0]
    small = pl.BlockSpec((rows, D_V), lambda i: (0, 0))
    lead = pl.BlockSpec((1, rows, D_V), lambda i: (i, 0, 0))
    whole = pl.BlockSpec(memory_space=pl.ANY)
    return pl.pallas_call(
        _fill_rows_kernel,
        name="fill_meta_rows",
        grid=(b,),
        in_specs=[small, small, whole, whole],
        out_specs=(lead, lead),
        out_shape=(jax.ShapeDtypeStruct(k_all.shape, k_all.dtype),
                   jax.ShapeDtypeStruct(v_all.shape, v_all.dtype)),
        input_output_aliases={2: 0, 3: 1},
    )(k_meta, v_meta, k_all, v_all)


def _rel_bucket(rel):
    nb = N_BUCKETS // 2
    ret = np.where(rel > 0, nb, 0)
    n = np.abs(rel)
    max_exact = nb // 2
    nf = np.maximum(n, 1).astype(np.float32)
    large = max_exact + (np.log(nf / np.float32(max_exact)) / np.float32(math.log(MAX_DIST / max_exact))
                         * np.float32(nb - max_exact)).astype(np.int32)
    large = np.minimum(large, nb - 1)
    return (ret + np.where(n < max_exact, n, large)).astype(np.int32)


def _chunk_id_np(pos):
    return np.where(pos < 0, -1, pos // CHUNK)


def _assert_far_past(q_pos, k_pos):
    rel = k_pos[None, :] - q_pos[:, None]
    nb = N_BUCKETS // 2
    max_exact = nb // 2
    n = np.abs(rel).astype(np.float64)
    large = max_exact + np.log(n / max_exact) / math.log(MAX_DIST / max_exact) * (nb - max_exact)
    assert np.all(rel < 0) and np.all(large >= nb), "tile is not in the saturated bucket"
    assert np.all(_chunk_id_np(k_pos)[None, :] <= _chunk_id_np(q_pos)[:, None])


def _bias_kernel(rb_ref, *refs):
    n = len(refs) // 3
    h = pl.program_id(0)
    far = rb_ref[FAR_BUCKET, h]
    shifted = [rb_ref[b, h] - far for b in range(N_BUCKETS)]
    for j in range(n):
        bucket = refs[2 * j][...]
        acc = jnp.zeros(bucket.shape, f32)
        for b in range(N_BUCKETS):
            acc = jnp.where(bucket == b, shifted[b], acc)
        refs[2 * n + j][0] = jnp.where(refs[2 * j + 1][...] != 0, acc * LOG2E, NEG)


def _build_biases(tiles, rel_bias):
    args, specs, out_specs, out_shapes = [], [], [], []
    for q_pos, k_pos, keys_major in tiles:
        rel = k_pos[None, :] - q_pos[:, None]
        vis = _chunk_id_np(k_pos)[None, :] <= _chunk_id_np(q_pos)[:, None]
        if keys_major:
            rel, vis = rel.T, vis.T
        r, c = rel.shape
        args += [jnp.asarray(_rel_bucket(rel)), jnp.asarray(vis.astype(np.int32))]
        specs += [pl.BlockSpec((r, c), lambda h: (0, 0))] * 2
        out_specs.append(pl.BlockSpec((1, r, c), lambda h: (h, 0, 0)))
        out_shapes.append(jax.ShapeDtypeStruct((H_ATT, r, c), f32))
    return pl.pallas_call(
        _bias_kernel,
        name="bias_tiles",
        grid=(H_ATT,),
        in_specs=[pl.BlockSpec(memory_space=pltpu.SMEM)] + specs,
        out_specs=tuple(out_specs),
        out_shape=tuple(out_shapes),
    )(rel_bias, *args)


def _diff_lambda(lam_ref):
    lv = lam_ref[...]
    s1 = jnp.sum(lv[0:1] * lv[1:2], axis=-1, keepdims=True)
    s2 = jnp.sum(lv[2:3] * lv[3:4], axis=-1, keepdims=True)
    return jnp.exp(s1) - jnp.exp(s2) + LAM_INIT


def _gate_heads(o1, o2, lam, sg, gate):
    o = o1 - lam * o2
    return _rms(o, sg) * (1.0 - LAM_INIT) * gate


def _prompt_attn_kernel(qa_ref, qb_ref, k_ref, vt_ref, km_ref, vmt_ref, bn_ref, bm_ref, gate_ref,
                        sg_ref, lam_ref, o_ref, sa_sc, mx_sc, m_sc, l_sc, acc_sc,
                        *, t, n_sub, n_q):
    i = pl.program_id(2)
    t2 = 2 * t
    n_wide = i * (n_sub // 2)
    slots = [(c, p) for c in range(2) for p in range(n_sub)]

    def keys(start, rows):
        return k_ref[0, pl.ds(pl.multiple_of(start, t), rows), :]

    def cols(p):
        return slice(p * t, (p + 1) * t)

    def qk(kt, j, c, p):
        rows = pl.ds(pl.multiple_of((j * n_sub + p) * t, t), t)
        return lax.dot_general(kt, (qa_ref, qb_ref)[c][0, rows, :], (((1,), (1,)), ((), ())),
                               preferred_element_type=f32)

    def col_max(*parts):
        m = jnp.max(parts[0], axis=0, keepdims=True)
        for part in parts[1:]:
            m = jnp.maximum(m, jnp.max(part, axis=0, keepdims=True))
        return m

    def update(c, p, s_parts, vt_parts):
        m_prev = m_sc[c, :, cols(p)]
        m_new = jnp.maximum(m_prev, mx_sc[c, :, cols(p)])
        alpha = jnp.exp2(m_prev - m_new)
        e_sum = pv = None
        for s, vt in zip(s_parts, vt_parts):
            e = jnp.exp2(s - m_new)
            e_k = jnp.sum(e, axis=0, keepdims=True)
            pv_k = jnp.dot(vt, e.astype(bf16), preferred_element_type=f32)
            e_sum = e_k if e_sum is None else e_sum + e_k
            pv = pv_k if pv is None else pv + pv_k
        l_sc[c, :, cols(p)] = alpha * l_sc[c, :, cols(p)] + e_sum
        acc_sc[c, :, cols(p)] = alpha * acc_sc[c, :, cols(p)] + pv
        m_sc[c, :, cols(p)] = m_new

    def near_bias(s, kind):
        if kind == "both":
            return s + bn_ref[0]
        if kind == "own":
            return s + bn_ref[0, t:t2]
        return jnp.concatenate([s[:t], s[t:] + bn_ref[0, 0:t]], axis=0)

    def far_refill(kt, j, c, p, bias=None):
        s = qk(kt, j, c, p)
        if bias is not None:
            s = near_bias(s, bias)
        sa_sc[c, 0:t2, cols(p)] = s
        mx_sc[c, :, cols(p)] = col_max(s)

    def far_update(c, p, vt):
        update(c, p, [sa_sc[c, 0:t2, cols(p)]], [vt])

    def far_tile(w, refill):
        vt = vt_ref[0, w]
        for c, p in slots:
            far_update(c, p, vt)
            refill(c, p)

    def own_a_rows(p):
        return t if p == 0 else t2

    def own_a_refill(j, c, p, meta_bias=None):
        rows = own_a_rows(p)
        s = qk(keys(n_wide * t2, rows), j, c, p)
        if p < 3:
            s = near_bias(s, ("own", "both", "before")[p])
        sm = qk(km_ref[...], j, c, p)
        if meta_bias is not None:
            sm = sm + meta_bias
        sa_sc[c, 0:rows, cols(p)] = s
        sa_sc[c, t2:t2 + N_META, cols(p)] = sm
        mx_sc[c, :, cols(p)] = col_max(s, sm)

    def own_a_update(c, p):
        rows = own_a_rows(p)
        update(c, p, [sa_sc[c, 0:rows, cols(p)], sa_sc[c, t2:t2 + N_META, cols(p)]],
               [vt_ref[0, n_wide, :, 0:rows], vmt_ref[...]])

    def own_b_rows(p):
        return t if p == 2 else t2

    def own_b_refill(j, c, p):
        rows = own_b_rows(p)
        s = near_bias(qk(keys((n_wide + 1) * t2, rows), j, c, p), ("own", "both")[p - 2])
        sa_sc[c, 0:rows, cols(p)] = s
        mx_sc[c, :, cols(p)] = col_max(s)

    def own_b_update(c, p):
        rows = own_b_rows(p)
        update(c, p, [sa_sc[c, 0:rows, cols(p)]], [vt_ref[0, n_wide + 1, :, 0:rows]])

    m_sc[...] = jnp.full(m_sc.shape, NEG, f32)
    l_sc[...] = jnp.zeros(l_sc.shape, f32)
    acc_sc[...] = jnp.zeros(acc_sc.shape, f32)

    def far_pair(pp, carry):
        w = 2 * pp
        kt1, kt2 = keys((w + 1) * t2, t2), keys((w + 2) * t2, t2)
        far_tile(w, lambda c, p: far_refill(kt1, i, c, p))
        far_tile(w + 1, lambda c, p: far_refill(kt2, i, c, p))
        return carry

    def own_keys_and_finish():
        j_next = jnp.minimum(i + 1, n_q - 1)
        kt0 = keys(0, t2)
        for c, p in slots:
            own_a_update(c, p)
            if p < 2:
                far_refill(kt0, j_next, c, p)
            else:
                own_b_refill(i, c, p)
        for c, p in slots:
            if p >= 2:
                own_b_update(c, p)
                far_refill(kt0, j_next, c, p)
        o = acc_sc[0] / l_sc[0] - _diff_lambda(lam_ref) * (acc_sc[1] / l_sc[1])
        o = o * lax.rsqrt(jnp.mean(o * o, axis=0, keepdims=True) + EPS)
        o_ref[0] = (o.T * (sg_ref[...] * (1.0 - LAM_INIT)) * gate_ref[0]).astype(bf16)

    @pl.when(i == 0)
    def _():
        for c, p in slots:
            own_a_refill(i, c, p, bm_ref[0] if p == 0 else None)
        own_keys_and_finish()

    @pl.when(i >= 1)
    def _():
        lax.fori_loop(0, i - 1, far_pair, 0)
        kt1 = keys((n_wide - 1) * t2, t2)
        far_tile(n_wide - 2,
                 lambda c, p: far_refill(kt1, i, c, p, "before" if p == 0 else None))
        far_tile(n_wide - 1, lambda c, p: own_a_refill(i, c, p))
        own_keys_and_finish()


def _prompt_attn(qa, qb, kb, vt, km, vmt, bias_near, bias_meta, gate, subln_g, lam_vecs):
    b, l, _ = qa.shape
    t = ATT_TILE
    tq = ATT_SUB * t
    assert ATT_SUB == 4 and l % tq == 0
    assert vt.shape == (b, l // (2 * t), W_ATT, 2 * t)
    tile_spec = pl.BlockSpec((1, tq, D_V), lambda bi, h, i: (bi, i, h))
    seq_spec = pl.BlockSpec((1, l, D_V), lambda bi, h, i: (bi, 0, h))
    vt_spec = pl.BlockSpec((1, l // (2 * t), D_V, 2 * t), lambda bi, h, i: (bi, 0, h, 0))
    per_head = lambda rows: pl.BlockSpec((1, rows, t), lambda bi, h, i: (h, 0, 0))
    return pl.pallas_call(
        functools.partial(_prompt_attn_kernel, t=t, n_sub=ATT_SUB, n_q=l // tq),
        name="prompt_attn",
        grid=(b, H_ATT, l // tq),
        in_specs=[
            seq_spec, seq_spec, seq_spec, vt_spec,
            pl.BlockSpec((N_META, D_V), lambda bi, h, i: (0, h)),
            pl.BlockSpec((D_V, N_META), lambda bi, h, i: (h, 0)),
            per_head(2 * t), per_head(N_META),
            tile_spec,
            pl.BlockSpec((1, D_V), lambda bi, h, i: (0, 0)),
            pl.BlockSpec((4, D_QK), lambda bi, h, i: (0, 0)),
        ],
        out_specs=tile_spec,
        out_shape=jax.ShapeDtypeStruct((b, l, W_ATT), bf16),
        scratch_shapes=[pltpu.VMEM((2, 2 * t + N_META, tq), f32),
                        pltpu.VMEM((2, 1, tq), f32),
                        pltpu.VMEM((2, 1, tq), f32), pltpu.VMEM((2, 1, tq), f32),
                        pltpu.VMEM((2, D_V, tq), f32)],
        compiler_params=pltpu.CompilerParams(
            dimension_semantics=("arbitrary", "arbitrary", "arbitrary"),
            vmem_limit_bytes=VMEM_LIMIT),
    )(qa, qb, kb, vt, km, vmt, bias_near, bias_meta, gate, subln_g, lam_vecs)


def _merge(og, yc, x, wo_ref, fg):
    y = x + jnp.dot(og, wo_ref[0:W_ATT, :], preferred_element_type=f32)
    y = y + jnp.dot(yc, wo_ref[W_ATT:, :], preferred_element_type=f32)
    return _rms(y, fg)


def _out_proj_kernel(og_ref, yc_ref, x_ref, wo_ref, fg_ref, y_ref):
    y_ref[...] = _merge(og_ref[...], yc_ref[...], x_ref[...], wo_ref, fg_ref[...])


def _out_proj(og, yc, x, wo_bf, final_g, tm):
    n, d = x.shape
    w_att = og.shape[1]
    w_conv = yc.shape[1]
    row_spec = lambda width: pl.BlockSpec((tm, width), lambda i: (i, 0))
    return pl.pallas_call(
        _out_proj_kernel,
        name="out_proj",
        grid=(n // tm,),
        in_specs=[row_spec(w_att), row_spec(w_conv), row_spec(d),
                  pl.BlockSpec((w_att + w_conv, d), lambda i: (0, 0)),
                  pl.BlockSpec((1, d), lambda i: (0, 0))],
        out_specs=row_spec(d),
        out_shape=jax.ShapeDtypeStruct((n, d), f32),
        compiler_params=pltpu.CompilerParams(
            dimension_semantics=("arbitrary",), vmem_limit_bytes=VMEM_LIMIT),
    )(og, yc, x, wo_bf, final_g.reshape(1, d))


def _small_attn_kernel(*refs, has_cache):
    if has_cache:
        (qa_ref, qb_ref, kn_ref, vn_ref, bn_ref, ck_ref, cv_ref, bc_ref,
         gate_ref, yc_ref, x_ref, sg_ref, lam_ref, wo_ref, fg_ref, y_ref) = refs
    else:
        (qa_ref, qb_ref, kn_ref, vn_ref, bn_ref,
         gate_ref, yc_ref, x_ref, sg_ref, lam_ref, wo_ref, fg_ref, y_ref) = refs
    lq = qa_ref.shape[1]
    lam = _diff_lambda(lam_ref)
    nt = (((1,), (1,)), ((), ()))
    both = lambda a: jnp.concatenate([a, a], axis=0)
    heads = []
    for h in range(H_ATT):
        cols = slice(h * D_V, (h + 1) * D_V)
        q2 = jnp.concatenate([qa_ref[0][:, cols], qb_ref[0][:, cols]], axis=0)
        kn = kn_ref[0][:, cols]
        vn = vn_ref[0][:, cols]
        sn = lax.dot_general(q2, kn, nt, preferred_element_type=f32) + both(bn_ref[h])
        m = jnp.max(sn, axis=-1, keepdims=True)
        if has_cache:
            n_rows = ck_ref.shape[1] // H_ATT
            ck = ck_ref[0, pl.ds(h, n_rows, stride=H_ATT), :].astype(bf16)
            cv = cv_ref[0, pl.ds(h, n_rows, stride=H_ATT), :].astype(bf16)
            sc = lax.dot_general(q2, ck, nt, preferred_element_type=f32) + both(bc_ref[h])
            m = jnp.maximum(m, jnp.max(sc, axis=-1, keepdims=True))
        en = jnp.exp2(sn - m)
        l = jnp.sum(en, axis=-1, keepdims=True)
        acc = jnp.dot(en.astype(bf16), vn, preferred_element_type=f32)
        if has_cache:
            ec = jnp.exp2(sc - m)
            l = l + jnp.sum(ec, axis=-1, keepdims=True)
            acc = acc + jnp.dot(ec.astype(bf16), cv, preferred_element_type=f32)
        o = acc / l
        heads.append(_gate_heads(o[:lq], o[lq:], lam, sg_ref[...], gate_ref[0][:, cols]))
    og = jnp.concatenate(heads, axis=-1).astype(bf16)
    y_ref[0] = _merge(og, yc_ref[0], x_ref[0], wo_ref, fg_ref[...])


def _small_attn(qa, qb, kb, vb, bias_new, gate, yc, x, subln_g, lam_vecs, wo_bf, final_g,
                cache_k=None, cache_v=None, bias_cache=None):
    b, lq, d = x.shape
    has_cache = cache_k is not None
    per_b = lambda rows, width: pl.BlockSpec((1, rows, width), lambda i: (i, 0, 0))
    const = lambda shape: pl.BlockSpec(shape, lambda i: (0,) * len(shape))
    w = qa.shape[2]
    args = [qa, qb, kb, vb, bias_new]
    specs = [per_b(lq, w)] * 4 + [const(bias_new.shape)]
    if has_cache:
        args += [cache_k, cache_v, bias_cache]
        specs += [per_b(*cache_k.shape[1:]), per_b(*cache_v.shape[1:]), const(bias_cache.shape)]
    args += [gate, yc, x, subln_g, lam_vecs, wo_bf, final_g.reshape(1, d)]
    specs += [per_b(lq, w), per_b(lq, yc.shape[2]), per_b(lq, d), const(subln_g.shape),
              const(lam_vecs.shape), const(wo_bf.shape), const((1, d))]
    return pl.pallas_call(
        functools.partial(_small_attn_kernel, has_cache=has_cache),
        name="small_attn",
        grid=(b,),
        in_specs=specs,
        out_specs=per_b(lq, d),
        out_shape=jax.ShapeDtypeStruct((b, lq, d), f32),
        compiler_params=pltpu.CompilerParams(
            dimension_semantics=("arbitrary",), vmem_limit_bytes=VMEM_LIMIT),
    )(*args)


def kernel(x_prompt, x_sample, cache_k, cache_v, state_conv, meta_tokens, rel_bias, norm_g, w_in,
           conv_w, lambda_q1, lambda_k1, lambda_q2, lambda_k2, subln_g, w_out, final_g):
    bp, sp, d = x_prompt.shape
    bd, sd, _ = x_sample.shape
    depth = w_in.shape[0]
    assert depth == 1, "single-layer step only"
    past = cache_k.shape[2] - N_META
    t = ATT_TILE
    assert sp % (ATT_SUB * t) == 0 and t % CHUNK == 0

    w_bf = w_in[0].astype(bf16)
    wo_bf = w_out[0].astype(bf16)
    lam_vecs = jnp.stack([lambda_q1[0], lambda_k1[0], lambda_q2[0], lambda_k2[0]])
    sg = subln_g[0].reshape(1, D_V)
    wg = w_bf.shape[1] // N_SPLIT

    zeros_state = jnp.zeros((1, 1, CONV_W - 1, wg), f32)
    (qma, qmb, km, kmb, vm, vmb, gm, ycm, cm) = _in_proj(
        meta_tokens[None], zeros_state, norm_g[0], w_bf, conv_w[0], tm=N_META)
    (qpa, qpb, kp, kpb, vp, vpt, gp, ycp, cp) = _in_proj(
        x_prompt, jnp.broadcast_to(cm, (bp, 1, CONV_W - 1, wg)), norm_g[0], w_bf, conv_w[0],
        tm=PROJ_ROWS, row_offset=N_META, transpose_v=2 * t)
    kp, vp = _fill_meta_rows(km[0], vm[0], kp, vp)
    per_stream = lambda a: a.reshape(bd, sd, a.shape[-1])
    (qda, qdb, kd, kdb, vd, vdb, gd, ycd, cd) = _in_proj(
        x_sample.reshape(1, bd * sd, d), state_conv[0][None], norm_g[0], w_bf, conv_w[0],
        tm=bd * sd)
    qda, qdb, kdb, vdb, gd, ycd = map(per_stream, (qda, qdb, kdb, vdb, gd, ycd))

    meta_pos = np.arange(-N_META, 0)
    tile_pos = np.arange(t)
    d_qpos = past + np.arange(sd)
    cache_pos = np.concatenate([meta_pos, np.arange(past)])
    (bias_near, bias_meta, bias_meta_self, bias_dec_new, bias_dec_cache) = (
        _build_biases([
            (tile_pos + t, np.arange(0, 2 * t), True),
            (tile_pos, meta_pos, True),
            (meta_pos, meta_pos, False),
            (d_qpos, d_qpos, False),
            (d_qpos, cache_pos, False),
        ], rel_bias))
    for off in range(2, sp // t):
        _assert_far_past(tile_pos + off * t, tile_pos)
    for off in range(1, sp // t):
        _assert_far_past(tile_pos + off * t, meta_pos)

    og = _prompt_attn(qpa, qpb, kpb, vpt, kmb[0], vmb[0].T, bias_near, bias_meta, gp, sg,
                      lam_vecs)
    y_prompt = _out_proj(og.reshape(bp * sp, W_ATT), ycp.reshape(bp * sp, wg),
                         x_prompt.reshape(bp * sp, d), wo_bf, final_g, tm=1024).reshape(bp, sp, d)

    _small_attn(qma, qmb, kmb, vmb, bias_meta_self, gm, ycm, meta_tokens[None], sg, lam_vecs,
                wo_bf, final_g)

    y_sample = _small_attn(
        qda, qdb, kdb, vdb, bias_dec_new, gd, ycd, x_sample, sg, lam_vecs, wo_bf, final_g,
        cache_k=cache_k[0].reshape(bd, (N_META + past) * H_ATT, D_V),
        cache_v=cache_v[0].reshape(bd, (N_META + past) * H_ATT, D_V),
        bias_cache=bias_dec_cache)

    return (y_prompt, y_sample,
            kp.reshape(1, bp, N_META + sp, H_ATT, D_V),
            vp.reshape(1, bp, N_META + sp, H_ATT, D_V),
            cp.reshape(1, bp, CONV_W - 1, wg),
            kd.reshape(1, bd, sd, H_ATT, D_V),
            vd.reshape(1, bd, sd, H_ATT, D_V),
            cd)
```

```python
import functools
import math

import numpy as np
import jax
import jax.numpy as jnp
from jax import lax
from jax.experimental import pallas as pl
from jax.experimental.pallas import tpu as pltpu

f32 = jnp.float32
bf16 = jnp.bfloat16

CHUNK = 64
N_META = 16
H_ATT = 4
D_QK = 64
D_V = 2 * D_QK
W_ATT = H_ATT * D_V
CONV_W = 3
N_SPLIT = 8
N_BUCKETS = 32
MAX_DIST = 128
EPS = 1e-6
NEG = -1e30
SCALE = D_QK ** -0.5
LOG2E = math.log2(math.e)
LAM_INIT = 0.8 - 0.6 * math.exp(-0.3 * 0)
FAR_BUCKET = N_BUCKETS // 2 - 1

LANES = 128
ATT_TILE = 256
ATT_SUB = 4
VMEM_LIMIT = 48 * 1024 * 1024
PROJ_ROWS = 1024


def _silu(z):
    return z * (1.0 / (1.0 + jnp.exp(-z)))


def _rms(x, g):
    return x * lax.rsqrt(jnp.mean(x * x, axis=-1, keepdims=True) + EPS) * g


def _in_proj_kernel(x_ref, cinit_ref, g_ref, w_ref, cw_ref,
                    qa_ref, qb_ref, k_ref, kb_ref, v_ref, vb_ref, gate_ref, yc_ref, clast_ref,
                    carry_ref, *, tm, wg, n_seg, transpose_v):
    j = pl.program_id(1)

    @pl.when(j == 0)
    def _():
        carry_ref[...] = cinit_ref[0]

    x = x_ref[0]
    h = _rms(x, g_ref[...]).astype(bf16)

    def proj(g):
        return jnp.dot(h, w_ref[:, g * wg:(g + 1) * wg], preferred_element_type=f32)

    def store_heads(ref, a):
        for hd in range(H_ATT):
            ref[pl.ds(hd, tm, stride=H_ATT), :] = a[:, hd * D_V:(hd + 1) * D_V]

    q = (proj(0) * (SCALE * LOG2E)).astype(bf16)
    first_half = lax.broadcasted_iota(jnp.int32, q.shape, 1) % D_V < D_QK
    zero = jnp.zeros_like(q)
    qa_ref[0] = jnp.where(first_half, q, zero)
    qb_ref[0] = jnp.where(first_half, zero, q)
    k = proj(1)
    store_heads(k_ref, k)
    kb_ref[0] = k.astype(bf16)
    v = proj(2)
    store_heads(v_ref, v)
    if transpose_v:
        for ch in range(tm // transpose_v):
            vb_ref[ch] = v[ch * transpose_v:(ch + 1) * transpose_v].T.astype(bf16)
    else:
        vb_ref[...] = v.astype(bf16)
    gate_ref[0] = _silu(proj(3))

    cu = proj(5) * proj(6)
    prev = carry_ref[...]
    seg = tm // n_seg
    row = lax.broadcasted_iota(jnp.int32, cu.shape, 0) % seg

    def before(k):
        if n_seg == 1:
            return prev[0, k:k + 1]
        return jnp.concatenate(
            [jnp.broadcast_to(prev[sg, k:k + 1], (seg, wg)) for sg in range(n_seg)], axis=0)

    cu_m1 = jnp.where(row == 0, before(1), pltpu.roll(cu, 1, axis=0))
    cu_m2 = jnp.where(row == 0, before(0),
                      jnp.where(row == 1, before(1), pltpu.roll(cu, 2, axis=0)))
    cw = cw_ref[...]
    conv = cw[0:1] * cu_m2 + cw[1:2] * cu_m1 + cw[2:3] * cu
    yc_ref[0] = (_silu(proj(7)) * proj(4) * conv).astype(bf16)

    for sg in range(n_seg):
        last = cu[(sg + 1) * seg - 2:(sg + 1) * seg]
        carry_ref[sg] = last
        clast_ref[0, sg] = last


def _in_proj(x, conv_init, norm_g, w_bf, conv_w, tm, row_offset=0, transpose_v=0):
    b, l, d = x.shape
    w_all = w_bf.shape[1]
    wg = w_all // N_SPLIT
    assert wg == W_ATT
    nj = l // tm
    n_seg = conv_init.shape[1]
    assert n_seg == 1 or nj == 1
    row_spec = lambda width: pl.BlockSpec((1, tm, width), lambda i, j: (i, j, 0))
    const = lambda shape: pl.BlockSpec(shape, lambda i, j: (0,) * len(shape))
    state_spec = pl.BlockSpec((1, n_seg, CONV_W - 1, wg), lambda i, j: (i, 0, 0, 0))
    if row_offset:
        cache_spec = pl.BlockSpec(
            (None, pl.Element(tm * H_ATT), pl.Element(D_V)),
            lambda i, j: (i, pl.multiple_of((row_offset + j * tm) * H_ATT, 8 * H_ATT), 0))
    else:
        cache_spec = pl.BlockSpec((None, tm * H_ATT, D_V), lambda i, j: (i, j, 0))
    if transpose_v:
        assert tm % transpose_v == 0
        vb_spec = pl.BlockSpec((None, tm // transpose_v, wg, transpose_v),
                               lambda i, j: (i, j, 0, 0))
        vb_shape = jax.ShapeDtypeStruct((b, l // transpose_v, wg, transpose_v), bf16)
    else:
        vb_spec = pl.BlockSpec((None, tm, wg), lambda i, j: (i, j, 0))
        vb_shape = jax.ShapeDtypeStruct((b, l, wg), bf16)
    cache_shape = jax.ShapeDtypeStruct((b, (row_offset + l) * H_ATT, D_V), f32)
    out_shape = (
        jax.ShapeDtypeStruct((b, l, wg), bf16),
        jax.ShapeDtypeStruct((b, l, wg), bf16),
        cache_shape,
        jax.ShapeDtypeStruct((b, l, wg), bf16),
        cache_shape,
        vb_shape,
        jax.ShapeDtypeStruct((b, l, wg), f32),
        jax.ShapeDtypeStruct((b, l, wg), bf16),
        jax.ShapeDtypeStruct((b, n_seg, CONV_W - 1, wg), f32),
    )
    return pl.pallas_call(
        functools.partial(_in_proj_kernel, tm=tm, wg=wg, n_seg=n_seg, transpose_v=transpose_v),
        name="in_proj",
        grid=(b, nj),
        in_specs=[row_spec(d), state_spec, const((1, d)),
                  pl.BlockSpec((d, w_all), lambda i, j: (0, 0), pipeline_mode=pl.Buffered(1)),
                  const((CONV_W, wg))],
        out_specs=(row_spec(wg), row_spec(wg), cache_spec, row_spec(wg), cache_spec, vb_spec,
                   row_spec(wg), row_spec(wg), state_spec),
        out_shape=out_shape,
        scratch_shapes=[pltpu.VMEM((n_seg, CONV_W - 1, wg), f32)],
        compiler_params=pltpu.CompilerParams(
            dimension_semantics=("arbitrary", "arbitrary"), vmem_limit_bytes=VMEM_LIMIT),
    )(x, conv_init, norm_g.reshape(1, d), w_bf, conv_w)


def _fill_rows_kernel(km_ref, vm_ref, k_hbm, v_hbm, ko_ref, vo_ref):
    del k_hbm, v_hbm
    ko_ref[0] = km_ref[...]
    vo_ref[0] = vm_ref[...]


def _fill_meta_rows(k_meta, v_meta, k_all, v_all):
    b = k_all.shape[0]
    rows = k_meta.shape[0]
    small = pl.BlockSpec((rows, D_V), lambda i: (0, 0))
    lead = pl.BlockSpec((1, rows, D_V), lambda i: (i, 0, 0))
    whole = pl.BlockSpec(memory_space=pl.ANY)
    return pl.pallas_call(
        _fill_rows_kernel,
        name="fill_meta_rows",
        grid=(b,),
        in_specs=[small, small, whole, whole],
        out_specs=(lead, lead),
        out_shape=(jax.ShapeDtypeStruct(k_all.shape, k_all.dtype),
                   jax.ShapeDtypeStruct(v_all.shape, v_all.dtype)),
        input_output_aliases={2: 0, 3: 1},
    )(k_meta, v_meta, k_all, v_all)


def _rel_bucket(rel):
    nb = N_BUCKETS // 2
    ret = np.where(rel > 0, nb, 0)
    n = np.abs(rel)
    max_exact = nb // 2
    nf = np.maximum(n, 1).astype(np.float32)
    large = max_exact + (np.log(nf / np.float32(max_exact)) / np.float32(math.log(MAX_DIST / max_exact))
                         * np.float32(nb - max_exact)).astype(np.int32)
    large = np.minimum(large, nb - 1)
    return (ret + np.where(n < max_exact, n, large)).astype(np.int32)


def _chunk_id_np(pos):
    return np.where(pos < 0, -1, pos // CHUNK)


def _assert_far_past(q_pos, k_pos):
    rel = k_pos[None, :] - q_pos[:, None]
    nb = N_BUCKETS // 2
    max_exact = nb // 2
    n = np.abs(rel).astype(np.float64)
    large = max_exact + np.log(n / max_exact) / math.log(MAX_DIST / max_exact) * (nb - max_exact)
    assert np.all(rel < 0) and np.all(large >= nb), "tile is not in the saturated bucket"
    assert np.all(_chunk_id_np(k_pos)[None, :] <= _chunk_id_np(q_pos)[:, None])


def _bias_kernel(rb_ref, *refs):
    n = len(refs) // 3
    h = pl.program_id(0)
    far = rb_ref[FAR_BUCKET, h]
    shifted = [rb_ref[b, h] - far for b in range(N_BUCKETS)]
    for j in range(n):
        bucket = refs[2 * j][...]
        acc = jnp.zeros(bucket.shape, f32)
        for b in range(N_BUCKETS):
            acc = jnp.where(bucket == b, shifted[b], acc)
        refs[2 * n + j][0] = jnp.where(refs[2 * j + 1][...] != 0, acc * LOG2E, NEG)


def _build_biases(tiles, rel_bias):
    args, specs, out_specs, out_shapes = [], [], [], []
    for q_pos, k_pos, keys_major in tiles:
        rel = k_pos[None, :] - q_pos[:, None]
        vis = _chunk_id_np(k_pos)[None, :] <= _chunk_id_np(q_pos)[:, None]
        if keys_major:
            rel, vis = rel.T, vis.T
        r, c = rel.shape
        args += [jnp.asarray(_rel_bucket(rel)), jnp.asarray(vis.astype(np.int32))]
        specs += [pl.BlockSpec((r, c), lambda h: (0, 0))] * 2
        out_specs.append(pl.BlockSpec((1, r, c), lambda h: (h, 0, 0)))
        out_shapes.append(jax.ShapeDtypeStruct((H_ATT, r, c), f32))
    return pl.pallas_call(
        _bias_kernel,
        name="bias_tiles",
        grid=(H_ATT,),
        in_specs=[pl.BlockSpec(memory_space=pltpu.SMEM)] + specs,
        out_specs=tuple(out_specs),
        out_shape=tuple(out_shapes),
    )(rel_bias, *args)


def _diff_lambda(lam_ref):
    lv = lam_ref[...]
    s1 = jnp.sum(lv[0:1] * lv[1:2], axis=-1, keepdims=True)
    s2 = jnp.sum(lv[2:3] * lv[3:4], axis=-1, keepdims=True)
    return jnp.exp(s1) - jnp.exp(s2) + LAM_INIT


def _gate_heads(o1, o2, lam, sg, gate):
    o = o1 - lam * o2
    return _rms(o, sg) * (1.0 - LAM_INIT) * gate


def _prompt_attn_kernel(qa_ref, qb_ref, k_ref, vt_ref, km_ref, vmt_ref, bn_ref, bm_ref, gate_ref,
                        sg_ref, lam_ref, x_ref, yc_ref, wo_ref, fg_ref, y_ref,
                        sa_sc, mx_sc, m_sc, l_sc, acc_sc, og_sc, *, t, n_sub, n_q):
    i = pl.program_id(2)
    t2 = 2 * t
    n_wide = i * (n_sub // 2)
    slots = [(c, p) for c in range(2) for p in range(n_sub)]

    def keys(start, rows):
        return k_ref[0, pl.ds(pl.multiple_of(start, t), rows), :]

    def cols(p):
        return slice(p * t, (p + 1) * t)

    def qk(kt, j, c, p):
        rows = pl.ds(pl.multiple_of((j * n_sub + p) * t, t), t)
        return lax.dot_general(kt, (qa_ref, qb_ref)[c][0, rows, :], (((1,), (1,)), ((), ())),
                               preferred_element_type=f32)

    def col_max(*parts):
        m = jnp.max(parts[0], axis=0, keepdims=True)
        for part in parts[1:]:
            m = jnp.maximum(m, jnp.max(part, axis=0, keepdims=True))
        return m

    def update(c, p, s_parts, vt_parts):
        m_prev = m_sc[c, :, cols(p)]
        m_new = jnp.maximum(m_prev, mx_sc[c, :, cols(p)])
        alpha = jnp.exp2(m_prev - m_new)
        e_sum = pv = None
        for s, vt in zip(s_parts, vt_parts):
            e = jnp.exp2(s - m_new)
            e_k = jnp.sum(e, axis=0, keepdims=True)
            pv_k = jnp.dot(vt, e.astype(bf16), preferred_element_type=f32)
            e_sum = e_k if e_sum is None else e_sum + e_k
            pv = pv_k if pv is None else pv + pv_k
        l_sc[c, :, cols(p)] = alpha * l_sc[c, :, cols(p)] + e_sum
        acc_sc[c, :, cols(p)] = alpha * acc_sc[c, :, cols(p)] + pv
        m_sc[c, :, cols(p)] = m_new

    def near_bias(s, kind):
        if kind == "both":
            return s + bn_ref[0]
        if kind == "own":
            return s + bn_ref[0, t:t2]
        return jnp.concatenate([s[:t], s[t:] + bn_ref[0, 0:t]], axis=0)

    def far_refill(kt, j, c, p, bias=None):
        s = qk(kt, j, c, p)
        if bias is not None:
            s = near_bias(s, bias)
        sa_sc[c, 0:t2, cols(p)] = s
        mx_sc[c, :, cols(p)] = col_max(s)

    def far_update(c, p, vt):
        update(c, p, [sa_sc[c, 0:t2, cols(p)]], [vt])

    def far_tile(w, refill):
        vt = vt_ref[0, w]
        for c, p in slots:
            far_update(c, p, vt)
            refill(c, p)

    def own_a_rows(p):
        return t if p == 0 else t2

    def own_a_refill(j, c, p, meta_bias=None):
        rows = own_a_rows(p)
        s = qk(keys(n_wide * t2, rows), j, c, p)
        if p < 3:
            s = near_bias(s, ("own", "both", "before")[p])
        sm = qk(km_ref[...], j, c, p)
        if meta_bias is not None:
            sm = sm + meta_bias
        sa_sc[c, 0:rows, cols(p)] = s
        sa_sc[c, t2:t2 + N_META, cols(p)] = sm
        mx_sc[c, :, cols(p)] = col_max(s, sm)

    def own_a_update(c, p):
        rows = own_a_rows(p)
        update(c, p, [sa_sc[c, 0:rows, cols(p)], sa_sc[c, t2:t2 + N_META, cols(p)]],
               [vt_ref[0, n_wide, :, 0:rows], vmt_ref[...]])

    def own_b_rows(p):
        return t if p == 2 else t2

    def own_b_refill(j, c, p):
        rows = own_b_rows(p)
        s = near_bias(qk(keys((n_wide + 1) * t2, rows), j, c, p), ("own", "both")[p - 2])
        sa_sc[c, 0:rows, cols(p)] = s
        mx_sc[c, :, cols(p)] = col_max(s)

    def own_b_update(c, p):
        rows = own_b_rows(p)
        update(c, p, [sa_sc[c, 0:rows, cols(p)]], [vt_ref[0, n_wide + 1, :, 0:rows]])

    m_sc[...] = jnp.full(m_sc.shape, NEG, f32)
    l_sc[...] = jnp.zeros(l_sc.shape, f32)
    acc_sc[...] = jnp.zeros(acc_sc.shape, f32)

    def far_pair(pp, carry):
        w = 2 * pp
        kt1, kt2 = keys((w + 1) * t2, t2), keys((w + 2) * t2, t2)
        far_tile(w, lambda c, p: far_refill(kt1, i, c, p))
        far_tile(w + 1, lambda c, p: far_refill(kt2, i, c, p))
        return carry

    def own_keys_and_finish():
        j_next = jnp.minimum(i + 1, n_q - 1)
        kt0 = keys(0, t2)
        for c, p in slots:
            own_a_update(c, p)
            if p < 2:
                far_refill(kt0, j_next, c, p)
            else:
                own_b_refill(i, c, p)
        for c, p in slots:
            if p >= 2:
                own_b_update(c, p)
                far_refill(kt0, j_next, c, p)
        o = acc_sc[0] / l_sc[0] - _diff_lambda(lam_ref) * (acc_sc[1] / l_sc[1])
        o = o * lax.rsqrt(jnp.mean(o * o, axis=0, keepdims=True) + EPS)
        og = (o.T * (sg_ref[...] * (1.0 - LAM_INIT)) * gate_ref[0]).astype(bf16)
        head = pl.program_id(1)
        rows = pl.ds(pl.multiple_of(i * (n_sub * t), n_sub * t), n_sub * t)

        @pl.when(head < H_ATT - 1)
        def _():
            og_sc[head, rows, :] = og

        @pl.when(head == H_ATT - 1)
        def _():
            heads = [og_sc[hd, rows, :] for hd in range(H_ATT - 1)] + [og]
            y_ref[0] = _merge(jnp.concatenate(heads, axis=1), yc_ref[0], x_ref[0], wo_ref,
                              fg_ref[...])

    @pl.when(i == 0)
    def _():
        for c, p in slots:
            own_a_refill(i, c, p, bm_ref[0] if p == 0 else None)
        own_keys_and_finish()

    @pl.when(i >= 1)
    def _():
        lax.fori_loop(0, i - 1, far_pair, 0)
        kt1 = keys((n_wide - 1) * t2, t2)
        far_tile(n_wide - 2,
                 lambda c, p: far_refill(kt1, i, c, p, "before" if p == 0 else None))
        far_tile(n_wide - 1, lambda c, p: own_a_refill(i, c, p))
        own_keys_and_finish()


def _prompt_attn(qa, qb, kb, vt, km, vmt, bias_near, bias_meta, gate, subln_g, lam_vecs,
                 x, yc, wo_bf, final_g):
    b, l, _ = qa.shape
    t = ATT_TILE
    tq = ATT_SUB * t
    assert ATT_SUB == 4 and l % tq == 0
    assert vt.shape == (b, l // (2 * t), W_ATT, 2 * t)
    tile_spec = pl.BlockSpec((1, tq, D_V), lambda bi, h, i: (bi, i, h))
    seq_spec = pl.BlockSpec((1, l, D_V), lambda bi, h, i: (bi, 0, h))
    vt_spec = pl.BlockSpec((1, l // (2 * t), D_V, 2 * t), lambda bi, h, i: (bi, 0, h, 0))
    per_head = lambda rows: pl.BlockSpec((1, rows, t), lambda bi, h, i: (h, 0, 0))
    d = x.shape[2]
    last_head = lambda width: pl.BlockSpec(
        (1, tq, width), lambda bi, h, i: (bi, jnp.where(h == H_ATT - 1, i, 0), 0))
    return pl.pallas_call(
        functools.partial(_prompt_attn_kernel, t=t, n_sub=ATT_SUB, n_q=l // tq),
        name="prompt_attn",
        grid=(b, H_ATT, l // tq),
        in_specs=[
            seq_spec, seq_spec, seq_spec, vt_spec,
            pl.BlockSpec((N_META, D_V), lambda bi, h, i: (0, h)),
            pl.BlockSpec((D_V, N_META), lambda bi, h, i: (h, 0)),
            per_head(2 * t), per_head(N_META),
            tile_spec,
            pl.BlockSpec((1, D_V), lambda bi, h, i: (0, 0)),
            pl.BlockSpec((4, D_QK), lambda bi, h, i: (0, 0)),
            last_head(d), last_head(yc.shape[2]),
            pl.BlockSpec(wo_bf.shape, lambda bi, h, i: (0, 0), pipeline_mode=pl.Buffered(1)),
            pl.BlockSpec((1, d), lambda bi, h, i: (0, 0)),
        ],
        out_specs=last_head(d),
        out_shape=jax.ShapeDtypeStruct((b, l, d), f32),
        scratch_shapes=[pltpu.VMEM((2, 2 * t + N_META, tq), f32),
                        pltpu.VMEM((2, 1, tq), f32),
                        pltpu.VMEM((2, 1, tq), f32), pltpu.VMEM((2, 1, tq), f32),
                        pltpu.VMEM((2, D_V, tq), f32),
                        pltpu.VMEM((H_ATT - 1, l, D_V), bf16)],
        compiler_params=pltpu.CompilerParams(
            dimension_semantics=("arbitrary", "arbitrary", "arbitrary"),
            vmem_limit_bytes=VMEM_LIMIT),
    )(qa, qb, kb, vt, km, vmt, bias_near, bias_meta, gate, subln_g, lam_vecs,
      x, yc, wo_bf, final_g.reshape(1, d))


def _merge(og, yc, x, wo_ref, fg):
    y = x + jnp.dot(og, wo_ref[0:W_ATT, :], preferred_element_type=f32)
    y = y + jnp.dot(yc, wo_ref[W_ATT:, :], preferred_element_type=f32)
    return _rms(y, fg)


def _out_proj_kernel(og_ref, yc_ref, x_ref, wo_ref, fg_ref, y_ref):
    y_ref[...] = _merge(og_ref[...], yc_ref[...], x_ref[...], wo_ref, fg_ref[...])


def _out_proj(og, yc, x, wo_bf, final_g, tm):
    n, d = x.shape
    w_att = og.shape[1]
    w_conv = yc.shape[1]
    row_spec = lambda width: pl.BlockSpec((tm, width), lambda i: (i, 0))
    return pl.pallas_call(
        _out_proj_kernel,
        name="out_proj",
        grid=(n // tm,),
        in_specs=[row_spec(w_att), row_spec(w_conv), row_spec(d),
                  pl.BlockSpec((w_att + w_conv, d), lambda i: (0, 0)),
                  pl.BlockSpec((1, d), lambda i: (0, 0))],
        out_specs=row_spec(d),
        out_shape=jax.ShapeDtypeStruct((n, d), f32),
        compiler_params=pltpu.CompilerParams(
            dimension_semantics=("arbitrary",), vmem_limit_bytes=VMEM_LIMIT),
    )(og, yc, x, wo_bf, final_g.reshape(1, d))


def _small_attn_kernel(*refs, has_cache):
    if has_cache:
        (qa_ref, qb_ref, kn_ref, vn_ref, bn_ref, ck_ref, cv_ref, bc_ref,
         gate_ref, yc_ref, x_ref, sg_ref, lam_ref, wo_ref, fg_ref, y_ref) = refs
    else:
        (qa_ref, qb_ref, kn_ref, vn_ref, bn_ref,
         gate_ref, yc_ref, x_ref, sg_ref, lam_ref, wo_ref, fg_ref, y_ref) = refs
    lq = qa_ref.shape[1]
    lam = _diff_lambda(lam_ref)
    nt = (((1,), (1,)), ((), ()))
    both = lambda a: jnp.concatenate([a, a], axis=0)
    heads = []
    for h in range(H_ATT):
        cols = slice(h * D_V, (h + 1) * D_V)
        q2 = jnp.concatenate([qa_ref[0][:, cols], qb_ref[0][:, cols]], axis=0)
        kn = kn_ref[0][:, cols]
        vn = vn_ref[0][:, cols]
        sn = lax.dot_general(q2, kn, nt, preferred_element_type=f32) + both(bn_ref[h])
        m = jnp.max(sn, axis=-1, keepdims=True)
        if has_cache:
            n_rows = ck_ref.shape[1] // H_ATT
            ck = ck_ref[0, pl.ds(h, n_rows, stride=H_ATT), :].astype(bf16)
            cv = cv_ref[0, pl.ds(h, n_rows, stride=H_ATT), :].astype(bf16)
            sc = lax.dot_general(q2, ck, nt, preferred_element_type=f32) + both(bc_ref[h])
            m = jnp.maximum(m, jnp.max(sc, axis=-1, keepdims=True))
        en = jnp.exp2(sn - m)
        l = jnp.sum(en, axis=-1, keepdims=True)
        acc = jnp.dot(en.astype(bf16), vn, preferred_element_type=f32)
        if has_cache:
            ec = jnp.exp2(sc - m)
            l = l + jnp.sum(ec, axis=-1, keepdims=True)
            acc = acc + jnp.dot(ec.astype(bf16), cv, preferred_element_type=f32)
        o = acc / l
        heads.append(_gate_heads(o[:lq], o[lq:], lam, sg_ref[...], gate_ref[0][:, cols]))
    og = jnp.concatenate(heads, axis=-1).astype(bf16)
    y_ref[0] = _merge(og, yc_ref[0], x_ref[0], wo_ref, fg_ref[...])


def _small_attn(qa, qb, kb, vb, bias_new, gate, yc, x, subln_g, lam_vecs, wo_bf, final_g,
                cache_k=None, cache_v=None, bias_cache=None):
    b, lq, d = x.shape
    has_cache = cache_k is not None
    per_b = lambda rows, width: pl.BlockSpec((1, rows, width), lambda i: (i, 0, 0))
    const = lambda shape: pl.BlockSpec(shape, lambda i: (0,) * len(shape))
    w = qa.shape[2]
    args = [qa, qb, kb, vb, bias_new]
    specs = [per_b(lq, w)] * 4 + [const(bias_new.shape)]
    if has_cache:
        args += [cache_k, cache_v, bias_cache]
        specs += [per_b(*cache_k.shape[1:]), per_b(*cache_v.shape[1:]), const(bias_cache.shape)]
    args += [gate, yc, x, subln_g, lam_vecs, wo_bf, final_g.reshape(1, d)]
    specs += [per_b(lq, w), per_b(lq, yc.shape[2]), per_b(lq, d), const(subln_g.shape),
              const(lam_vecs.shape), const(wo_bf.shape), const((1, d))]
    return pl.pallas_call(
        functools.partial(_small_attn_kernel, has_cache=has_cache),
        name="small_attn",
        grid=(b,),
        in_specs=specs,
        out_specs=per_b(lq, d),
        out_shape=jax.ShapeDtypeStruct((b, lq, d), f32),
        compiler_params=pltpu.CompilerParams(
            dimension_semantics=("arbitrary",), vmem_limit_bytes=VMEM_LIMIT),
    )(*args)


def kernel(x_prompt, x_sample, cache_k, cache_v, state_conv, meta_tokens, rel_bias, norm_g, w_in,
           conv_w, lambda_q1, lambda_k1, lambda_q2, lambda_k2, subln_g, w_out, final_g):
    bp, sp, d = x_prompt.shape
    bd, sd, _ = x_sample.shape
    depth = w_in.shape[0]
    assert depth == 1, "single-layer step only"
    past = cache_k.shape[2] - N_META
    t = ATT_TILE
    assert sp % (ATT_SUB * t) == 0 and t % CHUNK == 0

    w_bf = w_in[0].astype(bf16)
    wo_bf = w_out[0].astype(bf16)
    lam_vecs = jnp.stack([lambda_q1[0], lambda_k1[0], lambda_q2[0], lambda_k2[0]])
    sg = subln_g[0].reshape(1, D_V)
    wg = w_bf.shape[1] // N_SPLIT

    zeros_state = jnp.zeros((1, 1, CONV_W - 1, wg), f32)
    (qma, qmb, km, kmb, vm, vmb, gm, ycm, cm) = _in_proj(
        meta_tokens[None], zeros_state, norm_g[0], w_bf, conv_w[0], tm=N_META)
    (qpa, qpb, kp, kpb, vp, vpt, gp, ycp, cp) = _in_proj(
        x_prompt, jnp.broadcast_to(cm, (bp, 1, CONV_W - 1, wg)), norm_g[0], w_bf, conv_w[0],
        tm=PROJ_ROWS, row_offset=N_META, transpose_v=2 * t)
    kp, vp = _fill_meta_rows(km[0], vm[0], kp, vp)
    per_stream = lambda a: a.reshape(bd, sd, a.shape[-1])
    (qda, qdb, kd, kdb, vd, vdb, gd, ycd, cd) = _in_proj(
        x_sample.reshape(1, bd * sd, d), state_conv[0][None], norm_g[0], w_bf, conv_w[0],
        tm=bd * sd)
    qda, qdb, kdb, vdb, gd, ycd = map(per_stream, (qda, qdb, kdb, vdb, gd, ycd))

    meta_pos = np.arange(-N_META, 0)
    tile_pos = np.arange(t)
    d_qpos = past + np.arange(sd)
    cache_pos = np.concatenate([meta_pos, np.arange(past)])
    (bias_near, bias_meta, bias_meta_self, bias_dec_new, bias_dec_cache) = (
        _build_biases([
            (tile_pos + t, np.arange(0, 2 * t), True),
            (tile_pos, meta_pos, True),
            (meta_pos, meta_pos, False),
            (d_qpos, d_qpos, False),
            (d_qpos, cache_pos, False),
        ], rel_bias))
    for off in range(2, sp // t):
        _assert_far_past(tile_pos + off * t, tile_pos)
    for off in range(1, sp // t):
        _assert_far_past(tile_pos + off * t, meta_pos)

    y_prompt = _prompt_attn(qpa, qpb, kpb, vpt, kmb[0], vmb[0].T, bias_near, bias_meta, gp, sg,
                            lam_vecs, x_prompt, ycp, wo_bf, final_g)

    _small_attn(qma, qmb, kmb, vmb, bias_meta_self, gm, ycm, meta_tokens[None], sg, lam_vecs,
                wo_bf, final_g)

    y_sample = _small_attn(
        qda, qdb, kdb, vdb, bias_dec_new, gd, ycd, x_sample, sg, lam_vecs, wo_bf, final_g,
        cache_k=cache_k[0].reshape(bd, (N_META + past) * H_ATT, D_V),
        cache_v=cache_v[0].reshape(bd, (N_META + past) * H_ATT, D_V),
        bias_cache=bias_dec_cache)

    return (y_prompt, y_sample,
            kp.reshape(1, bp, N_META + sp, H_ATT, D_V),
            vp.reshape(1, bp, N_META + sp, H_ATT, D_V),
            cp.reshape(1, bp, CONV_W - 1, wg),
            kd.reshape(1, bd, sd, H_ATT, D_V),
            vd.reshape(1, bd, sd, H_ATT, D_V),
            cd)
```

```python
import functools
import math

import numpy as np
import jax
import jax.numpy as jnp
from jax import lax
from jax.experimental import pallas as pl
from jax.experimental.pallas import tpu as pltpu

f32 = jnp.float32
bf16 = jnp.bfloat16

CHUNK = 64
N_META = 16
H_ATT = 4
D_QK = 64
D_V = 2 * D_QK
W_ATT = H_ATT * D_V
CONV_W = 3
N_SPLIT = 8
N_BUCKETS = 32
MAX_DIST = 128
EPS = 1e-6
NEG = -1e30
SCALE = D_QK ** -0.5
LOG2E = math.log2(math.e)
LAM_INIT = 0.8 - 0.6 * math.exp(-0.3 * 0)
FAR_BUCKET = N_BUCKETS // 2 - 1

ATT_TILE = 256
ATT_SUB = 4
VMEM_LIMIT = 48 * 1024 * 1024
PROJ_ROWS = 1024


def _silu(z):
    return z * (1.0 / (1.0 + jnp.exp(-z)))


def _rms(x, g):
    return x * lax.rsqrt(jnp.mean(x * x, axis=-1, keepdims=True) + EPS) * g


def _in_proj_kernel(x_ref, cinit_ref, g_ref, w_ref, cw_ref,
                    qa_ref, qb_ref, k_ref, kb_ref, v_ref, vb_ref, gate_ref, yc_ref, clast_ref,
                    carry_ref, *, tm, wg, n_seg, transpose_v):
    j = pl.program_id(1)

    @pl.when(j == 0)
    def _():
        carry_ref[...] = cinit_ref[0]

    x = x_ref[0]
    h = _rms(x, g_ref[...]).astype(bf16)

    def proj(g):
        return jnp.dot(h, w_ref[:, g * wg:(g + 1) * wg], preferred_element_type=f32)

    def store_heads(ref, a):
        for hd in range(H_ATT):
            ref[pl.ds(hd, tm, stride=H_ATT), :] = a[:, hd * D_V:(hd + 1) * D_V]

    q = (proj(0) * (SCALE * LOG2E)).astype(bf16)
    first_half = lax.broadcasted_iota(jnp.int32, q.shape, 1) % D_V < D_QK
    zero = jnp.zeros_like(q)
    qa_ref[0] = jnp.where(first_half, q, zero)
    qb_ref[0] = jnp.where(first_half, zero, q)
    k = proj(1)
    store_heads(k_ref, k)
    kb_ref[0] = k.astype(bf16)
    v = proj(2)
    store_heads(v_ref, v)
    if transpose_v:
        for ch in range(tm // transpose_v):
            vb_ref[ch] = v[ch * transpose_v:(ch + 1) * transpose_v].T.astype(bf16)
    else:
        vb_ref[...] = v.astype(bf16)
    gate_ref[0] = _silu(proj(3))

    cu = proj(5) * proj(6)
    prev = carry_ref[...]
    seg = tm // n_seg
    row = lax.broadcasted_iota(jnp.int32, cu.shape, 0) % seg

    def before(k):
        if n_seg == 1:
            return prev[0, k:k + 1]
        return jnp.concatenate(
            [jnp.broadcast_to(prev[sg, k:k + 1], (seg, wg)) for sg in range(n_seg)], axis=0)

    cu_m1 = jnp.where(row == 0, before(1), pltpu.roll(cu, 1, axis=0))
    cu_m2 = jnp.where(row == 0, before(0),
                      jnp.where(row == 1, before(1), pltpu.roll(cu, 2, axis=0)))
    cw = cw_ref[...]
    conv = cw[0:1] * cu_m2 + cw[1:2] * cu_m1 + cw[2:3] * cu
    yc_ref[0] = (_silu(proj(7)) * proj(4) * conv).astype(bf16)

    for sg in range(n_seg):
        last = cu[(sg + 1) * seg - 2:(sg + 1) * seg]
        carry_ref[sg] = last
        clast_ref[0, sg] = last


def _in_proj(x, conv_init, norm_g, w_bf, conv_w, tm, row_offset=0, transpose_v=0):
    b, l, d = x.shape
    w_all = w_bf.shape[1]
    wg = w_all // N_SPLIT
    assert wg == W_ATT
    nj = l // tm
    n_seg = conv_init.shape[1]
    assert n_seg == 1 or nj == 1
    row_spec = lambda width: pl.BlockSpec((1, tm, width), lambda i, j: (i, j, 0))
    const = lambda shape: pl.BlockSpec(shape, lambda i, j: (0,) * len(shape))
    state_spec = pl.BlockSpec((1, n_seg, CONV_W - 1, wg), lambda i, j: (i, 0, 0, 0))
    if row_offset:
        cache_spec = pl.BlockSpec(
            (None, pl.Element(tm * H_ATT), pl.Element(D_V)),
            lambda i, j: (i, pl.multiple_of((row_offset + j * tm) * H_ATT, 8 * H_ATT), 0))
    else:
        cache_spec = pl.BlockSpec((None, tm * H_ATT, D_V), lambda i, j: (i, j, 0))
    if transpose_v:
        assert tm % transpose_v == 0
        vb_spec = pl.BlockSpec((None, tm // transpose_v, wg, transpose_v),
                               lambda i, j: (i, j, 0, 0))
        vb_shape = jax.ShapeDtypeStruct((b, l // transpose_v, wg, transpose_v), bf16)
    else:
        vb_spec = pl.BlockSpec((None, tm, wg), lambda i, j: (i, j, 0))
        vb_shape = jax.ShapeDtypeStruct((b, l, wg), bf16)
    cache_shape = jax.ShapeDtypeStruct((b, (row_offset + l) * H_ATT, D_V), f32)
    out_shape = (
        jax.ShapeDtypeStruct((b, l, wg), bf16),
        jax.ShapeDtypeStruct((b, l, wg), bf16),
        cache_shape,
        jax.ShapeDtypeStruct((b, l, wg), bf16),
        cache_shape,
        vb_shape,
        jax.ShapeDtypeStruct((b, l, wg), f32),
        jax.ShapeDtypeStruct((b, l, wg), bf16),
        jax.ShapeDtypeStruct((b, n_seg, CONV_W - 1, wg), f32),
    )
    return pl.pallas_call(
        functools.partial(_in_proj_kernel, tm=tm, wg=wg, n_seg=n_seg, transpose_v=transpose_v),
        name="in_proj",
        grid=(b, nj),
        in_specs=[row_spec(d), state_spec, const((1, d)),
                  pl.BlockSpec((d, w_all), lambda i, j: (0, 0), pipeline_mode=pl.Buffered(1)),
                  const((CONV_W, wg))],
        out_specs=(row_spec(wg), row_spec(wg), cache_spec, row_spec(wg), cache_spec, vb_spec,
                   row_spec(wg), row_spec(wg), state_spec),
        out_shape=out_shape,
        scratch_shapes=[pltpu.VMEM((n_seg, CONV_W - 1, wg), f32)],
        compiler_params=pltpu.CompilerParams(
            dimension_semantics=("arbitrary", "arbitrary"), vmem_limit_bytes=VMEM_LIMIT),
    )(x, conv_init, norm_g.reshape(1, d), w_bf, conv_w)


def _fill_rows_kernel(km_ref, vm_ref, k_hbm, v_hbm, ko_ref, vo_ref):
    del k_hbm, v_hbm
    ko_ref[0] = km_ref[...]
    vo_ref[0] = vm_ref[...]


def _fill_meta_rows(k_meta, v_meta, k_all, v_all):
    b = k_all.shape[0]
    rows = k_meta.shape[0]
    small = pl.BlockSpec((rows, D_V), lambda i: (0, 0))
    lead = pl.BlockSpec((1, rows, D_V), lambda i: (i, 0, 0))
    whole = pl.BlockSpec(memory_space=pl.ANY)
    return pl.pallas_call(
        _fill_rows_kernel,
        name="fill_meta_rows",
        grid=(b,),
        in_specs=[small, small, whole, whole],
        out_specs=(lead, lead),
        out_shape=(jax.ShapeDtypeStruct(k_all.shape, k_all.dtype),
                   jax.ShapeDtypeStruct(v_all.shape, v_all.dtype)),
        input_output_aliases={2: 0, 3: 1},
    )(k_meta, v_meta, k_all, v_all)


def _rel_bucket(rel):
    nb = N_BUCKETS // 2
    ret = np.where(rel > 0, nb, 0)
    n = np.abs(rel)
    max_exact = nb // 2
    nf = np.maximum(n, 1).astype(np.float32)
    large = max_exact + (np.log(nf / np.float32(max_exact)) / np.float32(math.log(MAX_DIST / max_exact))
                         * np.float32(nb - max_exact)).astype(np.int32)
    large = np.minimum(large, nb - 1)
    return (ret + np.where(n < max_exact, n, large)).astype(np.int32)


def _chunk_id_np(pos):
    return np.where(pos < 0, -1, pos // CHUNK)


def _assert_far_past(q_pos, k_pos):
    rel = k_pos[None, :] - q_pos[:, None]
    nb = N_BUCKETS // 2
    max_exact = nb // 2
    n = np.abs(rel).astype(np.float64)
    large = max_exact + np.log(n / max_exact) / math.log(MAX_DIST / max_exact) * (nb - max_exact)
    assert np.all(rel < 0) and np.all(large >= nb), "tile is not in the saturated bucket"
    assert np.all(_chunk_id_np(k_pos)[None, :] <= _chunk_id_np(q_pos)[:, None])


def _bias_kernel(rb_ref, *refs):
    n = len(refs) // 3
    h = pl.program_id(0)
    far = rb_ref[FAR_BUCKET, h]
    shifted = [rb_ref[b, h] - far for b in range(N_BUCKETS)]
    for j in range(n):
        bucket = refs[2 * j][...]
        acc = jnp.zeros(bucket.shape, f32)
        for b in range(N_BUCKETS):
            acc = jnp.where(bucket == b, shifted[b], acc)
        refs[2 * n + j][0] = jnp.where(refs[2 * j + 1][...] != 0, acc * LOG2E, NEG)


def _build_biases(tiles, rel_bias):
    args, specs, out_specs, out_shapes = [], [], [], []
    for q_pos, k_pos, keys_major in tiles:
        rel = k_pos[None, :] - q_pos[:, None]
        vis = _chunk_id_np(k_pos)[None, :] <= _chunk_id_np(q_pos)[:, None]
        if keys_major:
            rel, vis = rel.T, vis.T
        r, c = rel.shape
        args += [jnp.asarray(_rel_bucket(rel)), jnp.asarray(vis.astype(np.int32))]
        specs += [pl.BlockSpec((r, c), lambda h: (0, 0))] * 2
        out_specs.append(pl.BlockSpec((1, r, c), lambda h: (h, 0, 0)))
        out_shapes.append(jax.ShapeDtypeStruct((H_ATT, r, c), f32))
    return pl.pallas_call(
        _bias_kernel,
        name="bias_tiles",
        grid=(H_ATT,),
        in_specs=[pl.BlockSpec(memory_space=pltpu.SMEM)] + specs,
        out_specs=tuple(out_specs),
        out_shape=tuple(out_shapes),
    )(rel_bias, *args)


def _diff_lambda(lam_ref):
    lv = lam_ref[...]
    s1 = jnp.sum(lv[0:1] * lv[1:2], axis=-1, keepdims=True)
    s2 = jnp.sum(lv[2:3] * lv[3:4], axis=-1, keepdims=True)
    return jnp.exp(s1) - jnp.exp(s2) + LAM_INIT


def _gate_heads(o1, o2, lam, sg, gate):
    o = o1 - lam * o2
    return _rms(o, sg) * (1.0 - LAM_INIT) * gate


def _prompt_attn_kernel(qa_ref, qb_ref, k_ref, vt_ref, km_ref, vmt_ref, bn_ref, bm_ref, gate_ref,
                        sg_ref, lam_ref, x_ref, yc_ref, wo_ref, fg_ref, y_ref,
                        sa_sc, mx_sc, m_sc, l_sc, acc_sc, og_sc, *, t, n_sub, n_q):
    i = pl.program_id(2)
    t2 = 2 * t
    n_wide = i * (n_sub // 2)
    slots = [(c, p) for c in range(2) for p in range(n_sub)]

    def keys(start, rows):
        return k_ref[0, pl.ds(pl.multiple_of(start, t), rows), :]

    def cols(p):
        return slice(p * t, (p + 1) * t)

    def qk(kt, j, c, p):
        rows = pl.ds(pl.multiple_of((j * n_sub + p) * t, t), t)
        return lax.dot_general(kt, (qa_ref, qb_ref)[c][0, rows, :], (((1,), (1,)), ((), ())),
                               preferred_element_type=f32)

    def col_max(*parts):
        m = jnp.max(parts[0], axis=0, keepdims=True)
        for part in parts[1:]:
            m = jnp.maximum(m, jnp.max(part, axis=0, keepdims=True))
        return m

    def update(c, p, s_parts, vt_parts):
        m_prev = m_sc[c, :, cols(p)]
        m_new = jnp.maximum(m_prev, mx_sc[c, :, cols(p)])
        alpha = jnp.exp2(m_prev - m_new)
        e_sum = pv = None
        for s, vt in zip(s_parts, vt_parts):
            e = jnp.exp2(s - m_new)
            e_k = jnp.sum(e, axis=0, keepdims=True)
            pv_k = jnp.dot(vt, e.astype(bf16), preferred_element_type=f32)
            e_sum = e_k if e_sum is None else e_sum + e_k
            pv = pv_k if pv is None else pv + pv_k
        l_sc[c, :, cols(p)] = alpha * l_sc[c, :, cols(p)] + e_sum
        acc_sc[c, :, cols(p)] = alpha * acc_sc[c, :, cols(p)] + pv
        m_sc[c, :, cols(p)] = m_new

    def near_bias(s, kind):
        if kind == "both":
            return s + bn_ref[0]
        if kind == "own":
            return s + bn_ref[0, t:t2]
        return jnp.concatenate([s[:t], s[t:] + bn_ref[0, 0:t]], axis=0)

    def far_refill(kt, j, c, p, bias=None):
        s = qk(kt, j, c, p)
        if bias is not None:
            s = near_bias(s, bias)
        sa_sc[c, 0:t2, cols(p)] = s
        mx_sc[c, :, cols(p)] = col_max(s)

    def far_update(c, p, vt):
        update(c, p, [sa_sc[c, 0:t2, cols(p)]], [vt])

    def far_tile(w, refill):
        vt = vt_ref[0, w]
        for c, p in slots:
            far_update(c, p, vt)
            refill(c, p)

    def own_a_rows(p):
        return t if p == 0 else t2

    def own_a_refill(j, c, p, meta_bias=None):
        rows = own_a_rows(p)
        s = qk(keys(n_wide * t2, rows), j, c, p)
        if p < 3:
            s = near_bias(s, ("own", "both", "before")[p])
        sm = qk(km_ref[...], j, c, p)
        if meta_bias is not None:
            sm = sm + meta_bias
        sa_sc[c, 0:rows, cols(p)] = s
        sa_sc[c, t2:t2 + N_META, cols(p)] = sm
        mx_sc[c, :, cols(p)] = col_max(s, sm)

    def own_a_update(c, p):
        rows = own_a_rows(p)
        update(c, p, [sa_sc[c, 0:rows, cols(p)], sa_sc[c, t2:t2 + N_META, cols(p)]],
               [vt_ref[0, n_wide, :, 0:rows], vmt_ref[...]])

    def own_b_rows(p):
        return t if p == 2 else t2

    def own_b_refill(j, c, p):
        rows = own_b_rows(p)
        s = near_bias(qk(keys((n_wide + 1) * t2, rows), j, c, p), ("own", "both")[p - 2])
        sa_sc[c, 0:rows, cols(p)] = s
        mx_sc[c, :, cols(p)] = col_max(s)

    def own_b_update(c, p):
        rows = own_b_rows(p)
        update(c, p, [sa_sc[c, 0:rows, cols(p)]], [vt_ref[0, n_wide + 1, :, 0:rows]])

    m_sc[...] = jnp.full(m_sc.shape, NEG, f32)
    l_sc[...] = jnp.zeros(l_sc.shape, f32)
    acc_sc[...] = jnp.zeros(acc_sc.shape, f32)

    def far_pair(pp, carry):
        w = 2 * pp
        kt1, kt2 = keys((w + 1) * t2, t2), keys((w + 2) * t2, t2)
        far_tile(w, lambda c, p: far_refill(kt1, i, c, p))
        far_tile(w + 1, lambda c, p: far_refill(kt2, i, c, p))
        return carry

    def own_keys_and_finish():
        j_next = jnp.minimum(i + 1, n_q - 1)
        kt0 = keys(0, t2)
        for c, p in slots:
            own_a_update(c, p)
            if p < 2:
                far_refill(kt0, j_next, c, p)
            else:
                own_b_refill(i, c, p)
        for c, p in slots:
            if p >= 2:
                own_b_update(c, p)
                far_refill(kt0, j_next, c, p)
        o = acc_sc[0] / l_sc[0] - _diff_lambda(lam_ref) * (acc_sc[1] / l_sc[1])
        o = o * lax.rsqrt(jnp.mean(o * o, axis=0, keepdims=True) + EPS)
        og = (o.T * (sg_ref[...] * (1.0 - LAM_INIT)) * gate_ref[0]).astype(bf16)
        head = pl.program_id(1)
        rows = pl.ds(pl.multiple_of(i * (n_sub * t), n_sub * t), n_sub * t)

        @pl.when(head < H_ATT - 1)
        def _():
            og_sc[head, rows, :] = og

        @pl.when(head == H_ATT - 1)
        def _():
            for p in range(n_sub):
                sub = pl.ds(pl.multiple_of((i * n_sub + p) * t, t), t)
                heads = [og_sc[hd, sub, :] for hd in range(H_ATT - 1)] + [og[cols(p)]]
                y_ref[0, cols(p), :] = _merge(jnp.concatenate(heads, axis=1), yc_ref[0, cols(p), :],
                                              x_ref[0, cols(p), :], wo_ref, fg_ref[...])

    @pl.when(i == 0)
    def _():
        for c, p in slots:
            own_a_refill(i, c, p, bm_ref[0] if p == 0 else None)
        own_keys_and_finish()

    @pl.when(i >= 1)
    def _():
        lax.fori_loop(0, i - 1, far_pair, 0)
        kt1 = keys((n_wide - 1) * t2, t2)
        far_tile(n_wide - 2,
                 lambda c, p: far_refill(kt1, i, c, p, "before" if p == 0 else None))
        far_tile(n_wide - 1, lambda c, p: own_a_refill(i, c, p))
        own_keys_and_finish()


def _prompt_attn(qa, qb, kb, vt, km, vmt, bias_near, bias_meta, gate, subln_g, lam_vecs,
                 x, yc, wo_bf, final_g):
    b, l, _ = qa.shape
    t = ATT_TILE
    tq = ATT_SUB * t
    assert ATT_SUB == 4 and l % tq == 0
    assert vt.shape == (b, l // (2 * t), W_ATT, 2 * t)
    tile_spec = pl.BlockSpec((1, tq, D_V), lambda bi, h, i: (bi, i, h))
    seq_spec = pl.BlockSpec((1, l, D_V), lambda bi, h, i: (bi, 0, h))
    vt_spec = pl.BlockSpec((1, l // (2 * t), D_V, 2 * t), lambda bi, h, i: (bi, 0, h, 0))
    per_head = lambda rows: pl.BlockSpec((1, rows, t), lambda bi, h, i: (h, 0, 0))
    d = x.shape[2]
    last_head = lambda width: pl.BlockSpec(
        (1, tq, width), lambda bi, h, i: (bi, jnp.where(h == H_ATT - 1, i, 0), 0))
    return pl.pallas_call(
        functools.partial(_prompt_attn_kernel, t=t, n_sub=ATT_SUB, n_q=l // tq),
        name="prompt_attn",
        grid=(b, H_ATT, l // tq),
        in_specs=[
            seq_spec, seq_spec, seq_spec, vt_spec,
            pl.BlockSpec((N_META, D_V), lambda bi, h, i: (0, h)),
            pl.BlockSpec((D_V, N_META), lambda bi, h, i: (h, 0)),
            per_head(2 * t), per_head(N_META),
            tile_spec,
            pl.BlockSpec((1, D_V), lambda bi, h, i: (0, 0)),
            pl.BlockSpec((4, D_QK), lambda bi, h, i: (0, 0)),
            last_head(d), last_head(yc.shape[2]),
            pl.BlockSpec(wo_bf.shape, lambda bi, h, i: (0, 0), pipeline_mode=pl.Buffered(1)),
            pl.BlockSpec((1, d), lambda bi, h, i: (0, 0)),
        ],
        out_specs=last_head(d),
        out_shape=jax.ShapeDtypeStruct((b, l, d), f32),
        scratch_shapes=[pltpu.VMEM((2, 2 * t + N_META, tq), f32),
                        pltpu.VMEM((2, 1, tq), f32),
                        pltpu.VMEM((2, 1, tq), f32), pltpu.VMEM((2, 1, tq), f32),
                        pltpu.VMEM((2, D_V, tq), f32),
                        pltpu.VMEM((H_ATT - 1, l, D_V), bf16)],
        compiler_params=pltpu.CompilerParams(
            dimension_semantics=("arbitrary", "arbitrary", "arbitrary"),
            vmem_limit_bytes=VMEM_LIMIT),
    )(qa, qb, kb, vt, km, vmt, bias_near, bias_meta, gate, subln_g, lam_vecs,
      x, yc, wo_bf, final_g.reshape(1, d))


def _merge(og, yc, x, wo_ref, fg):
    y = x + jnp.dot(jnp.concatenate([og, yc], axis=1), wo_ref[...], preferred_element_type=f32)
    return _rms(y, fg)


def _small_attn_kernel(*refs, has_cache):
    if has_cache:
        (qa_ref, qb_ref, kn_ref, vn_ref, bn_ref, ck_ref, cv_ref, bc_ref,
         gate_ref, yc_ref, x_ref, sg_ref, lam_ref, wo_ref, fg_ref, y_ref) = refs
    else:
        (qa_ref, qb_ref, kn_ref, vn_ref, bn_ref,
         gate_ref, yc_ref, x_ref, sg_ref, lam_ref, wo_ref, fg_ref, y_ref) = refs
    lq = qa_ref.shape[1]
    lam = _diff_lambda(lam_ref)
    nt = (((1,), (1,)), ((), ()))
    both = lambda a: jnp.concatenate([a, a], axis=0)
    heads = []
    for h in range(H_ATT):
        cols = slice(h * D_V, (h + 1) * D_V)
        q2 = jnp.concatenate([qa_ref[0][:, cols], qb_ref[0][:, cols]], axis=0)
        kn = kn_ref[0][:, cols]
        vn = vn_ref[0][:, cols]
        sn = lax.dot_general(q2, kn, nt, preferred_element_type=f32) + both(bn_ref[h])
        m = jnp.max(sn, axis=-1, keepdims=True)
        if has_cache:
            n_rows = ck_ref.shape[1] // H_ATT
            ck = ck_ref[0, pl.ds(h, n_rows, stride=H_ATT), :].astype(bf16)
            cv = cv_ref[0, pl.ds(h, n_rows, stride=H_ATT), :].astype(bf16)
            sc = lax.dot_general(q2, ck, nt, preferred_element_type=f32) + both(bc_ref[h])
            m = jnp.maximum(m, jnp.max(sc, axis=-1, keepdims=True))
        en = jnp.exp2(sn - m)
        l = jnp.sum(en, axis=-1, keepdims=True)
        acc = jnp.dot(en.astype(bf16), vn, preferred_element_type=f32)
        if has_cache:
            ec = jnp.exp2(sc - m)
            l = l + jnp.sum(ec, axis=-1, keepdims=True)
            acc = acc + jnp.dot(ec.astype(bf16), cv, preferred_element_type=f32)
        o = acc / l
        heads.append(_gate_heads(o[:lq], o[lq:], lam, sg_ref[...], gate_ref[0][:, cols]))
    og = jnp.concatenate(heads, axis=-1).astype(bf16)
    y_ref[0] = _merge(og, yc_ref[0], x_ref[0], wo_ref, fg_ref[...])


def _small_attn(qa, qb, kb, vb, bias_new, gate, yc, x, subln_g, lam_vecs, wo_bf, final_g,
                cache_k=None, cache_v=None, bias_cache=None):
    b, lq, d = x.shape
    has_cache = cache_k is not None
    per_b = lambda rows, width: pl.BlockSpec((1, rows, width), lambda i: (i, 0, 0))
    const = lambda shape: pl.BlockSpec(shape, lambda i: (0,) * len(shape))
    w = qa.shape[2]
    args = [qa, qb, kb, vb, bias_new]
    specs = [per_b(lq, w)] * 4 + [const(bias_new.shape)]
    if has_cache:
        args += [cache_k, cache_v, bias_cache]
        specs += [per_b(*cache_k.shape[1:]), per_b(*cache_v.shape[1:]), const(bias_cache.shape)]
    args += [gate, yc, x, subln_g, lam_vecs, wo_bf, final_g.reshape(1, d)]
    specs += [per_b(lq, w), per_b(lq, yc.shape[2]), per_b(lq, d), const(subln_g.shape),
              const(lam_vecs.shape), const(wo_bf.shape), const((1, d))]
    return pl.pallas_call(
        functools.partial(_small_attn_kernel, has_cache=has_cache),
        name="small_attn",
        grid=(b,),
        in_specs=specs,
        out_specs=per_b(lq, d),
        out_shape=jax.ShapeDtypeStruct((b, lq, d), f32),
        compiler_params=pltpu.CompilerParams(
            dimension_semantics=("arbitrary",), vmem_limit_bytes=VMEM_LIMIT),
    )(*args)


def kernel(x_prompt, x_sample, cache_k, cache_v, state_conv, meta_tokens, rel_bias, norm_g, w_in,
           conv_w, lambda_q1, lambda_k1, lambda_q2, lambda_k2, subln_g, w_out, final_g):
    bp, sp, d = x_prompt.shape
    bd, sd, _ = x_sample.shape
    depth = w_in.shape[0]
    assert depth == 1, "single-layer step only"
    past = cache_k.shape[2] - N_META
    t = ATT_TILE
    assert sp % (ATT_SUB * t) == 0 and t % CHUNK == 0

    w_bf = w_in[0].astype(bf16)
    wo_bf = w_out[0].astype(bf16)
    lam_vecs = jnp.stack([lambda_q1[0], lambda_k1[0], lambda_q2[0], lambda_k2[0]])
    sg = subln_g[0].reshape(1, D_V)
    wg = w_bf.shape[1] // N_SPLIT

    zeros_state = jnp.zeros((1, 1, CONV_W - 1, wg), f32)
    (qma, qmb, km, kmb, vm, vmb, gm, ycm, cm) = _in_proj(
        meta_tokens[None], zeros_state, norm_g[0], w_bf, conv_w[0], tm=N_META)
    (qpa, qpb, kp, kpb, vp, vpt, gp, ycp, cp) = _in_proj(
        x_prompt, jnp.broadcast_to(cm, (bp, 1, CONV_W - 1, wg)), norm_g[0], w_bf, conv_w[0],
        tm=PROJ_ROWS, row_offset=N_META, transpose_v=2 * t)
    kp, vp = _fill_meta_rows(km[0], vm[0], kp, vp)
    per_stream = lambda a: a.reshape(bd, sd, a.shape[-1])
    (qda, qdb, kd, kdb, vd, vdb, gd, ycd, cd) = _in_proj(
        x_sample.reshape(1, bd * sd, d), state_conv[0][None], norm_g[0], w_bf, conv_w[0],
        tm=bd * sd)
    qda, qdb, kdb, vdb, gd, ycd = map(per_stream, (qda, qdb, kdb, vdb, gd, ycd))

    meta_pos = np.arange(-N_META, 0)
    tile_pos = np.arange(t)
    d_qpos = past + np.arange(sd)
    cache_pos = np.concatenate([meta_pos, np.arange(past)])
    (bias_near, bias_meta, bias_meta_self, bias_dec_new, bias_dec_cache) = (
        _build_biases([
            (tile_pos + t, np.arange(0, 2 * t), True),
            (tile_pos, meta_pos, True),
            (meta_pos, meta_pos, False),
            (d_qpos, d_qpos, False),
            (d_qpos, cache_pos, False),
        ], rel_bias))
    for off in range(2, sp // t):
        _assert_far_past(tile_pos + off * t, tile_pos)
    for off in range(1, sp // t):
        _assert_far_past(tile_pos + off * t, meta_pos)

    y_prompt = _prompt_attn(qpa, qpb, kpb, vpt, kmb[0], vmb[0].T, bias_near, bias_meta, gp, sg,
                            lam_vecs, x_prompt, ycp, wo_bf, final_g)

    _small_attn(qma, qmb, kmb, vmb, bias_meta_self, gm, ycm, meta_tokens[None], sg, lam_vecs,
                wo_bf, final_g)

    y_sample = _small_attn(
        qda, qdb, kdb, vdb, bias_dec_new, gd, ycd, x_sample, sg, lam_vecs, wo_bf, final_g,
        cache_k=cache_k[0].reshape(bd, (N_META + past) * H_ATT, D_V),
        cache_v=cache_v[0].reshape(bd, (N_META + past) * H_ATT, D_V),
        bias_cache=bias_dec_cache)

    return (y_prompt, y_sample,
            kp.reshape(1, bp, N_META + sp, H_ATT, D_V),
            vp.reshape(1, bp, N_META + sp, H_ATT, D_V),
            cp.reshape(1, bp, CONV_W - 1, wg),
            kd.reshape(1, bd, sd, H_ATT, D_V),
            vd.reshape(1, bd, sd, H_ATT, D_V),
            cd)
```

```python
import functools
import math

import numpy as np
import jax
import jax.numpy as jnp
from jax import lax
from jax.experimental import pallas as pl
from jax.experimental.pallas import tpu as pltpu

f32 = jnp.float32
bf16 = jnp.bfloat16

CHUNK = 64
N_META = 16
H_ATT = 4
D_QK = 64
D_V = 2 * D_QK
W_ATT = H_ATT * D_V
CONV_W = 3
N_SPLIT = 8
N_BUCKETS = 32
MAX_DIST = 128
EPS = 1e-6
NEG = -1e30
SCALE = D_QK ** -0.5
LOG2E = math.log2(math.e)
LAM_INIT = 0.8 - 0.6 * math.exp(-0.3 * 0)
FAR_BUCKET = N_BUCKETS // 2 - 1

ATT_TILE = 256
ATT_SUB = 4
VMEM_LIMIT = 56 * 1024 * 1024
PROJ_ROWS = 1024


def _silu(z):
    return z * (1.0 / (1.0 + jnp.exp(-z)))


def _rms(x, g):
    return x * lax.rsqrt(jnp.mean(x * x, axis=-1, keepdims=True) + EPS) * g


def _in_proj_kernel(x_ref, cinit_ref, g_ref, w_ref, cw_ref,
                    qa_ref, qb_ref, k_ref, kb_ref, v_ref, vb_ref, gate_ref, yc_ref, clast_ref,
                    carry_ref, *, tm, wg, n_seg, transpose_v):
    j = pl.program_id(1)

    @pl.when(j == 0)
    def _():
        carry_ref[...] = cinit_ref[0]

    x = x_ref[0]
    h = _rms(x, g_ref[...]).astype(bf16)

    def proj(g):
        return jnp.dot(h, w_ref[:, g * wg:(g + 1) * wg].astype(bf16), preferred_element_type=f32)

    def store_heads(ref, a):
        for hd in range(H_ATT):
            ref[pl.ds(hd, tm, stride=H_ATT), :] = a[:, hd * D_V:(hd + 1) * D_V]

    q = (proj(0) * (SCALE * LOG2E)).astype(bf16)
    first_half = lax.broadcasted_iota(jnp.int32, q.shape, 1) % D_V < D_QK
    zero = jnp.zeros_like(q)
    qa_ref[0] = jnp.where(first_half, q, zero)
    qb_ref[0] = jnp.where(first_half, zero, q)
    k = proj(1)
    store_heads(k_ref, k)
    kb_ref[0] = k.astype(bf16)
    v = proj(2)
    store_heads(v_ref, v)
    if transpose_v:
        for ch in range(tm // transpose_v):
            vb_ref[ch] = v[ch * transpose_v:(ch + 1) * transpose_v].T.astype(bf16)
    else:
        vb_ref[...] = v.astype(bf16)
    gate_ref[0] = _silu(proj(3))

    cu = proj(5) * proj(6)
    prev = carry_ref[...]
    seg = tm // n_seg
    row = lax.broadcasted_iota(jnp.int32, cu.shape, 0) % seg

    def before(k):
        if n_seg == 1:
            return prev[0, k:k + 1]
        return jnp.concatenate(
            [jnp.broadcast_to(prev[sg, k:k + 1], (seg, wg)) for sg in range(n_seg)], axis=0)

    cu_m1 = jnp.where(row == 0, before(1), pltpu.roll(cu, 1, axis=0))
    cu_m2 = jnp.where(row == 0, before(0),
                      jnp.where(row == 1, before(1), pltpu.roll(cu, 2, axis=0)))
    cw = cw_ref[...]
    conv = cw[0:1] * cu_m2 + cw[1:2] * cu_m1 + cw[2:3] * cu
    yc_ref[0] = (_silu(proj(7)) * proj(4) * conv).astype(bf16)

    for sg in range(n_seg):
        last = cu[(sg + 1) * seg - 2:(sg + 1) * seg]
        carry_ref[sg] = last
        clast_ref[0, sg] = last


def _in_proj(x, conv_init, norm_g, w_proj, conv_w, tm, row_offset=0, transpose_v=0):
    b, l, d = x.shape
    w_all = w_proj.shape[1]
    wg = w_all // N_SPLIT
    assert wg == W_ATT
    nj = l // tm
    n_seg = conv_init.shape[1]
    assert n_seg == 1 or nj == 1
    row_spec = lambda width: pl.BlockSpec((1, tm, width), lambda i, j: (i, j, 0))
    const = lambda shape: pl.BlockSpec(shape, lambda i, j: (0,) * len(shape))
    state_spec = pl.BlockSpec((1, n_seg, CONV_W - 1, wg), lambda i, j: (i, 0, 0, 0))
    if row_offset:
        cache_spec = pl.BlockSpec(
            (None, pl.Element(tm * H_ATT), pl.Element(D_V)),
            lambda i, j: (i, pl.multiple_of((row_offset + j * tm) * H_ATT, 8 * H_ATT), 0))
    else:
        cache_spec = pl.BlockSpec((None, tm * H_ATT, D_V), lambda i, j: (i, j, 0))
    if transpose_v:
        assert tm % transpose_v == 0
        vb_spec = pl.BlockSpec((None, tm // transpose_v, wg, transpose_v),
                               lambda i, j: (i, j, 0, 0))
        vb_shape = jax.ShapeDtypeStruct((b, l // transpose_v, wg, transpose_v), bf16)
    else:
        vb_spec = pl.BlockSpec((None, tm, wg), lambda i, j: (i, j, 0))
        vb_shape = jax.ShapeDtypeStruct((b, l, wg), bf16)
    cache_shape = jax.ShapeDtypeStruct((b, (row_offset + l) * H_ATT, D_V), f32)
    out_shape = (
        jax.ShapeDtypeStruct((b, l, wg), bf16),
        jax.ShapeDtypeStruct((b, l, wg), bf16),
        cache_shape,
        jax.ShapeDtypeStruct((b, l, wg), bf16),
        cache_shape,
        vb_shape,
        jax.ShapeDtypeStruct((b, l, wg), f32),
        jax.ShapeDtypeStruct((b, l, wg), bf16),
        jax.ShapeDtypeStruct((b, n_seg, CONV_W - 1, wg), f32),
    )
    return pl.pallas_call(
        functools.partial(_in_proj_kernel, tm=tm, wg=wg, n_seg=n_seg, transpose_v=transpose_v),
        name="in_proj",
        grid=(b, nj),
        in_specs=[row_spec(d), state_spec, const((1, d)),
                  pl.BlockSpec((d, w_all), lambda i, j: (0, 0), pipeline_mode=pl.Buffered(1)),
                  const((CONV_W, wg))],
        out_specs=(row_spec(wg), row_spec(wg), cache_spec, row_spec(wg), cache_spec, vb_spec,
                   row_spec(wg), row_spec(wg), state_spec),
        out_shape=out_shape,
        scratch_shapes=[pltpu.VMEM((n_seg, CONV_W - 1, wg), f32)],
        compiler_params=pltpu.CompilerParams(
            dimension_semantics=("arbitrary", "arbitrary"), vmem_limit_bytes=VMEM_LIMIT),
    )(x, conv_init, norm_g.reshape(1, d), w_proj, conv_w)


def _fill_rows_kernel(km_ref, vm_ref, k_hbm, v_hbm, ko_ref, vo_ref):
    del k_hbm, v_hbm
    ko_ref[0] = km_ref[...]
    vo_ref[0] = vm_ref[...]


def _fill_meta_rows(k_meta, v_meta, k_all, v_all):
    b = k_all.shape[0]
    rows = k_meta.shape[0]
    small = pl.BlockSpec((rows, D_V), lambda i: (0, 0))
    lead = pl.BlockSpec((1, rows, D_V), lambda i: (i, 0, 0))
    whole = pl.BlockSpec(memory_space=pl.ANY)
    return pl.pallas_call(
        _fill_rows_kernel,
        name="fill_meta_rows",
        grid=(b,),
        in_specs=[small, small, whole, whole],
        out_specs=(lead, lead),
        out_shape=(jax.ShapeDtypeStruct(k_all.shape, k_all.dtype),
                   jax.ShapeDtypeStruct(v_all.shape, v_all.dtype)),
        input_output_aliases={2: 0, 3: 1},
    )(k_meta, v_meta, k_all, v_all)


def _rel_bucket(rel):
    nb = N_BUCKETS // 2
    ret = np.where(rel > 0, nb, 0)
    n = np.abs(rel)
    max_exact = nb // 2
    nf = np.maximum(n, 1).astype(np.float32)
    large = max_exact + (np.log(nf / np.float32(max_exact)) / np.float32(math.log(MAX_DIST / max_exact))
                         * np.float32(nb - max_exact)).astype(np.int32)
    large = np.minimum(large, nb - 1)
    return (ret + np.where(n < max_exact, n, large)).astype(np.int32)


def _chunk_id_np(pos):
    return np.where(pos < 0, -1, pos // CHUNK)


def _assert_far_past(q_pos, k_pos):
    rel = k_pos[None, :] - q_pos[:, None]
    nb = N_BUCKETS // 2
    max_exact = nb // 2
    n = np.abs(rel).astype(np.float64)
    large = max_exact + np.log(n / max_exact) / math.log(MAX_DIST / max_exact) * (nb - max_exact)
    assert np.all(rel < 0) and np.all(large >= nb), "tile is not in the saturated bucket"
    assert np.all(_chunk_id_np(k_pos)[None, :] <= _chunk_id_np(q_pos)[:, None])


def _bias_kernel(rb_ref, *refs):
    n = len(refs) // 3
    h = pl.program_id(0)
    far = rb_ref[FAR_BUCKET, h]
    shifted = [rb_ref[b, h] - far for b in range(N_BUCKETS)]
    for j in range(n):
        bucket = refs[2 * j][...]
        acc = jnp.zeros(bucket.shape, f32)
        for b in range(N_BUCKETS):
            acc = jnp.where(bucket == b, shifted[b], acc)
        refs[2 * n + j][0] = jnp.where(refs[2 * j + 1][...] != 0, acc * LOG2E, NEG)


def _build_biases(tiles, rel_bias):
    args, specs, out_specs, out_shapes = [], [], [], []
    for q_pos, k_pos, keys_major in tiles:
        rel = k_pos[None, :] - q_pos[:, None]
        vis = _chunk_id_np(k_pos)[None, :] <= _chunk_id_np(q_pos)[:, None]
        if keys_major:
            rel, vis = rel.T, vis.T
        r, c = rel.shape
        args += [jnp.asarray(_rel_bucket(rel)), jnp.asarray(vis.astype(np.int32))]
        specs += [pl.BlockSpec((r, c), lambda h: (0, 0))] * 2
        out_specs.append(pl.BlockSpec((1, r, c), lambda h: (h, 0, 0)))
        out_shapes.append(jax.ShapeDtypeStruct((H_ATT, r, c), f32))
    return pl.pallas_call(
        _bias_kernel,
        name="bias_tiles",
        grid=(H_ATT,),
        in_specs=[pl.BlockSpec(memory_space=pltpu.SMEM)] + specs,
        out_specs=tuple(out_specs),
        out_shape=tuple(out_shapes),
    )(rel_bias, *args)


def _diff_lambda(lam_ref):
    lv = lam_ref[...]
    s1 = jnp.sum(lv[0:1] * lv[1:2], axis=-1, keepdims=True)
    s2 = jnp.sum(lv[2:3] * lv[3:4], axis=-1, keepdims=True)
    return jnp.exp(s1) - jnp.exp(s2) + LAM_INIT


def _gate_heads(o1, o2, lam, sg, gate):
    o = o1 - lam * o2
    return _rms(o, sg) * (1.0 - LAM_INIT) * gate


def _prompt_attn_kernel(qa_ref, qb_ref, k_ref, vt_ref, km_ref, vmt_ref, bn_ref, bm_ref, gate_ref,
                        sg_ref, lam_ref, x_ref, yc_ref, wo_ref, fg_ref, y_ref,
                        sa_sc, mx_sc, m_sc, l_sc, acc_sc, og_sc, *, t, n_sub, n_q):
    i = pl.program_id(2)
    t2 = 2 * t
    n_wide = i * (n_sub // 2)
    slots = [(c, p) for c in range(2) for p in range(n_sub)]

    def keys(start, rows):
        return k_ref[0, pl.ds(pl.multiple_of(start, t), rows), :]

    def cols(p):
        return slice(p * t, (p + 1) * t)

    def qk(kt, j, c, p):
        rows = pl.ds(pl.multiple_of((j * n_sub + p) * t, t), t)
        return lax.dot_general(kt, (qa_ref, qb_ref)[c][0, rows, :], (((1,), (1,)), ((), ())),
                               preferred_element_type=f32)

    def col_max(*parts):
        m = jnp.max(parts[0], axis=0, keepdims=True)
        for part in parts[1:]:
            m = jnp.maximum(m, jnp.max(part, axis=0, keepdims=True))
        return m

    def update(c, p, s_parts, vt_parts):
        m_prev = m_sc[c, :, cols(p)]
        m_new = jnp.maximum(m_prev, mx_sc[c, :, cols(p)])
        alpha = jnp.exp2(m_prev - m_new)
        e_sum = pv = None
        for s, vt in zip(s_parts, vt_parts):
            e = jnp.exp2(s - m_new)
            e_k = jnp.sum(e, axis=0, keepdims=True)
            pv_k = jnp.dot(vt, e.astype(bf16), preferred_element_type=f32)
            e_sum = e_k if e_sum is None else e_sum + e_k
            pv = pv_k if pv is None else pv + pv_k
        l_sc[c, :, cols(p)] = alpha * l_sc[c, :, cols(p)] + e_sum
        acc_sc[c, :, cols(p)] = alpha * acc_sc[c, :, cols(p)] + pv
        m_sc[c, :, cols(p)] = m_new

    def near_bias(s, kind):
        if kind == "both":
            return s + bn_ref[0]
        if kind == "own":
            return s + bn_ref[0, t:t2]
        return jnp.concatenate([s[:t], s[t:] + bn_ref[0, 0:t]], axis=0)

    def far_refill(kt, j, c, p, bias=None):
        s = qk(kt, j, c, p)
        if bias is not None:
            s = near_bias(s, bias)
        sa_sc[c, 0:t2, cols(p)] = s
        mx_sc[c, :, cols(p)] = col_max(s)

    def far_update(c, p, vt):
        update(c, p, [sa_sc[c, 0:t2, cols(p)]], [vt])

    def far_tile(w, refill):
        vt = vt_ref[0, w]
        for c, p in slots:
            far_update(c, p, vt)
            refill(c, p)

    def own_a_rows(p):
        return t if p == 0 else t2

    def own_a_refill(j, c, p, meta_bias=None):
        rows = own_a_rows(p)
        s = qk(keys(n_wide * t2, rows), j, c, p)
        if p < 3:
            s = near_bias(s, ("own", "both", "before")[p])
        sm = qk(km_ref[...], j, c, p)
        if meta_bias is not None:
            sm = sm + meta_bias
        sa_sc[c, 0:rows, cols(p)] = s
        sa_sc[c, t2:t2 + N_META, cols(p)] = sm
        mx_sc[c, :, cols(p)] = col_max(s, sm)

    def own_a_update(c, p):
        rows = own_a_rows(p)
        update(c, p, [sa_sc[c, 0:rows, cols(p)], sa_sc[c, t2:t2 + N_META, cols(p)]],
               [vt_ref[0, n_wide, :, 0:rows], vmt_ref[...]])

    def own_b_rows(p):
        return t if p == 2 else t2

    def own_b_refill(j, c, p):
        rows = own_b_rows(p)
        s = near_bias(qk(keys((n_wide + 1) * t2, rows), j, c, p), ("own", "both")[p - 2])
        sa_sc[c, 0:rows, cols(p)] = s
        mx_sc[c, :, cols(p)] = col_max(s)

    def own_b_update(c, p):
        rows = own_b_rows(p)
        update(c, p, [sa_sc[c, 0:rows, cols(p)]], [vt_ref[0, n_wide + 1, :, 0:rows]])

    m_sc[...] = jnp.full(m_sc.shape, NEG, f32)
    l_sc[...] = jnp.zeros(l_sc.shape, f32)
    acc_sc[...] = jnp.zeros(acc_sc.shape, f32)

    def far_pair(pp, carry):
        w = 2 * pp
        kt1, kt2 = keys((w + 1) * t2, t2), keys((w + 2) * t2, t2)
        far_tile(w, lambda c, p: far_refill(kt1, i, c, p))
        far_tile(w + 1, lambda c, p: far_refill(kt2, i, c, p))
        return carry

    def own_keys_and_finish():
        j_next = jnp.minimum(i + 1, n_q - 1)
        kt0 = keys(0, t2)
        for c, p in slots:
            own_a_update(c, p)
            if p < 2:
                far_refill(kt0, j_next, c, p)
            else:
                own_b_refill(i, c, p)
        for c, p in slots:
            if p >= 2:
                own_b_update(c, p)
                far_refill(kt0, j_next, c, p)
        o = acc_sc[0] / l_sc[0] - _diff_lambda(lam_ref) * (acc_sc[1] / l_sc[1])
        o = o * lax.rsqrt(jnp.mean(o * o, axis=0, keepdims=True) + EPS)
        og = (o.T * (sg_ref[...] * (1.0 - LAM_INIT)) * gate_ref[0]).astype(bf16)
        head = pl.program_id(1)
        rows = pl.ds(pl.multiple_of(i * (n_sub * t), n_sub * t), n_sub * t)

        @pl.when(head < H_ATT - 1)
        def _():
            og_sc[head, rows, :] = og

        @pl.when(head == H_ATT - 1)
        def _():
            for p in range(n_sub):
                sub = pl.ds(pl.multiple_of((i * n_sub + p) * t, t), t)
                heads = [og_sc[hd, sub, :] for hd in range(H_ATT - 1)] + [og[cols(p)]]
                y_ref[0, cols(p), :] = _merge(jnp.concatenate(heads, axis=1), yc_ref[0, cols(p), :],
                                              x_ref[0, cols(p), :], wo_ref, fg_ref[...])

    @pl.when(i == 0)
    def _():
        for c, p in slots:
            own_a_refill(i, c, p, bm_ref[0] if p == 0 else None)
        own_keys_and_finish()

    @pl.when(i >= 1)
    def _():
        lax.fori_loop(0, i - 1, far_pair, 0)
        kt1 = keys((n_wide - 1) * t2, t2)
        far_tile(n_wide - 2,
                 lambda c, p: far_refill(kt1, i, c, p, "before" if p == 0 else None))
        far_tile(n_wide - 1, lambda c, p: own_a_refill(i, c, p))
        own_keys_and_finish()


def _prompt_attn(qa, qb, kb, vt, km, vmt, bias_near, bias_meta, gate, subln_g, lam_vecs,
                 x, yc, wo_bf, final_g):
    b, l, _ = qa.shape
    t = ATT_TILE
    tq = ATT_SUB * t
    assert ATT_SUB == 4 and l % tq == 0
    assert vt.shape == (b, l // (2 * t), W_ATT, 2 * t)
    tile_spec = pl.BlockSpec((1, tq, D_V), lambda bi, h, i: (bi, i, h))
    seq_spec = pl.BlockSpec((1, l, D_V), lambda bi, h, i: (bi, 0, h))
    vt_spec = pl.BlockSpec((1, l // (2 * t), D_V, 2 * t), lambda bi, h, i: (bi, 0, h, 0))
    per_head = lambda rows: pl.BlockSpec((1, rows, t), lambda bi, h, i: (h, 0, 0))
    d = x.shape[2]
    last_head = lambda width: pl.BlockSpec(
        (1, tq, width), lambda bi, h, i: (bi, jnp.where(h == H_ATT - 1, i, 0), 0))
    return pl.pallas_call(
        functools.partial(_prompt_attn_kernel, t=t, n_sub=ATT_SUB, n_q=l // tq),
        name="prompt_attn",
        grid=(b, H_ATT, l // tq),
        in_specs=[
            seq_spec, seq_spec, seq_spec, vt_spec,
            pl.BlockSpec((N_META, D_V), lambda bi, h, i: (0, h)),
            pl.BlockSpec((D_V, N_META), lambda bi, h, i: (h, 0)),
            per_head(2 * t), per_head(N_META),
            tile_spec,
            pl.BlockSpec((1, D_V), lambda bi, h, i: (0, 0)),
            pl.BlockSpec((4, D_QK), lambda bi, h, i: (0, 0)),
            last_head(d), last_head(yc.shape[2]),
            pl.BlockSpec(wo_bf.shape, lambda bi, h, i: (0, 0), pipeline_mode=pl.Buffered(1)),
            pl.BlockSpec((1, d), lambda bi, h, i: (0, 0)),
        ],
        out_specs=last_head(d),
        out_shape=jax.ShapeDtypeStruct((b, l, d), f32),
        scratch_shapes=[pltpu.VMEM((2, 2 * t + N_META, tq), f32),
                        pltpu.VMEM((2, 1, tq), f32),
                        pltpu.VMEM((2, 1, tq), f32), pltpu.VMEM((2, 1, tq), f32),
                        pltpu.VMEM((2, D_V, tq), f32),
                        pltpu.VMEM((H_ATT - 1, l, D_V), bf16)],
        compiler_params=pltpu.CompilerParams(
            dimension_semantics=("arbitrary", "arbitrary", "arbitrary"),
            vmem_limit_bytes=VMEM_LIMIT),
    )(qa, qb, kb, vt, km, vmt, bias_near, bias_meta, gate, subln_g, lam_vecs,
      x, yc, wo_bf, final_g.reshape(1, d))


def _merge(og, yc, x, wo_ref, fg):
    y = x + jnp.dot(jnp.concatenate([og, yc], axis=1), wo_ref[...], preferred_element_type=f32)
    return _rms(y, fg)


def _small_attn_kernel(*refs, has_cache):
    if has_cache:
        (qa_ref, qb_ref, kn_ref, vn_ref, bn_ref, ck_ref, cv_ref, bc_ref,
         gate_ref, yc_ref, x_ref, sg_ref, lam_ref, wo_ref, fg_ref, y_ref) = refs
    else:
        (qa_ref, qb_ref, kn_ref, vn_ref, bn_ref,
         gate_ref, yc_ref, x_ref, sg_ref, lam_ref, wo_ref, fg_ref, y_ref) = refs
    lq = qa_ref.shape[1]
    lam = _diff_lambda(lam_ref)
    nt = (((1,), (1,)), ((), ()))
    both = lambda a: jnp.concatenate([a, a], axis=0)
    heads = []
    for h in range(H_ATT):
        cols = slice(h * D_V, (h + 1) * D_V)
        q2 = jnp.concatenate([qa_ref[0][:, cols], qb_ref[0][:, cols]], axis=0)
        kn = kn_ref[0][:, cols]
        vn = vn_ref[0][:, cols]
        sn = lax.dot_general(q2, kn, nt, preferred_element_type=f32) + both(bn_ref[h])
        m = jnp.max(sn, axis=-1, keepdims=True)
        if has_cache:
            n_rows = ck_ref.shape[1] // H_ATT
            ck = ck_ref[0, pl.ds(h, n_rows, stride=H_ATT), :].astype(bf16)
            cv = cv_ref[0, pl.ds(h, n_rows, stride=H_ATT), :].astype(bf16)
            sc = lax.dot_general(q2, ck, nt, preferred_element_type=f32) + both(bc_ref[h])
            m = jnp.maximum(m, jnp.max(sc, axis=-1, keepdims=True))
        en = jnp.exp2(sn - m)
        l = jnp.sum(en, axis=-1, keepdims=True)
        acc = jnp.dot(en.astype(bf16), vn, preferred_element_type=f32)
        if has_cache:
            ec = jnp.exp2(sc - m)
            l = l + jnp.sum(ec, axis=-1, keepdims=True)
            acc = acc + jnp.dot(ec.astype(bf16), cv, preferred_element_type=f32)
        o = acc / l
        heads.append(_gate_heads(o[:lq], o[lq:], lam, sg_ref[...], gate_ref[0][:, cols]))
    og = jnp.concatenate(heads, axis=-1).astype(bf16)
    y_ref[0] = _merge(og, yc_ref[0], x_ref[0], wo_ref, fg_ref[...])


def _small_attn(qa, qb, kb, vb, bias_new, gate, yc, x, subln_g, lam_vecs, wo_bf, final_g,
                cache_k=None, cache_v=None, bias_cache=None):
    b, lq, d = x.shape
    has_cache = cache_k is not None
    per_b = lambda rows, width: pl.BlockSpec((1, rows, width), lambda i: (i, 0, 0))
    const = lambda shape: pl.BlockSpec(shape, lambda i: (0,) * len(shape))
    w = qa.shape[2]
    args = [qa, qb, kb, vb, bias_new]
    specs = [per_b(lq, w)] * 4 + [const(bias_new.shape)]
    if has_cache:
        args += [cache_k, cache_v, bias_cache]
        specs += [per_b(*cache_k.shape[1:]), per_b(*cache_v.shape[1:]), const(bias_cache.shape)]
    args += [gate, yc, x, subln_g, lam_vecs, wo_bf, final_g.reshape(1, d)]
    specs += [per_b(lq, w), per_b(lq, yc.shape[2]), per_b(lq, d), const(subln_g.shape),
              const(lam_vecs.shape), const(wo_bf.shape), const((1, d))]
    return pl.pallas_call(
        functools.partial(_small_attn_kernel, has_cache=has_cache),
        name="small_attn",
        grid=(b,),
        in_specs=specs,
        out_specs=per_b(lq, d),
        out_shape=jax.ShapeDtypeStruct((b, lq, d), f32),
        compiler_params=pltpu.CompilerParams(
            dimension_semantics=("arbitrary",), vmem_limit_bytes=VMEM_LIMIT),
    )(*args)


def kernel(x_prompt, x_sample, cache_k, cache_v, state_conv, meta_tokens, rel_bias, norm_g, w_in,
           conv_w, lambda_q1, lambda_k1, lambda_q2, lambda_k2, subln_g, w_out, final_g):
    bp, sp, d = x_prompt.shape
    bd, sd, _ = x_sample.shape
    depth = w_in.shape[0]
    assert depth == 1, "single-layer step only"
    past = cache_k.shape[2] - N_META
    t = ATT_TILE
    assert sp % (ATT_SUB * t) == 0 and t % CHUNK == 0

    w_proj = w_in[0]
    wo_bf = w_out[0].astype(bf16)
    lam_vecs = jnp.stack([lambda_q1[0], lambda_k1[0], lambda_q2[0], lambda_k2[0]])
    sg = subln_g[0].reshape(1, D_V)
    wg = w_proj.shape[1] // N_SPLIT

    zeros_state = jnp.zeros((1, 1, CONV_W - 1, wg), f32)
    (qma, qmb, km, kmb, vm, vmb, gm, ycm, cm) = _in_proj(
        meta_tokens[None], zeros_state, norm_g[0], w_proj, conv_w[0], tm=N_META)
    (qpa, qpb, kp, kpb, vp, vpt, gp, ycp, cp) = _in_proj(
        x_prompt, jnp.broadcast_to(cm, (bp, 1, CONV_W - 1, wg)), norm_g[0], w_proj, conv_w[0],
        tm=PROJ_ROWS, row_offset=N_META, transpose_v=2 * t)
    kp, vp = _fill_meta_rows(km[0], vm[0], kp, vp)
    per_stream = lambda a: a.reshape(bd, sd, a.shape[-1])
    (qda, qdb, kd, kdb, vd, vdb, gd, ycd, cd) = _in_proj(
        x_sample.reshape(1, bd * sd, d), state_conv[0][None], norm_g[0], w_proj, conv_w[0],
        tm=bd * sd)
    qda, qdb, kdb, vdb, gd, ycd = map(per_stream, (qda, qdb, kdb, vdb, gd, ycd))

    meta_pos = np.arange(-N_META, 0)
    tile_pos = np.arange(t)
    d_qpos = past + np.arange(sd)
    cache_pos = np.concatenate([meta_pos, np.arange(past)])
    (bias_near, bias_meta, bias_meta_self, bias_dec_new, bias_dec_cache) = (
        _build_biases([
            (tile_pos + t, np.arange(0, 2 * t), True),
            (tile_pos, meta_pos, True),
            (meta_pos, meta_pos, False),
            (d_qpos, d_qpos, False),
            (d_qpos, cache_pos, False),
        ], rel_bias))
    for off in range(2, sp // t):
        _assert_far_past(tile_pos + off * t, tile_pos)
    for off in range(1, sp // t):
        _assert_far_past(tile_pos + off * t, meta_pos)

    y_prompt = _prompt_attn(qpa, qpb, kpb, vpt, kmb[0], vmb[0].T, bias_near, bias_meta, gp, sg,
                            lam_vecs, x_prompt, ycp, wo_bf, final_g)

    _small_attn(qma, qmb, kmb, vmb, bias_meta_self, gm, ycm, meta_tokens[None], sg, lam_vecs,
                wo_bf, final_g)

    y_sample = _small_attn(
        qda, qdb, kdb, vdb, bias_dec_new, gd, ycd, x_sample, sg, lam_vecs, wo_bf, final_g,
        cache_k=cache_k[0].reshape(bd, (N_META + past) * H_ATT, D_V),
        cache_v=cache_v[0].reshape(bd, (N_META + past) * H_ATT, D_V),
        bias_cache=bias_dec_cache)

    return (y_prompt, y_sample,
            kp.reshape(1, bp, N_META + sp, H_ATT, D_V),
            vp.reshape(1, bp, N_META + sp, H_ATT, D_V),
            cp.reshape(1, bp, CONV_W - 1, wg),
            kd.reshape(1, bd, sd, H_ATT, D_V),
            vd.reshape(1, bd, sd, H_ATT, D_V),
            cd)
```

```python
import functools
import math

import numpy as np
import jax
import jax.numpy as jnp
from jax import lax
from jax.experimental import pallas as pl
from jax.experimental.pallas import tpu as pltpu

f32 = jnp.float32
bf16 = jnp.bfloat16

CHUNK = 64
N_META = 16
H_ATT = 4
D_QK = 64
D_V = 2 * D_QK
W_ATT = H_ATT * D_V
CONV_W = 3
N_SPLIT = 8
N_BUCKETS = 32
MAX_DIST = 128
EPS = 1e-6
NEG = -1e30
SCALE = D_QK ** -0.5
LOG2E = math.log2(math.e)
LAM_INIT = 0.8 - 0.6 * math.exp(-0.3 * 0)
FAR_BUCKET = N_BUCKETS // 2 - 1

ATT_TILE = 256
ATT_SUB = 4
VMEM_LIMIT = 48 * 1024 * 1024
PROJ_ROWS = 1024


def _silu(z):
    return z * (1.0 / (1.0 + jnp.exp(-z)))


def _rms(x, g):
    return x * lax.rsqrt(jnp.mean(x * x, axis=-1, keepdims=True) + EPS) * g


def _in_proj_kernel(x_ref, cinit_ref, g_ref, w_ref, cw_ref,
                    qa_ref, qb_ref, k_ref, kb_ref, v_ref, vb_ref, gate_ref, yc_ref, clast_ref,
                    carry_ref, *, tm, wg, n_seg, transpose_v):
    j = pl.program_id(1)

    @pl.when(j == 0)
    def _():
        carry_ref[...] = cinit_ref[0]

    x = x_ref[0]
    h = _rms(x, g_ref[...]).astype(bf16)

    def proj(g):
        return jnp.dot(h, w_ref[:, g * wg:(g + 1) * wg], preferred_element_type=f32)

    def store_heads(ref, a):
        for hd in range(H_ATT):
            ref[pl.ds(hd, tm, stride=H_ATT), :] = a[:, hd * D_V:(hd + 1) * D_V]

    q = (proj(0) * (SCALE * LOG2E)).astype(bf16)
    first_half = lax.broadcasted_iota(jnp.int32, q.shape, 1) % D_V < D_QK
    zero = jnp.zeros_like(q)
    qa_ref[0] = jnp.where(first_half, q, zero)
    qb_ref[0] = jnp.where(first_half, zero, q)
    k = proj(1)
    store_heads(k_ref, k)
    kb_ref[0] = k.astype(bf16)
    v = proj(2)
    store_heads(v_ref, v)
    if transpose_v:
        for ch in range(tm // transpose_v):
            vb_ref[ch] = v[ch * transpose_v:(ch + 1) * transpose_v].T.astype(bf16)
    else:
        vb_ref[...] = v.astype(bf16)
    gate_ref[0] = _silu(proj(3))

    cu = proj(5) * proj(6)
    prev = carry_ref[...]
    seg = tm // n_seg
    row = lax.broadcasted_iota(jnp.int32, cu.shape, 0) % seg

    def before(k):
        if n_seg == 1:
            return prev[0, k:k + 1]
        return jnp.concatenate(
            [jnp.broadcast_to(prev[sg, k:k + 1], (seg, wg)) for sg in range(n_seg)], axis=0)

    cu_m1 = jnp.where(row == 0, before(1), pltpu.roll(cu, 1, axis=0))
    cu_m2 = jnp.where(row == 0, before(0),
                      jnp.where(row == 1, before(1), pltpu.roll(cu, 2, axis=0)))
    cw = cw_ref[...]
    conv = cw[0:1] * cu_m2 + cw[1:2] * cu_m1 + cw[2:3] * cu
    yc_ref[0] = (_silu(proj(7)) * proj(4) * conv).astype(bf16)

    for sg in range(n_seg):
        last = cu[(sg + 1) * seg - 2:(sg + 1) * seg]
        carry_ref[sg] = last
        clast_ref[0, sg] = last


def _in_proj(x, conv_init, norm_g, w_bf, conv_w, tm, row_offset=0, transpose_v=0):
    b, l, d = x.shape
    w_all = w_bf.shape[1]
    wg = w_all // N_SPLIT
    assert wg == W_ATT
    nj = l // tm
    n_seg = conv_init.shape[1]
    assert n_seg == 1 or nj == 1
    row_spec = lambda width: pl.BlockSpec((1, tm, width), lambda i, j: (i, j, 0))
    const = lambda shape: pl.BlockSpec(shape, lambda i, j: (0,) * len(shape))
    state_spec = pl.BlockSpec((1, n_seg, CONV_W - 1, wg), lambda i, j: (i, 0, 0, 0))
    if row_offset:
        cache_spec = pl.BlockSpec(
            (None, pl.Element(tm * H_ATT), pl.Element(D_V)),
            lambda i, j: (i, pl.multiple_of((row_offset + j * tm) * H_ATT, 8 * H_ATT), 0))
    else:
        cache_spec = pl.BlockSpec((None, tm * H_ATT, D_V), lambda i, j: (i, j, 0))
    if transpose_v:
        assert tm % transpose_v == 0
        vb_spec = pl.BlockSpec((None, tm // transpose_v, wg, transpose_v),
                               lambda i, j: (i, j, 0, 0))
        vb_shape = jax.ShapeDtypeStruct((b, l // transpose_v, wg, transpose_v), bf16)
    else:
        vb_spec = pl.BlockSpec((None, tm, wg), lambda i, j: (i, j, 0))
        vb_shape = jax.ShapeDtypeStruct((b, l, wg), bf16)
    cache_shape = jax.ShapeDtypeStruct((b, (row_offset + l) * H_ATT, D_V), f32)
    out_shape = (
        jax.ShapeDtypeStruct((b, l, wg), bf16),
        jax.ShapeDtypeStruct((b, l, wg), bf16),
        cache_shape,
        jax.ShapeDtypeStruct((b, l, wg), bf16),
        cache_shape,
        vb_shape,
        jax.ShapeDtypeStruct((b, l, wg), f32),
        jax.ShapeDtypeStruct((b, l, wg), bf16),
        jax.ShapeDtypeStruct((b, n_seg, CONV_W - 1, wg), f32),
    )
    return pl.pallas_call(
        functools.partial(_in_proj_kernel, tm=tm, wg=wg, n_seg=n_seg, transpose_v=transpose_v),
        name="in_proj",
        grid=(b, nj),
        in_specs=[row_spec(d), state_spec, const((1, d)),
                  pl.BlockSpec((d, w_all), lambda i, j: (0, 0), pipeline_mode=pl.Buffered(1)),
                  const((CONV_W, wg))],
        out_specs=(row_spec(wg), row_spec(wg), cache_spec, row_spec(wg), cache_spec, vb_spec,
                   row_spec(wg), row_spec(wg), state_spec),
        out_shape=out_shape,
        scratch_shapes=[pltpu.VMEM((n_seg, CONV_W - 1, wg), f32)],
        compiler_params=pltpu.CompilerParams(
            dimension_semantics=("arbitrary", "arbitrary"), vmem_limit_bytes=VMEM_LIMIT),
    )(x, conv_init, norm_g.reshape(1, d), w_bf, conv_w)


def _fill_rows_kernel(km_ref, vm_ref, k_hbm, v_hbm, ko_ref, vo_ref):
    del k_hbm, v_hbm
    ko_ref[0] = km_ref[...]
    vo_ref[0] = vm_ref[...]


def _fill_meta_rows(k_meta, v_meta, k_all, v_all):
    b = k_all.shape[0]
    rows = k_meta.shape[0]
    small = pl.BlockSpec((rows, D_V), lambda i: (0, 0))
    lead = pl.BlockSpec((1, rows, D_V), lambda i: (i, 0, 0))
    whole = pl.BlockSpec(memory_space=pl.ANY)
    return pl.pallas_call(
        _fill_rows_kernel,
        name="fill_meta_rows",
        grid=(b,),
        in_specs=[small, small, whole, whole],
        out_specs=(lead, lead),
        out_shape=(jax.ShapeDtypeStruct(k_all.shape, k_all.dtype),
                   jax.ShapeDtypeStruct(v_all.shape, v_all.dtype)),
        input_output_aliases={2: 0, 3: 1},
    )(k_meta, v_meta, k_all, v_all)


def _rel_bucket(rel):
    nb = N_BUCKETS // 2
    ret = np.where(rel > 0, nb, 0)
    n = np.abs(rel)
    max_exact = nb // 2
    nf = np.maximum(n, 1).astype(np.float32)
    large = max_exact + (np.log(nf / np.float32(max_exact)) / np.float32(math.log(MAX_DIST / max_exact))
                         * np.float32(nb - max_exact)).astype(np.int32)
    large = np.minimum(large, nb - 1)
    return (ret + np.where(n < max_exact, n, large)).astype(np.int32)


def _chunk_id_np(pos):
    return np.where(pos < 0, -1, pos // CHUNK)


def _assert_far_past(q_pos, k_pos):
    rel = k_pos[None, :] - q_pos[:, None]
    nb = N_BUCKETS // 2
    max_exact = nb // 2
    n = np.abs(rel).astype(np.float64)
    large = max_exact + np.log(n / max_exact) / math.log(MAX_DIST / max_exact) * (nb - max_exact)
    assert np.all(rel < 0) and np.all(large >= nb), "tile is not in the saturated bucket"
    assert np.all(_chunk_id_np(k_pos)[None, :] <= _chunk_id_np(q_pos)[:, None])


def _bias_kernel(rb_ref, *refs):
    n = len(refs) // 3
    h = pl.program_id(0)
    far = rb_ref[FAR_BUCKET, h]
    shifted = [rb_ref[b, h] - far for b in range(N_BUCKETS)]
    for j in range(n):
        bucket = refs[2 * j][...]
        acc = jnp.zeros(bucket.shape, f32)
        for b in range(N_BUCKETS):
            acc = jnp.where(bucket == b, shifted[b], acc)
        refs[2 * n + j][0] = jnp.where(refs[2 * j + 1][...] != 0, acc * LOG2E, NEG)


def _build_biases(tiles, rel_bias):
    args, specs, out_specs, out_shapes = [], [], [], []
    for q_pos, k_pos, keys_major in tiles:
        rel = k_pos[None, :] - q_pos[:, None]
        vis = _chunk_id_np(k_pos)[None, :] <= _chunk_id_np(q_pos)[:, None]
        if keys_major:
            rel, vis = rel.T, vis.T
        r, c = rel.shape
        args += [jnp.asarray(_rel_bucket(rel)), jnp.asarray(vis.astype(np.int32))]
        specs += [pl.BlockSpec((r, c), lambda h: (0, 0))] * 2
        out_specs.append(pl.BlockSpec((1, r, c), lambda h: (h, 0, 0)))
        out_shapes.append(jax.ShapeDtypeStruct((H_ATT, r, c), f32))
    return pl.pallas_call(
        _bias_kernel,
        name="bias_tiles",
        grid=(H_ATT,),
        in_specs=[pl.BlockSpec(memory_space=pltpu.SMEM)] + specs,
        out_specs=tuple(out_specs),
        out_shape=tuple(out_shapes),
    )(rel_bias, *args)


def _diff_lambda(lam_ref):
    lv = lam_ref[...]
    s1 = jnp.sum(lv[0:1] * lv[1:2], axis=-1, keepdims=True)
    s2 = jnp.sum(lv[2:3] * lv[3:4], axis=-1, keepdims=True)
    return jnp.exp(s1) - jnp.exp(s2) + LAM_INIT


def _gate_heads(o1, o2, lam, sg, gate):
    o = o1 - lam * o2
    return _rms(o, sg) * (1.0 - LAM_INIT) * gate


def _prompt_attn_kernel(qa_ref, qb_ref, k_ref, vt_ref, km_ref, vmt_ref, bn_ref, bm_ref, gate_ref,
                        sg_ref, lam_ref, x_ref, yc_ref, wo_ref, fg_ref, y_ref,
                        sa_sc, mx_sc, m_sc, l_sc, acc_sc, og_sc, *, t, n_sub, n_q):
    i = pl.program_id(2)
    t2 = 2 * t
    n_wide = i * (n_sub // 2)
    slots = [(c, p) for c in range(2) for p in range(n_sub)]

    def keys(start, rows):
        return k_ref[0, pl.ds(pl.multiple_of(start, t), rows), :]

    def cols(p):
        return slice(p * t, (p + 1) * t)

    def qk(kt, j, c, p):
        rows = pl.ds(pl.multiple_of((j * n_sub + p) * t, t), t)
        return lax.dot_general(kt, (qa_ref, qb_ref)[c][0, rows, :], (((1,), (1,)), ((), ())),
                               preferred_element_type=f32)

    def col_max(*parts):
        m = jnp.max(parts[0], axis=0, keepdims=True)
        for part in parts[1:]:
            m = jnp.maximum(m, jnp.max(part, axis=0, keepdims=True))
        return m

    def update(c, p, s_parts, vt_parts):
        m_prev = m_sc[c, :, cols(p)]
        m_new = jnp.maximum(m_prev, mx_sc[c, :, cols(p)])
        alpha = jnp.exp2(m_prev - m_new)
        e_sum = pv = None
        for s, vt in zip(s_parts, vt_parts):
            e = jnp.exp2(s - m_new)
            e_k = jnp.sum(e, axis=0, keepdims=True)
            pv_k = jnp.dot(vt, e.astype(bf16), preferred_element_type=f32)
            e_sum = e_k if e_sum is None else e_sum + e_k
            pv = pv_k if pv is None else pv + pv_k
        l_sc[c, :, cols(p)] = alpha * l_sc[c, :, cols(p)] + e_sum
        acc_sc[c, :, cols(p)] = alpha * acc_sc[c, :, cols(p)] + pv
        m_sc[c, :, cols(p)] = m_new

    def near_bias(s, kind):
        if kind == "both":
            return s + bn_ref[0]
        if kind == "own":
            return s + bn_ref[0, t:t2]
        return jnp.concatenate([s[:t], s[t:] + bn_ref[0, 0:t]], axis=0)

    def far_refill(kt, j, c, p, bias=None):
        s = qk(kt, j, c, p)
        if bias is not None:
            s = near_bias(s, bias)
        sa_sc[c, 0:t2, cols(p)] = s
        mx_sc[c, :, cols(p)] = col_max(s)

    def far_update(c, p, vt):
        update(c, p, [sa_sc[c, 0:t2, cols(p)]], [vt])

    def far_tile(w, refill):
        vt = vt_ref[0, w]
        for c, p in slots:
            far_update(c, p, vt)
            refill(c, p)

    def own_a_rows(p):
        return t if p == 0 else t2

    def own_a_refill(j, c, p, meta_bias=None):
        rows = own_a_rows(p)
        s = qk(keys(n_wide * t2, rows), j, c, p)
        if p < 3:
            s = near_bias(s, ("own", "both", "before")[p])
        sm = qk(km_ref[...], j, c, p)
        if meta_bias is not None:
            sm = sm + meta_bias
        sa_sc[c, 0:rows, cols(p)] = s
        sa_sc[c, t2:t2 + N_META, cols(p)] = sm
        mx_sc[c, :, cols(p)] = col_max(s, sm)

    def own_a_update(c, p):
        rows = own_a_rows(p)
        update(c, p, [sa_sc[c, 0:rows, cols(p)], sa_sc[c, t2:t2 + N_META, cols(p)]],
               [vt_ref[0, n_wide, :, 0:rows], vmt_ref[...]])

    def own_b_rows(p):
        return t if p == 2 else t2

    def own_b_refill(j, c, p):
        rows = own_b_rows(p)
        s = near_bias(qk(keys((n_wide + 1) * t2, rows), j, c, p), ("own", "both")[p - 2])
        sa_sc[c, 0:rows, cols(p)] = s
        mx_sc[c, :, cols(p)] = col_max(s)

    def own_b_update(c, p):
        rows = own_b_rows(p)
        update(c, p, [sa_sc[c, 0:rows, cols(p)]], [vt_ref[0, n_wide + 1, :, 0:rows]])

    m_sc[...] = jnp.full(m_sc.shape, NEG, f32)
    l_sc[...] = jnp.zeros(l_sc.shape, f32)
    acc_sc[...] = jnp.zeros(acc_sc.shape, f32)

    def far_pair(pp, carry):
        w = 2 * pp
        kt1, kt2 = keys((w + 1) * t2, t2), keys((w + 2) * t2, t2)
        far_tile(w, lambda c, p: far_refill(kt1, i, c, p))
        far_tile(w + 1, lambda c, p: far_refill(kt2, i, c, p))
        return carry

    def own_keys_and_finish(look_ahead):
        kt0 = keys(0, t2)
        for c, p in slots:
            own_a_update(c, p)
            if p >= 2:
                own_b_refill(i, c, p)
            elif look_ahead:
                far_refill(kt0, i + 1, c, p)
        for c, p in slots:
            if p >= 2:
                own_b_update(c, p)
                if look_ahead:
                    far_refill(kt0, i + 1, c, p)
        o = acc_sc[0] / l_sc[0] - _diff_lambda(lam_ref) * (acc_sc[1] / l_sc[1])
        o = o * lax.rsqrt(jnp.mean(o * o, axis=0, keepdims=True) + EPS)
        og = (o.T * (sg_ref[...] * (1.0 - LAM_INIT)) * gate_ref[0]).astype(bf16)
        head = pl.program_id(1)
        rows = pl.ds(pl.multiple_of(i * (n_sub * t), n_sub * t), n_sub * t)

        @pl.when(head < H_ATT - 1)
        def _():
            og_sc[head, rows, :] = og

        @pl.when(head == H_ATT - 1)
        def _():
            for p in range(n_sub):
                sub = pl.ds(pl.multiple_of((i * n_sub + p) * t, t), t)
                heads = [og_sc[hd, sub, :] for hd in range(H_ATT - 1)] + [og[cols(p)]]
                y_ref[0, cols(p), :] = _merge(jnp.concatenate(heads, axis=1), yc_ref[0, cols(p), :],
                                              x_ref[0, cols(p), :], wo_ref, fg_ref[...])

    @pl.when(i == 0)
    def _():
        for c, p in slots:
            own_a_refill(i, c, p, bm_ref[0] if p == 0 else None)
        own_keys_and_finish(look_ahead=n_q > 1)

    def later_tile(look_ahead):
        lax.fori_loop(0, i - 1, far_pair, 0)
        kt1 = keys((n_wide - 1) * t2, t2)
        far_tile(n_wide - 2,
                 lambda c, p: far_refill(kt1, i, c, p, "before" if p == 0 else None))
        far_tile(n_wide - 1, lambda c, p: own_a_refill(i, c, p))
        own_keys_and_finish(look_ahead)

    @pl.when((i >= 1) & (i < n_q - 1))
    def _():
        later_tile(look_ahead=True)

    @pl.when((i >= 1) & (i == n_q - 1))
    def _():
        later_tile(look_ahead=False)


def _prompt_attn(qa, qb, kb, vt, km, vmt, bias_near, bias_meta, gate, subln_g, lam_vecs,
                 x, yc, wo_bf, final_g):
    b, l, _ = qa.shape
    t = ATT_TILE
    tq = ATT_SUB * t
    assert ATT_SUB == 4 and l % tq == 0
    assert vt.shape == (b, l // (2 * t), W_ATT, 2 * t)
    tile_spec = pl.BlockSpec((1, tq, D_V), lambda bi, h, i: (bi, i, h))
    seq_spec = pl.BlockSpec((1, l, D_V), lambda bi, h, i: (bi, 0, h))
    vt_spec = pl.BlockSpec((1, l // (2 * t), D_V, 2 * t), lambda bi, h, i: (bi, 0, h, 0))
    per_head = lambda rows: pl.BlockSpec((1, rows, t), lambda bi, h, i: (h, 0, 0))
    d = x.shape[2]
    last_head = lambda width: pl.BlockSpec(
        (1, tq, width), lambda bi, h, i: (bi, jnp.where(h == H_ATT - 1, i, 0), 0))
    return pl.pallas_call(
        functools.partial(_prompt_attn_kernel, t=t, n_sub=ATT_SUB, n_q=l // tq),
        name="prompt_attn",
        grid=(b, H_ATT, l // tq),
        in_specs=[
            seq_spec, seq_spec, seq_spec, vt_spec,
            pl.BlockSpec((N_META, D_V), lambda bi, h, i: (0, h)),
            pl.BlockSpec((D_V, N_META), lambda bi, h, i: (h, 0)),
            per_head(2 * t), per_head(N_META),
            tile_spec,
            pl.BlockSpec((1, D_V), lambda bi, h, i: (0, 0)),
            pl.BlockSpec((4, D_QK), lambda bi, h, i: (0, 0)),
            last_head(d), last_head(yc.shape[2]),
            pl.BlockSpec(wo_bf.shape, lambda bi, h, i: (0, 0), pipeline_mode=pl.Buffered(1)),
            pl.BlockSpec((1, d), lambda bi, h, i: (0, 0)),
        ],
        out_specs=last_head(d),
        out_shape=jax.ShapeDtypeStruct((b, l, d), f32),
        scratch_shapes=[pltpu.VMEM((2, 2 * t + N_META, tq), f32),
                        pltpu.VMEM((2, 1, tq), f32),
                        pltpu.VMEM((2, 1, tq), f32), pltpu.VMEM((2, 1, tq), f32),
                        pltpu.VMEM((2, D_V, tq), f32),
                        pltpu.VMEM((H_ATT - 1, l, D_V), bf16)],
        compiler_params=pltpu.CompilerParams(
            dimension_semantics=("arbitrary", "arbitrary", "arbitrary"),
            vmem_limit_bytes=VMEM_LIMIT),
    )(qa, qb, kb, vt, km, vmt, bias_near, bias_meta, gate, subln_g, lam_vecs,
      x, yc, wo_bf, final_g.reshape(1, d))


def _merge(og, yc, x, wo_ref, fg):
    y = x + jnp.dot(jnp.concatenate([og, yc], axis=1), wo_ref[...], preferred_element_type=f32)
    return _rms(y, fg)


def _small_attn_kernel(*refs, has_cache):
    if has_cache:
        (qa_ref, qb_ref, kn_ref, vn_ref, bn_ref, ck_ref, cv_ref, bc_ref,
         gate_ref, yc_ref, x_ref, sg_ref, lam_ref, wo_ref, fg_ref, y_ref) = refs
    else:
        (qa_ref, qb_ref, kn_ref, vn_ref, bn_ref,
         gate_ref, yc_ref, x_ref, sg_ref, lam_ref, wo_ref, fg_ref, y_ref) = refs
    lq = qa_ref.shape[1]
    lam = _diff_lambda(lam_ref)
    nt = (((1,), (1,)), ((), ()))
    both = lambda a: jnp.concatenate([a, a], axis=0)
    heads = []
    for h in range(H_ATT):
        cols = slice(h * D_V, (h + 1) * D_V)
        q2 = jnp.concatenate([qa_ref[0][:, cols], qb_ref[0][:, cols]], axis=0)
        kn = kn_ref[0][:, cols]
        vn = vn_ref[0][:, cols]
        sn = lax.dot_general(q2, kn, nt, preferred_element_type=f32) + both(bn_ref[h])
        m = jnp.max(sn, axis=-1, keepdims=True)
        if has_cache:
            n_rows = ck_ref.shape[1] // H_ATT
            ck = ck_ref[0, pl.ds(h, n_rows, stride=H_ATT), :].astype(bf16)
            cv = cv_ref[0, pl.ds(h, n_rows, stride=H_ATT), :].astype(bf16)
            sc = lax.dot_general(q2, ck, nt, preferred_element_type=f32) + both(bc_ref[h])
            m = jnp.maximum(m, jnp.max(sc, axis=-1, keepdims=True))
        en = jnp.exp2(sn - m)
        l = jnp.sum(en, axis=-1, keepdims=True)
        acc = jnp.dot(en.astype(bf16), vn, preferred_element_type=f32)
        if has_cache:
            ec = jnp.exp2(sc - m)
            l = l + jnp.sum(ec, axis=-1, keepdims=True)
            acc = acc + jnp.dot(ec.astype(bf16), cv, preferred_element_type=f32)
        o = acc / l
        heads.append(_gate_heads(o[:lq], o[lq:], lam, sg_ref[...], gate_ref[0][:, cols]))
    og = jnp.concatenate(heads, axis=-1).astype(bf16)
    y_ref[0] = _merge(og, yc_ref[0], x_ref[0], wo_ref, fg_ref[...])


def _small_attn(qa, qb, kb, vb, bias_new, gate, yc, x, subln_g, lam_vecs, wo_bf, final_g,
                cache_k=None, cache_v=None, bias_cache=None):
    b, lq, d = x.shape
    has_cache = cache_k is not None
    per_b = lambda rows, width: pl.BlockSpec((1, rows, width), lambda i: (i, 0, 0))
    const = lambda shape: pl.BlockSpec(shape, lambda i: (0,) * len(shape))
    w = qa.shape[2]
    args = [qa, qb, kb, vb, bias_new]
    specs = [per_b(lq, w)] * 4 + [const(bias_new.shape)]
    if has_cache:
        args += [cache_k, cache_v, bias_cache]
        specs += [per_b(*cache_k.shape[1:]), per_b(*cache_v.shape[1:]), const(bias_cache.shape)]
    args += [gate, yc, x, subln_g, lam_vecs, wo_bf, final_g.reshape(1, d)]
    specs += [per_b(lq, w), per_b(lq, yc.shape[2]), per_b(lq, d), const(subln_g.shape),
              const(lam_vecs.shape), const(wo_bf.shape), const((1, d))]
    return pl.pallas_call(
        functools.partial(_small_attn_kernel, has_cache=has_cache),
        name="small_attn",
        grid=(b,),
        in_specs=specs,
        out_specs=per_b(lq, d),
        out_shape=jax.ShapeDtypeStruct((b, lq, d), f32),
        compiler_params=pltpu.CompilerParams(
            dimension_semantics=("arbitrary",), vmem_limit_bytes=VMEM_LIMIT),
    )(*args)


def kernel(x_prompt, x_sample, cache_k, cache_v, state_conv, meta_tokens, rel_bias, norm_g, w_in,
           conv_w, lambda_q1, lambda_k1, lambda_q2, lambda_k2, subln_g, w_out, final_g):
    bp, sp, d = x_prompt.shape
    bd, sd, _ = x_sample.shape
    depth = w_in.shape[0]
    assert depth == 1, "single-layer step only"
    past = cache_k.shape[2] - N_META
    t = ATT_TILE
    assert sp % (ATT_SUB * t) == 0 and t % CHUNK == 0

    w_bf = w_in[0].astype(bf16)
    wo_bf = w_out[0].astype(bf16)
    lam_vecs = jnp.stack([lambda_q1[0], lambda_k1[0], lambda_q2[0], lambda_k2[0]])
    sg = subln_g[0].reshape(1, D_V)
    wg = w_bf.shape[1] // N_SPLIT

    zeros_state = jnp.zeros((1, 1, CONV_W - 1, wg), f32)
    (qma, qmb, km, kmb, vm, vmb, gm, ycm, cm) = _in_proj(
        meta_tokens[None], zeros_state, norm_g[0], w_bf, conv_w[0], tm=N_META)
    (qpa, qpb, kp, kpb, vp, vpt, gp, ycp, cp) = _in_proj(
        x_prompt, jnp.broadcast_to(cm, (bp, 1, CONV_W - 1, wg)), norm_g[0], w_bf, conv_w[0],
        tm=PROJ_ROWS, row_offset=N_META, transpose_v=2 * t)
    kp, vp = _fill_meta_rows(km[0], vm[0], kp, vp)
    per_stream = lambda a: a.reshape(bd, sd, a.shape[-1])
    (qda, qdb, kd, kdb, vd, vdb, gd, ycd, cd) = _in_proj(
        x_sample.reshape(1, bd * sd, d), state_conv[0][None], norm_g[0], w_bf, conv_w[0],
        tm=bd * sd)
    qda, qdb, kdb, vdb, gd, ycd = map(per_stream, (qda, qdb, kdb, vdb, gd, ycd))

    meta_pos = np.arange(-N_META, 0)
    tile_pos = np.arange(t)
    d_qpos = past + np.arange(sd)
    cache_pos = np.concatenate([meta_pos, np.arange(past)])
    (bias_near, bias_meta, bias_meta_self, bias_dec_new, bias_dec_cache) = (
        _build_biases([
            (tile_pos + t, np.arange(0, 2 * t), True),
            (tile_pos, meta_pos, True),
            (meta_pos, meta_pos, False),
            (d_qpos, d_qpos, False),
            (d_qpos, cache_pos, False),
        ], rel_bias))
    for off in range(2, sp // t):
        _assert_far_past(tile_pos + off * t, tile_pos)
    for off in range(1, sp // t):
        _assert_far_past(tile_pos + off * t, meta_pos)

    y_prompt = _prompt_attn(qpa, qpb, kpb, vpt, kmb[0], vmb[0].T, bias_near, bias_meta, gp, sg,
                            lam_vecs, x_prompt, ycp, wo_bf, final_g)

    _small_attn(qma, qmb, kmb, vmb, bias_meta_self, gm, ycm, meta_tokens[None], sg, lam_vecs,
                wo_bf, final_g)

    y_sample = _small_attn(
        qda, qdb, kdb, vdb, bias_dec_new, gd, ycd, x_sample, sg, lam_vecs, wo_bf, final_g,
        cache_k=cache_k[0].reshape(bd, (N_META + past) * H_ATT, D_V),
        cache_v=cache_v[0].reshape(bd, (N_META + past) * H_ATT, D_V),
        bias_cache=bias_dec_cache)

    return (y_prompt, y_sample,
            kp.reshape(1, bp, N_META + sp, H_ATT, D_V),
            vp.reshape(1, bp, N_META + sp, H_ATT, D_V),
            cp.reshape(1, bp, CONV_W - 1, wg),
            kd.reshape(1, bd, sd, H_ATT, D_V),
            vd.reshape(1, bd, sd, H_ATT, D_V),
            cd)
```

```python
import functools
import math

import numpy as np
import jax
import jax.numpy as jnp
from jax import lax
from jax.experimental import pallas as pl
from jax.experimental.pallas import tpu as pltpu

f32 = jnp.float32
bf16 = jnp.bfloat16

CHUNK = 64
N_META = 16
H_ATT = 4
D_QK = 64
D_V = 2 * D_QK
W_ATT = H_ATT * D_V
CONV_W = 3
N_SPLIT = 8
N_BUCKETS = 32
MAX_DIST = 128
EPS = 1e-6
NEG = -1e30
SCALE = D_QK ** -0.5
LOG2E = math.log2(math.e)
LAM_INIT = 0.8 - 0.6 * math.exp(-0.3 * 0)
FAR_BUCKET = N_BUCKETS // 2 - 1

ATT_TILE = 256
ATT_SUB = 4
VMEM_LIMIT = 48 * 1024 * 1024
PROJ_ROWS = 1024


def _silu(z):
    return z * (1.0 / (1.0 + jnp.exp(-z)))


def _rms(x, g):
    return x * lax.rsqrt(jnp.mean(x * x, axis=-1, keepdims=True) + EPS) * g


def _in_proj_kernel(x_ref, cinit_ref, g_ref, w_ref, cw_ref,
                    qa_ref, qb_ref, k_ref, kb_ref, v_ref, vb_ref, gate_ref, yc_ref, clast_ref,
                    carry_ref, *, tm, wg, n_seg, transpose_v):
    j = pl.program_id(1)

    @pl.when(j == 0)
    def _():
        carry_ref[...] = cinit_ref[0]

    x = x_ref[0]
    h = _rms(x, g_ref[...]).astype(bf16)

    def proj(g):
        return jnp.dot(h, w_ref[:, g * wg:(g + 1) * wg], preferred_element_type=f32)

    def store_heads(ref, a):
        for hd in range(H_ATT):
            ref[pl.ds(hd, tm, stride=H_ATT), :] = a[:, hd * D_V:(hd + 1) * D_V]

    q = (proj(0) * (SCALE * LOG2E)).astype(bf16)
    first_half = lax.broadcasted_iota(jnp.int32, q.shape, 1) % D_V < D_QK
    zero = jnp.zeros_like(q)
    qa_ref[0] = jnp.where(first_half, q, zero)
    qb_ref[0] = jnp.where(first_half, zero, q)
    k = proj(1)
    store_heads(k_ref, k)
    kb_ref[0] = k.astype(bf16)
    v = proj(2)
    store_heads(v_ref, v)
    if transpose_v:
        for ch in range(tm // transpose_v):
            vb_ref[ch] = v[ch * transpose_v:(ch + 1) * transpose_v].T.astype(bf16)
    else:
        vb_ref[...] = v.astype(bf16)
    gate_ref[0] = _silu(proj(3))

    cu = proj(5) * proj(6)
    prev = carry_ref[...]
    seg = tm // n_seg
    row = lax.broadcasted_iota(jnp.int32, cu.shape, 0) % seg

    def before(k):
        if n_seg == 1:
            return prev[0, k:k + 1]
        return jnp.concatenate(
            [jnp.broadcast_to(prev[sg, k:k + 1], (seg, wg)) for sg in range(n_seg)], axis=0)

    cu_m1 = jnp.where(row == 0, before(1), pltpu.roll(cu, 1, axis=0))
    cu_m2 = jnp.where(row == 0, before(0),
                      jnp.where(row == 1, before(1), pltpu.roll(cu, 2, axis=0)))
    cw = cw_ref[...]
    conv = cw[0:1] * cu_m2 + cw[1:2] * cu_m1 + cw[2:3] * cu
    yc_ref[0] = (_silu(proj(7)) * proj(4) * conv).astype(bf16)

    for sg in range(n_seg):
        last = cu[(sg + 1) * seg - 2:(sg + 1) * seg]
        carry_ref[sg] = last
        clast_ref[0, sg] = last


def _in_proj(x, conv_init, norm_g, w_bf, conv_w, tm, row_offset=0, transpose_v=0):
    b, l, d = x.shape
    w_all = w_bf.shape[1]
    wg = w_all // N_SPLIT
    assert wg == W_ATT
    nj = l // tm
    n_seg = conv_init.shape[1]
    assert n_seg == 1 or nj == 1
    row_spec = lambda width: pl.BlockSpec((1, tm, width), lambda i, j: (i, j, 0))
    const = lambda shape: pl.BlockSpec(shape, lambda i, j: (0,) * len(shape))
    state_spec = pl.BlockSpec((1, n_seg, CONV_W - 1, wg), lambda i, j: (i, 0, 0, 0))
    if row_offset:
        cache_spec = pl.BlockSpec(
            (None, pl.Element(tm * H_ATT), pl.Element(D_V)),
            lambda i, j: (i, _aligned((row_offset + j * tm) * H_ATT, 8 * H_ATT), 0))
    else:
        cache_spec = pl.BlockSpec((None, tm * H_ATT, D_V), lambda i, j: (i, j, 0))
    if transpose_v:
        assert tm % transpose_v == 0
        vb_spec = pl.BlockSpec((None, tm // transpose_v, wg, transpose_v),
                               lambda i, j: (i, j, 0, 0))
        vb_shape = jax.ShapeDtypeStruct((b, l // transpose_v, wg, transpose_v), bf16)
    else:
        vb_spec = pl.BlockSpec((None, tm, wg), lambda i, j: (i, j, 0))
        vb_shape = jax.ShapeDtypeStruct((b, l, wg), bf16)
    cache_shape = jax.ShapeDtypeStruct((b, (row_offset + l) * H_ATT, D_V), f32)
    out_shape = (
        jax.ShapeDtypeStruct((b, l, wg), bf16),
        jax.ShapeDtypeStruct((b, l, wg), bf16),
        cache_shape,
        jax.ShapeDtypeStruct((b, l, wg), bf16),
        cache_shape,
        vb_shape,
        jax.ShapeDtypeStruct((b, l, wg), f32),
        jax.ShapeDtypeStruct((b, l, wg), bf16),
        jax.ShapeDtypeStruct((b, n_seg, CONV_W - 1, wg), f32),
    )
    return pl.pallas_call(
        functools.partial(_in_proj_kernel, tm=tm, wg=wg, n_seg=n_seg, transpose_v=transpose_v),
        name="in_proj",
        grid=(b, nj),
        in_specs=[row_spec(d), state_spec, const((1, d)),
                  pl.BlockSpec((d, w_all), lambda i, j: (0, 0), pipeline_mode=pl.Buffered(1)),
                  const((CONV_W, wg))],
        out_specs=(row_spec(wg), row_spec(wg), cache_spec, row_spec(wg), cache_spec, vb_spec,
                   row_spec(wg), row_spec(wg), state_spec),
        out_shape=out_shape,
        scratch_shapes=[pltpu.VMEM((n_seg, CONV_W - 1, wg), f32)],
        compiler_params=pltpu.CompilerParams(
            dimension_semantics=("arbitrary", "arbitrary"), vmem_limit_bytes=VMEM_LIMIT),
    )(x, conv_init, norm_g.reshape(1, d), w_bf, conv_w)


def _fill_rows_kernel(km_ref, vm_ref, k_hbm, v_hbm, ko_ref, vo_ref):
    del k_hbm, v_hbm
    ko_ref[0] = km_ref[...]
    vo_ref[0] = vm_ref[...]


def _fill_meta_rows(k_meta, v_meta, k_all, v_all):
    b = k_all.shape[0]
    rows = k_meta.shape[0]
    small = pl.BlockSpec((rows, D_V), lambda i: (0, 0))
    lead = pl.BlockSpec((1, rows, D_V), lambda i: (i, 0, 0))
    whole = pl.BlockSpec(memory_space=pl.ANY)
    return pl.pallas_call(
        _fill_rows_kernel,
        name="fill_meta_rows",
        grid=(b,),
        in_specs=[small, small, whole, whole],
        out_specs=(lead, lead),
        out_shape=(jax.ShapeDtypeStruct(k_all.shape, k_all.dtype),
                   jax.ShapeDtypeStruct(v_all.shape, v_all.dtype)),
        input_output_aliases={2: 0, 3: 1},
    )(k_meta, v_meta, k_all, v_all)


def _rel_bucket(rel):
    nb = N_BUCKETS // 2
    ret = np.where(rel > 0, nb, 0)
    n = np.abs(rel)
    max_exact = nb // 2
    nf = np.maximum(n, 1).astype(np.float32)
    large = max_exact + (np.log(nf / np.float32(max_exact)) / np.float32(math.log(MAX_DIST / max_exact))
                         * np.float32(nb - max_exact)).astype(np.int32)
    large = np.minimum(large, nb - 1)
    return (ret + np.where(n < max_exact, n, large)).astype(np.int32)


def _chunk_id_np(pos):
    return np.where(pos < 0, -1, pos // CHUNK)


def _assert_far_past(q_pos, k_pos):
    rel = k_pos[None, :] - q_pos[:, None]
    nb = N_BUCKETS // 2
    max_exact = nb // 2
    n = np.abs(rel).astype(np.float64)
    large = max_exact + np.log(n / max_exact) / math.log(MAX_DIST / max_exact) * (nb - max_exact)
    assert np.all(rel < 0) and np.all(large >= nb), "tile is not in the saturated bucket"
    assert np.all(_chunk_id_np(k_pos)[None, :] <= _chunk_id_np(q_pos)[:, None])


def _bias_kernel(rb_ref, *refs):
    n = len(refs) // 3
    h = pl.program_id(0)
    far = rb_ref[FAR_BUCKET, h]
    shifted = [rb_ref[b, h] - far for b in range(N_BUCKETS)]
    for j in range(n):
        bucket = refs[2 * j][...]
        acc = jnp.zeros(bucket.shape, f32)
        for b in range(N_BUCKETS):
            acc = jnp.where(bucket == b, shifted[b], acc)
        refs[2 * n + j][0] = jnp.where(refs[2 * j + 1][...] != 0, acc * LOG2E, NEG)


def _build_biases(tiles, rel_bias):
    args, specs, out_specs, out_shapes = [], [], [], []
    for q_pos, k_pos, keys_major in tiles:
        rel = k_pos[None, :] - q_pos[:, None]
        vis = _chunk_id_np(k_pos)[None, :] <= _chunk_id_np(q_pos)[:, None]
        if keys_major:
            rel, vis = rel.T, vis.T
        r, c = rel.shape
        args += [jnp.asarray(_rel_bucket(rel)), jnp.asarray(vis.astype(np.int32))]
        specs += [pl.BlockSpec((r, c), lambda h: (0, 0))] * 2
        out_specs.append(pl.BlockSpec((1, r, c), lambda h: (h, 0, 0)))
        out_shapes.append(jax.ShapeDtypeStruct((H_ATT, r, c), f32))
    return pl.pallas_call(
        _bias_kernel,
        name="bias_tiles",
        grid=(H_ATT,),
        in_specs=[pl.BlockSpec(memory_space=pltpu.SMEM)] + specs,
        out_specs=tuple(out_specs),
        out_shape=tuple(out_shapes),
    )(rel_bias, *args)


def _aligned(x, m):
    return x if isinstance(x, int) else pl.multiple_of(x, m)


def _diff_lambda(lam_ref):
    lv = lam_ref[...]
    s1 = jnp.sum(lv[0:1] * lv[1:2], axis=-1, keepdims=True)
    s2 = jnp.sum(lv[2:3] * lv[3:4], axis=-1, keepdims=True)
    return jnp.exp(s1) - jnp.exp(s2) + LAM_INIT


def _gate_heads(o1, o2, lam, sg, gate):
    o = o1 - lam * o2
    return _rms(o, sg) * (1.0 - LAM_INIT) * gate


def _prompt_attn_kernel(*refs, t, n_sub, n_q):
    for k in range(n_q):
        pl.when(pl.program_id(2) == k)(
            functools.partial(_attn_tile, *refs, i=k, t=t, n_sub=n_sub, n_q=n_q))


def _attn_tile(qa_ref, qb_ref, k_ref, vt_ref, km_ref, vmt_ref, bn_ref, bm_ref, gate_ref,
               sg_ref, lam_ref, x_ref, yc_ref, wo_ref, fg_ref, y_ref,
               sa_sc, mx_sc, m_sc, l_sc, acc_sc, og_sc, *, i, t, n_sub, n_q):
    t2 = 2 * t
    n_wide = i * (n_sub // 2)
    slots = [(c, p) for c in range(2) for p in range(n_sub)]

    def keys(start, rows):
        return k_ref[0, pl.ds(_aligned(start, t), rows), :]

    def cols(p):
        return slice(p * t, (p + 1) * t)

    def qk(kt, j, c, p):
        rows = pl.ds(_aligned((j * n_sub + p) * t, t), t)
        return lax.dot_general(kt, (qa_ref, qb_ref)[c][0, rows, :], (((1,), (1,)), ((), ())),
                               preferred_element_type=f32)

    def col_max(*parts):
        m = jnp.max(parts[0], axis=0, keepdims=True)
        for part in parts[1:]:
            m = jnp.maximum(m, jnp.max(part, axis=0, keepdims=True))
        return m

    def update(c, p, s_parts, vt_parts):
        m_prev = m_sc[c, :, cols(p)]
        m_new = jnp.maximum(m_prev, mx_sc[c, :, cols(p)])
        alpha = jnp.exp2(m_prev - m_new)
        e_sum = pv = None
        for s, vt in zip(s_parts, vt_parts):
            e = jnp.exp2(s - m_new)
            e_k = jnp.sum(e, axis=0, keepdims=True)
            pv_k = jnp.dot(vt, e.astype(bf16), preferred_element_type=f32)
            e_sum = e_k if e_sum is None else e_sum + e_k
            pv = pv_k if pv is None else pv + pv_k
        l_sc[c, :, cols(p)] = alpha * l_sc[c, :, cols(p)] + e_sum
        acc_sc[c, :, cols(p)] = alpha * acc_sc[c, :, cols(p)] + pv
        m_sc[c, :, cols(p)] = m_new

    def near_bias(s, kind):
        if kind == "both":
            return s + bn_ref[0]
        if kind == "own":
            return s + bn_ref[0, t:t2]
        return jnp.concatenate([s[:t], s[t:] + bn_ref[0, 0:t]], axis=0)

    def far_refill(kt, j, c, p, bias=None):
        s = qk(kt, j, c, p)
        if bias is not None:
            s = near_bias(s, bias)
        sa_sc[c, 0:t2, cols(p)] = s
        mx_sc[c, :, cols(p)] = col_max(s)

    def far_update(c, p, vt):
        update(c, p, [sa_sc[c, 0:t2, cols(p)]], [vt])

    def far_tile(w, refill):
        vt = vt_ref[0, w]
        for c, p in slots:
            far_update(c, p, vt)
            refill(c, p)

    def own_a_rows(p):
        return t if p == 0 else t2

    def own_a_refill(j, c, p, meta_bias=None):
        rows = own_a_rows(p)
        s = qk(keys(n_wide * t2, rows), j, c, p)
        if p < 3:
            s = near_bias(s, ("own", "both", "before")[p])
        sm = qk(km_ref[...], j, c, p)
        if meta_bias is not None:
            sm = sm + meta_bias
        sa_sc[c, 0:rows, cols(p)] = s
        sa_sc[c, t2:t2 + N_META, cols(p)] = sm
        mx_sc[c, :, cols(p)] = col_max(s, sm)

    def own_a_update(c, p):
        rows = own_a_rows(p)
        update(c, p, [sa_sc[c, 0:rows, cols(p)], sa_sc[c, t2:t2 + N_META, cols(p)]],
               [vt_ref[0, n_wide, :, 0:rows], vmt_ref[...]])

    def own_b_rows(p):
        return t if p == 2 else t2

    def own_b_refill(j, c, p):
        rows = own_b_rows(p)
        s = near_bias(qk(keys((n_wide + 1) * t2, rows), j, c, p), ("own", "both")[p - 2])
        sa_sc[c, 0:rows, cols(p)] = s
        mx_sc[c, :, cols(p)] = col_max(s)

    def own_b_update(c, p):
        rows = own_b_rows(p)
        update(c, p, [sa_sc[c, 0:rows, cols(p)]], [vt_ref[0, n_wide + 1, :, 0:rows]])

    m_sc[...] = jnp.full(m_sc.shape, NEG, f32)
    l_sc[...] = jnp.zeros(l_sc.shape, f32)
    acc_sc[...] = jnp.zeros(acc_sc.shape, f32)

    def far_pair(pp):
        w = 2 * pp
        kt1, kt2 = keys((w + 1) * t2, t2), keys((w + 2) * t2, t2)
        far_tile(w, lambda c, p: far_refill(kt1, i, c, p))
        far_tile(w + 1, lambda c, p: far_refill(kt2, i, c, p))

    def own_keys_and_finish(look_ahead):
        kt0 = keys(0, t2)
        for c, p in slots:
            own_a_update(c, p)
            if p >= 2:
                own_b_refill(i, c, p)
            elif look_ahead:
                far_refill(kt0, i + 1, c, p)
        for c, p in slots:
            if p >= 2:
                own_b_update(c, p)
                if look_ahead:
                    far_refill(kt0, i + 1, c, p)
        o = acc_sc[0] / l_sc[0] - _diff_lambda(lam_ref) * (acc_sc[1] / l_sc[1])
        o = o * lax.rsqrt(jnp.mean(o * o, axis=0, keepdims=True) + EPS)
        og = (o.T * (sg_ref[...] * (1.0 - LAM_INIT)) * gate_ref[0]).astype(bf16)
        head = pl.program_id(1)
        rows = pl.ds(_aligned(i * (n_sub * t), n_sub * t), n_sub * t)

        @pl.when(head < H_ATT - 1)
        def _():
            og_sc[head, rows, :] = og

        @pl.when(head == H_ATT - 1)
        def _():
            for p in range(n_sub):
                sub = pl.ds(_aligned((i * n_sub + p) * t, t), t)
                heads = [og_sc[hd, sub, :] for hd in range(H_ATT - 1)] + [og[cols(p)]]
                y_ref[0, cols(p), :] = _merge(jnp.concatenate(heads, axis=1), yc_ref[0, cols(p), :],
                                              x_ref[0, cols(p), :], wo_ref, fg_ref[...])

    if i == 0:
        for c, p in slots:
            own_a_refill(i, c, p, bm_ref[0] if p == 0 else None)
    else:
        for pp in range(i - 1):
            far_pair(pp)
        kt1 = keys((n_wide - 1) * t2, t2)
        far_tile(n_wide - 2,
                 lambda c, p: far_refill(kt1, i, c, p, "before" if p == 0 else None))
        far_tile(n_wide - 1, lambda c, p: own_a_refill(i, c, p))
    own_keys_and_finish(look_ahead=i < n_q - 1)


def _prompt_attn(qa, qb, kb, vt, km, vmt, bias_near, bias_meta, gate, subln_g, lam_vecs,
                 x, yc, wo_bf, final_g):
    b, l, _ = qa.shape
    t = ATT_TILE
    tq = ATT_SUB * t
    assert ATT_SUB == 4 and l % tq == 0
    assert vt.shape == (b, l // (2 * t), W_ATT, 2 * t)
    tile_spec = pl.BlockSpec((1, tq, D_V), lambda bi, h, i: (bi, i, h))
    seq_spec = pl.BlockSpec((1, l, D_V), lambda bi, h, i: (bi, 0, h))
    vt_spec = pl.BlockSpec((1, l // (2 * t), D_V, 2 * t), lambda bi, h, i: (bi, 0, h, 0))
    per_head = lambda rows: pl.BlockSpec((1, rows, t), lambda bi, h, i: (h, 0, 0))
    d = x.shape[2]
    last_head = lambda width: pl.BlockSpec(
        (1, tq, width), lambda bi, h, i: (bi, jnp.where(h == H_ATT - 1, i, 0), 0))
    return pl.pallas_call(
        functools.partial(_prompt_attn_kernel, t=t, n_sub=ATT_SUB, n_q=l // tq),
        name="prompt_attn",
        grid=(b, H_ATT, l // tq),
        in_specs=[
            seq_spec, seq_spec, seq_spec, vt_spec,
            pl.BlockSpec((N_META, D_V), lambda bi, h, i: (0, h)),
            pl.BlockSpec((D_V, N_META), lambda bi, h, i: (h, 0)),
            per_head(2 * t), per_head(N_META),
            tile_spec,
            pl.BlockSpec((1, D_V), lambda bi, h, i: (0, 0)),
            pl.BlockSpec((4, D_QK), lambda bi, h, i: (0, 0)),
            last_head(d), last_head(yc.shape[2]),
            pl.BlockSpec(wo_bf.shape, lambda bi, h, i: (0, 0), pipeline_mode=pl.Buffered(1)),
            pl.BlockSpec((1, d), lambda bi, h, i: (0, 0)),
        ],
        out_specs=last_head(d),
        out_shape=jax.ShapeDtypeStruct((b, l, d), f32),
        scratch_shapes=[pltpu.VMEM((2, 2 * t + N_META, tq), f32),
                        pltpu.VMEM((2, 1, tq), f32),
                        pltpu.VMEM((2, 1, tq), f32), pltpu.VMEM((2, 1, tq), f32),
                        pltpu.VMEM((2, D_V, tq), f32),
                        pltpu.VMEM((H_ATT - 1, l, D_V), bf16)],
        compiler_params=pltpu.CompilerParams(
            dimension_semantics=("arbitrary", "arbitrary", "arbitrary"),
            vmem_limit_bytes=VMEM_LIMIT),
    )(qa, qb, kb, vt, km, vmt, bias_near, bias_meta, gate, subln_g, lam_vecs,
      x, yc, wo_bf, final_g.reshape(1, d))


def _merge(og, yc, x, wo_ref, fg):
    y = x + jnp.dot(jnp.concatenate([og, yc], axis=1), wo_ref[...], preferred_element_type=f32)
    return _rms(y, fg)


def _small_attn_kernel(*refs, has_cache):
    if has_cache:
        (qa_ref, qb_ref, kn_ref, vn_ref, bn_ref, ck_ref, cv_ref, bc_ref,
         gate_ref, yc_ref, x_ref, sg_ref, lam_ref, wo_ref, fg_ref, y_ref) = refs
    else:
        (qa_ref, qb_ref, kn_ref, vn_ref, bn_ref,
         gate_ref, yc_ref, x_ref, sg_ref, lam_ref, wo_ref, fg_ref, y_ref) = refs
    lq = qa_ref.shape[1]
    lam = _diff_lambda(lam_ref)
    nt = (((1,), (1,)), ((), ()))
    both = lambda a: jnp.concatenate([a, a], axis=0)
    heads = []
    for h in range(H_ATT):
        cols = slice(h * D_V, (h + 1) * D_V)
        q2 = jnp.concatenate([qa_ref[0][:, cols], qb_ref[0][:, cols]], axis=0)
        kn = kn_ref[0][:, cols]
        vn = vn_ref[0][:, cols]
        sn = lax.dot_general(q2, kn, nt, preferred_element_type=f32) + both(bn_ref[h])
        m = jnp.max(sn, axis=-1, keepdims=True)
        if has_cache:
            n_rows = ck_ref.shape[1] // H_ATT
            ck = ck_ref[0, pl.ds(h, n_rows, stride=H_ATT), :].astype(bf16)
            cv = cv_ref[0, pl.ds(h, n_rows, stride=H_ATT), :].astype(bf16)
            sc = lax.dot_general(q2, ck, nt, preferred_element_type=f32) + both(bc_ref[h])
            m = jnp.maximum(m, jnp.max(sc, axis=-1, keepdims=True))
        en = jnp.exp2(sn - m)
        l = jnp.sum(en, axis=-1, keepdims=True)
        acc = jnp.dot(en.astype(bf16), vn, preferred_element_type=f32)
        if has_cache:
            ec = jnp.exp2(sc - m)
            l = l + jnp.sum(ec, axis=-1, keepdims=True)
            acc = acc + jnp.dot(ec.astype(bf16), cv, preferred_element_type=f32)
        o = acc / l
        heads.append(_gate_heads(o[:lq], o[lq:], lam, sg_ref[...], gate_ref[0][:, cols]))
    og = jnp.concatenate(heads, axis=-1).astype(bf16)
    y_ref[0] = _merge(og, yc_ref[0], x_ref[0], wo_ref, fg_ref[...])


def _small_attn(qa, qb, kb, vb, bias_new, gate, yc, x, subln_g, lam_vecs, wo_bf, final_g,
                cache_k=None, cache_v=None, bias_cache=None):
    b, lq, d = x.shape
    has_cache = cache_k is not None
    per_b = lambda rows, width: pl.BlockSpec((1, rows, width), lambda i: (i, 0, 0))
    const = lambda shape: pl.BlockSpec(shape, lambda i: (0,) * len(shape))
    w = qa.shape[2]
    args = [qa, qb, kb, vb, bias_new]
    specs = [per_b(lq, w)] * 4 + [const(bias_new.shape)]
    if has_cache:
        args += [cache_k, cache_v, bias_cache]
        specs += [per_b(*cache_k.shape[1:]), per_b(*cache_v.shape[1:]), const(bias_cache.shape)]
    args += [gate, yc, x, subln_g, lam_vecs, wo_bf, final_g.reshape(1, d)]
    specs += [per_b(lq, w), per_b(lq, yc.shape[2]), per_b(lq, d), const(subln_g.shape),
              const(lam_vecs.shape), const(wo_bf.shape), const((1, d))]
    return pl.pallas_call(
        functools.partial(_small_attn_kernel, has_cache=has_cache),
        name="small_attn",
        grid=(b,),
        in_specs=specs,
        out_specs=per_b(lq, d),
        out_shape=jax.ShapeDtypeStruct((b, lq, d), f32),
        compiler_params=pltpu.CompilerParams(
            dimension_semantics=("arbitrary",), vmem_limit_bytes=VMEM_LIMIT),
    )(*args)


def kernel(x_prompt, x_sample, cache_k, cache_v, state_conv, meta_tokens, rel_bias, norm_g, w_in,
           conv_w, lambda_q1, lambda_k1, lambda_q2, lambda_k2, subln_g, w_out, final_g):
    bp, sp, d = x_prompt.shape
    bd, sd, _ = x_sample.shape
    depth = w_in.shape[0]
    assert depth == 1, "single-layer step only"
    past = cache_k.shape[2] - N_META
    t = ATT_TILE
    assert sp % (ATT_SUB * t) == 0 and t % CHUNK == 0

    w_bf = w_in[0].astype(bf16)
    wo_bf = w_out[0].astype(bf16)
    lam_vecs = jnp.stack([lambda_q1[0], lambda_k1[0], lambda_q2[0], lambda_k2[0]])
    sg = subln_g[0].reshape(1, D_V)
    wg = w_bf.shape[1] // N_SPLIT

    zeros_state = jnp.zeros((1, 1, CONV_W - 1, wg), f32)
    (qma, qmb, km, kmb, vm, vmb, gm, ycm, cm) = _in_proj(
        meta_tokens[None], zeros_state, norm_g[0], w_bf, conv_w[0], tm=N_META)
    (qpa, qpb, kp, kpb, vp, vpt, gp, ycp, cp) = _in_proj(
        x_prompt, jnp.broadcast_to(cm, (bp, 1, CONV_W - 1, wg)), norm_g[0], w_bf, conv_w[0],
        tm=PROJ_ROWS, row_offset=N_META, transpose_v=2 * t)
    kp, vp = _fill_meta_rows(km[0], vm[0], kp, vp)
    per_stream = lambda a: a.reshape(bd, sd, a.shape[-1])
    (qda, qdb, kd, kdb, vd, vdb, gd, ycd, cd) = _in_proj(
        x_sample.reshape(1, bd * sd, d), state_conv[0][None], norm_g[0], w_bf, conv_w[0],
        tm=bd * sd)
    qda, qdb, kdb, vdb, gd, ycd = map(per_stream, (qda, qdb, kdb, vdb, gd, ycd))

    meta_pos = np.arange(-N_META, 0)
    tile_pos = np.arange(t)
    d_qpos = past + np.arange(sd)
    cache_pos = np.concatenate([meta_pos, np.arange(past)])
    (bias_near, bias_meta, bias_meta_self, bias_dec_new, bias_dec_cache) = (
        _build_biases([
            (tile_pos + t, np.arange(0, 2 * t), True),
            (tile_pos, meta_pos, True),
            (meta_pos, meta_pos, False),
            (d_qpos, d_qpos, False),
            (d_qpos, cache_pos, False),
        ], rel_bias))
    for off in range(2, sp // t):
        _assert_far_past(tile_pos + off * t, tile_pos)
    for off in range(1, sp // t):
        _assert_far_past(tile_pos + off * t, meta_pos)

    y_prompt = _prompt_attn(qpa, qpb, kpb, vpt, kmb[0], vmb[0].T, bias_near, bias_meta, gp, sg,
                            lam_vecs, x_prompt, ycp, wo_bf, final_g)

    _small_attn(qma, qmb, kmb, vmb, bias_meta_self, gm, ycm, meta_tokens[None], sg, lam_vecs,
                wo_bf, final_g)

    y_sample = _small_attn(
        qda, qdb, kdb, vdb, bias_dec_new, gd, ycd, x_sample, sg, lam_vecs, wo_bf, final_g,
        cache_k=cache_k[0].reshape(bd, (N_META + past) * H_ATT, D_V),
        cache_v=cache_v[0].reshape(bd, (N_META + past) * H_ATT, D_V),
        bias_cache=bias_dec_cache)

    return (y_prompt, y_sample,
            kp.reshape(1, bp, N_META + sp, H_ATT, D_V),
            vp.reshape(1, bp, N_META + sp, H_ATT, D_V),
            cp.reshape(1, bp, CONV_W - 1, wg),
            kd.reshape(1, bd, sd, H_ATT, D_V),
            vd.reshape(1, bd, sd, H_ATT, D_V),
            cd)
```

```python
import functools
import math

import numpy as np
import jax
import jax.numpy as jnp
from jax import lax
from jax.experimental import pallas as pl
from jax.experimental.pallas import tpu as pltpu

f32 = jnp.float32
bf16 = jnp.bfloat16

CHUNK = 64
N_META = 16
H_ATT = 4
D_QK = 64
D_V = 2 * D_QK
W_ATT = H_ATT * D_V
CONV_W = 3
N_SPLIT = 8
N_BUCKETS = 32
MAX_DIST = 128
EPS = 1e-6
NEG = -1e30
SCALE = D_QK ** -0.5
LOG2E = math.log2(math.e)
LAM_INIT = 0.8 - 0.6 * math.exp(-0.3 * 0)
FAR_BUCKET = N_BUCKETS // 2 - 1

ATT_TILE = 256
ATT_SUB = 4
VMEM_LIMIT = 48 * 1024 * 1024
PROJ_ROWS = 1024


def _silu(z):
    return z * (1.0 / (1.0 + jnp.exp(-z)))


def _rms(x, g):
    return x * lax.rsqrt(jnp.mean(x * x, axis=-1, keepdims=True) + EPS) * g


def _in_proj_kernel(x_ref, cinit_ref, g_ref, w_ref, cw_ref,
                    qa_ref, qb_ref, k_ref, kb_ref, v_ref, vb_ref, gate_ref, yc_ref, clast_ref,
                    carry_ref, *, tm, wg, n_seg, transpose_v):
    j = pl.program_id(1)

    @pl.when(j == 0)
    def _():
        carry_ref[...] = cinit_ref[0]

    x = x_ref[0]
    h = _rms(x, g_ref[...]).astype(bf16)

    def proj(g):
        return jnp.dot(h, w_ref[:, g * wg:(g + 1) * wg], preferred_element_type=f32)

    def store_heads(ref, a):
        for hd in range(H_ATT):
            ref[pl.ds(hd, tm, stride=H_ATT), :] = a[:, hd * D_V:(hd + 1) * D_V]

    q = (proj(0) * (SCALE * LOG2E)).astype(bf16)
    first_half = lax.broadcasted_iota(jnp.int32, q.shape, 1) % D_V < D_QK
    zero = jnp.zeros_like(q)
    qa_ref[0] = jnp.where(first_half, q, zero)
    qb_ref[0] = jnp.where(first_half, zero, q)
    k = proj(1)
    store_heads(k_ref, k)
    kb_ref[0] = k.astype(bf16)
    v = proj(2)
    store_heads(v_ref, v)
    if transpose_v:
        for ch in range(tm // transpose_v):
            vb_ref[ch] = v[ch * transpose_v:(ch + 1) * transpose_v].T.astype(bf16)
    else:
        vb_ref[...] = v.astype(bf16)
    gate_ref[0] = _silu(proj(3))

    cu = proj(5) * proj(6)
    prev = carry_ref[...]
    seg = tm // n_seg
    row = lax.broadcasted_iota(jnp.int32, cu.shape, 0) % seg

    def before(k):
        if n_seg == 1:
            return prev[0, k:k + 1]
        return jnp.concatenate(
            [jnp.broadcast_to(prev[sg, k:k + 1], (seg, wg)) for sg in range(n_seg)], axis=0)

    cu_m1 = jnp.where(row == 0, before(1), pltpu.roll(cu, 1, axis=0))
    cu_m2 = jnp.where(row == 0, before(0),
                      jnp.where(row == 1, before(1), pltpu.roll(cu, 2, axis=0)))
    cw = cw_ref[...]
    conv = cw[0:1] * cu_m2 + cw[1:2] * cu_m1 + cw[2:3] * cu
    yc_ref[0] = (_silu(proj(7)) * proj(4) * conv).astype(bf16)

    for sg in range(n_seg):
        last = cu[(sg + 1) * seg - 2:(sg + 1) * seg]
        carry_ref[sg] = last
        clast_ref[0, sg] = last


def _in_proj(x, conv_init, norm_g, w_bf, conv_w, tm, row_offset=0, transpose_v=0):
    b, l, d = x.shape
    w_all = w_bf.shape[1]
    wg = w_all // N_SPLIT
    assert wg == W_ATT
    nj = l // tm
    n_seg = conv_init.shape[1]
    assert n_seg == 1 or nj == 1
    row_spec = lambda width: pl.BlockSpec((1, tm, width), lambda i, j: (i, j, 0))
    const = lambda shape: pl.BlockSpec(shape, lambda i, j: (0,) * len(shape))
    state_spec = pl.BlockSpec((1, n_seg, CONV_W - 1, wg), lambda i, j: (i, 0, 0, 0))
    if row_offset:
        cache_spec = pl.BlockSpec(
            (None, pl.Element(tm * H_ATT), pl.Element(D_V)),
            lambda i, j: (i, _aligned((row_offset + j * tm) * H_ATT, 8 * H_ATT), 0))
    else:
        cache_spec = pl.BlockSpec((None, tm * H_ATT, D_V), lambda i, j: (i, j, 0))
    if transpose_v:
        assert tm % transpose_v == 0
        vb_spec = pl.BlockSpec((None, tm // transpose_v, wg, transpose_v),
                               lambda i, j: (i, j, 0, 0))
        vb_shape = jax.ShapeDtypeStruct((b, l // transpose_v, wg, transpose_v), bf16)
    else:
        vb_spec = pl.BlockSpec((None, tm, wg), lambda i, j: (i, j, 0))
        vb_shape = jax.ShapeDtypeStruct((b, l, wg), bf16)
    cache_shape = jax.ShapeDtypeStruct((b, (row_offset + l) * H_ATT, D_V), f32)
    out_shape = (
        jax.ShapeDtypeStruct((b, l, wg), bf16),
        jax.ShapeDtypeStruct((b, l, wg), bf16),
        cache_shape,
        jax.ShapeDtypeStruct((b, l, wg), bf16),
        cache_shape,
        vb_shape,
        jax.ShapeDtypeStruct((b, l, wg), f32),
        jax.ShapeDtypeStruct((b, l, wg), bf16),
        jax.ShapeDtypeStruct((b, n_seg, CONV_W - 1, wg), f32),
    )
    return pl.pallas_call(
        functools.partial(_in_proj_kernel, tm=tm, wg=wg, n_seg=n_seg, transpose_v=transpose_v),
        name="in_proj",
        grid=(b, nj),
        in_specs=[row_spec(d), state_spec, const((1, d)),
                  pl.BlockSpec((d, w_all), lambda i, j: (0, 0), pipeline_mode=pl.Buffered(1)),
                  const((CONV_W, wg))],
        out_specs=(row_spec(wg), row_spec(wg), cache_spec, row_spec(wg), cache_spec, vb_spec,
                   row_spec(wg), row_spec(wg), state_spec),
        out_shape=out_shape,
        scratch_shapes=[pltpu.VMEM((n_seg, CONV_W - 1, wg), f32)],
        compiler_params=pltpu.CompilerParams(
            dimension_semantics=("arbitrary", "arbitrary"), vmem_limit_bytes=VMEM_LIMIT),
    )(x, conv_init, norm_g.reshape(1, d), w_bf, conv_w)


def _fill_rows_kernel(km_ref, vm_ref, k_hbm, v_hbm, ko_ref, vo_ref):
    del k_hbm, v_hbm
    ko_ref[0] = km_ref[...]
    vo_ref[0] = vm_ref[...]


def _fill_meta_rows(k_meta, v_meta, k_all, v_all):
    b = k_all.shape[0]
    rows = k_meta.shape[0]
    small = pl.BlockSpec((rows, D_V), lambda i: (0, 0))
    lead = pl.BlockSpec((1, rows, D_V), lambda i: (i, 0, 0))
    whole = pl.BlockSpec(memory_space=pl.ANY)
    return pl.pallas_call(
        _fill_rows_kernel,
        name="fill_meta_rows",
        grid=(b,),
        in_specs=[small, small, whole, whole],
        out_specs=(lead, lead),
        out_shape=(jax.ShapeDtypeStruct(k_all.shape, k_all.dtype),
                   jax.ShapeDtypeStruct(v_all.shape, v_all.dtype)),
        input_output_aliases={2: 0, 3: 1},
    )(k_meta, v_meta, k_all, v_all)


def _rel_bucket(rel):
    nb = N_BUCKETS // 2
    ret = np.where(rel > 0, nb, 0)
    n = np.abs(rel)
    max_exact = nb // 2
    nf = np.maximum(n, 1).astype(np.float32)
    large = max_exact + (np.log(nf / np.float32(max_exact)) / np.float32(math.log(MAX_DIST / max_exact))
                         * np.float32(nb - max_exact)).astype(np.int32)
    large = np.minimum(large, nb - 1)
    return (ret + np.where(n < max_exact, n, large)).astype(np.int32)


def _chunk_id_np(pos):
    return np.where(pos < 0, -1, pos // CHUNK)


def _assert_far_past(q_pos, k_pos):
    rel = k_pos[None, :] - q_pos[:, None]
    nb = N_BUCKETS // 2
    max_exact = nb // 2
    n = np.abs(rel).astype(np.float64)
    large = max_exact + np.log(n / max_exact) / math.log(MAX_DIST / max_exact) * (nb - max_exact)
    assert np.all(rel < 0) and np.all(large >= nb), "tile is not in the saturated bucket"
    assert np.all(_chunk_id_np(k_pos)[None, :] <= _chunk_id_np(q_pos)[:, None])


def _bias_kernel(rb_ref, *refs):
    n = len(refs) // 3
    h = pl.program_id(0)
    far = rb_ref[FAR_BUCKET, h]
    shifted = [rb_ref[b, h] - far for b in range(N_BUCKETS)]
    for j in range(n):
        bucket = refs[2 * j][...]
        acc = jnp.zeros(bucket.shape, f32)
        for b in range(N_BUCKETS):
            acc = jnp.where(bucket == b, shifted[b], acc)
        refs[2 * n + j][0] = jnp.where(refs[2 * j + 1][...] != 0, acc * LOG2E, NEG)


def _build_biases(tiles, rel_bias):
    args, specs, out_specs, out_shapes = [], [], [], []
    for q_pos, k_pos, keys_major in tiles:
        rel = k_pos[None, :] - q_pos[:, None]
        vis = _chunk_id_np(k_pos)[None, :] <= _chunk_id_np(q_pos)[:, None]
        if keys_major:
            rel, vis = rel.T, vis.T
        r, c = rel.shape
        args += [jnp.asarray(_rel_bucket(rel)), jnp.asarray(vis.astype(np.int32))]
        specs += [pl.BlockSpec((r, c), lambda h: (0, 0))] * 2
        out_specs.append(pl.BlockSpec((1, r, c), lambda h: (h, 0, 0)))
        out_shapes.append(jax.ShapeDtypeStruct((H_ATT, r, c), f32))
    return pl.pallas_call(
        _bias_kernel,
        name="bias_tiles",
        grid=(H_ATT,),
        in_specs=[pl.BlockSpec(memory_space=pltpu.SMEM)] + specs,
        out_specs=tuple(out_specs),
        out_shape=tuple(out_shapes),
    )(rel_bias, *args)


def _aligned(x, m):
    return x if isinstance(x, int) else pl.multiple_of(x, m)


def _diff_lambda(lam_ref):
    lv = lam_ref[...]
    s1 = jnp.sum(lv[0:1] * lv[1:2], axis=-1, keepdims=True)
    s2 = jnp.sum(lv[2:3] * lv[3:4], axis=-1, keepdims=True)
    return jnp.exp(s1) - jnp.exp(s2) + LAM_INIT


def _gate_heads(o1, o2, lam, sg, gate):
    o = o1 - lam * o2
    return _rms(o, sg) * (1.0 - LAM_INIT) * gate


def _prompt_attn_kernel(*refs, t, n_sub, n_q):
    is_last = pl.program_id(1) == H_ATT - 1
    for k in range(n_q):
        at_k = pl.program_id(2) == k
        for last_head in (False, True):
            pl.when(at_k & (is_last if last_head else jnp.logical_not(is_last)))(
                functools.partial(_attn_tile, *refs, i=k, last_head=last_head, t=t, n_sub=n_sub,
                                  n_q=n_q))


def _attn_tile(qa_ref, qb_ref, k_ref, vt_ref, km_ref, vmt_ref, bn_ref, bm_ref, gate_ref,
               sg_ref, lam_ref, x_ref, yc_ref, wo_ref, fg_ref, y_ref,
               sa_sc, mx_sc, m_sc, l_sc, acc_sc, og_sc, *, i, last_head, t, n_sub, n_q):
    t2 = 2 * t
    n_wide = i * (n_sub // 2)
    slots = [(c, p) for c in range(2) for p in range(n_sub)]

    def keys(start, rows):
        return k_ref[0, pl.ds(_aligned(start, t), rows), :]

    def cols(p):
        return slice(p * t, (p + 1) * t)

    def qk(kt, j, c, p):
        rows = pl.ds(_aligned((j * n_sub + p) * t, t), t)
        return lax.dot_general(kt, (qa_ref, qb_ref)[c][0, rows, :], (((1,), (1,)), ((), ())),
                               preferred_element_type=f32)

    def col_max(*parts):
        m = jnp.max(parts[0], axis=0, keepdims=True)
        for part in parts[1:]:
            m = jnp.maximum(m, jnp.max(part, axis=0, keepdims=True))
        return m

    def update(c, p, s_parts, vt_parts):
        m_prev = m_sc[c, :, cols(p)]
        m_new = jnp.maximum(m_prev, mx_sc[c, :, cols(p)])
        alpha = jnp.exp2(m_prev - m_new)
        e_sum = pv = None
        for s, vt in zip(s_parts, vt_parts):
            e = jnp.exp2(s - m_new)
            e_k = jnp.sum(e, axis=0, keepdims=True)
            pv_k = jnp.dot(vt, e.astype(bf16), preferred_element_type=f32)
            e_sum = e_k if e_sum is None else e_sum + e_k
            pv = pv_k if pv is None else pv + pv_k
        l_sc[c, :, cols(p)] = alpha * l_sc[c, :, cols(p)] + e_sum
        acc_sc[c, :, cols(p)] = alpha * acc_sc[c, :, cols(p)] + pv
        m_sc[c, :, cols(p)] = m_new

    def near_bias(s, kind):
        if kind == "both":
            return s + bn_ref[0]
        if kind == "own":
            return s + bn_ref[0, t:t2]
        return jnp.concatenate([s[:t], s[t:] + bn_ref[0, 0:t]], axis=0)

    def far_refill(kt, j, c, p, bias=None):
        s = qk(kt, j, c, p)
        if bias is not None:
            s = near_bias(s, bias)
        sa_sc[c, 0:t2, cols(p)] = s
        mx_sc[c, :, cols(p)] = col_max(s)

    def far_update(c, p, vt):
        update(c, p, [sa_sc[c, 0:t2, cols(p)]], [vt])

    def far_tile(w, refill):
        vt = vt_ref[0, w]
        for c, p in slots:
            far_update(c, p, vt)
            refill(c, p)

    def own_a_rows(p):
        return t if p == 0 else t2

    def own_a_refill(j, c, p, meta_bias=None):
        rows = own_a_rows(p)
        s = qk(keys(n_wide * t2, rows), j, c, p)
        if p < 3:
            s = near_bias(s, ("own", "both", "before")[p])
        sm = qk(km_ref[...], j, c, p)
        if meta_bias is not None:
            sm = sm + meta_bias
        sa_sc[c, 0:rows, cols(p)] = s
        sa_sc[c, t2:t2 + N_META, cols(p)] = sm
        mx_sc[c, :, cols(p)] = col_max(s, sm)

    def own_a_update(c, p):
        rows = own_a_rows(p)
        update(c, p, [sa_sc[c, 0:rows, cols(p)], sa_sc[c, t2:t2 + N_META, cols(p)]],
               [vt_ref[0, n_wide, :, 0:rows], vmt_ref[...]])

    def own_b_rows(p):
        return t if p == 2 else t2

    def own_b_refill(j, c, p):
        rows = own_b_rows(p)
        s = near_bias(qk(keys((n_wide + 1) * t2, rows), j, c, p), ("own", "both")[p - 2])
        sa_sc[c, 0:rows, cols(p)] = s
        mx_sc[c, :, cols(p)] = col_max(s)

    def own_b_update(c, p):
        rows = own_b_rows(p)
        update(c, p, [sa_sc[c, 0:rows, cols(p)]], [vt_ref[0, n_wide + 1, :, 0:rows]])

    m_sc[...] = jnp.full(m_sc.shape, NEG, f32)
    l_sc[...] = jnp.zeros(l_sc.shape, f32)
    acc_sc[...] = jnp.zeros(acc_sc.shape, f32)

    def far_pair(pp):
        w = 2 * pp
        kt1, kt2 = keys((w + 1) * t2, t2), keys((w + 2) * t2, t2)
        far_tile(w, lambda c, p: far_refill(kt1, i, c, p))
        far_tile(w + 1, lambda c, p: far_refill(kt2, i, c, p))

    def own_keys_and_finish(look_ahead):
        kt0 = keys(0, t2)
        for c, p in slots:
            own_a_update(c, p)
            if p >= 2:
                own_b_refill(i, c, p)
            elif look_ahead:
                far_refill(kt0, i + 1, c, p)
        for c, p in slots:
            if p >= 2:
                own_b_update(c, p)
                if look_ahead:
                    far_refill(kt0, i + 1, c, p)
        o = acc_sc[0] / l_sc[0] - _diff_lambda(lam_ref) * (acc_sc[1] / l_sc[1])
        o = o * lax.rsqrt(jnp.mean(o * o, axis=0, keepdims=True) + EPS)
        og = (o.T * (sg_ref[...] * (1.0 - LAM_INIT)) * gate_ref[0]).astype(bf16)
        if not last_head:
            rows = pl.ds(i * (n_sub * t), n_sub * t)
            og_sc[pl.program_id(1), rows, :] = og
        else:
            for p in range(n_sub):
                sub = pl.ds((i * n_sub + p) * t, t)
                heads = [og_sc[hd, sub, :] for hd in range(H_ATT - 1)] + [og[cols(p)]]
                y_ref[0, cols(p), :] = _merge(jnp.concatenate(heads, axis=1), yc_ref[0, cols(p), :],
                                              x_ref[0, cols(p), :], wo_ref, fg_ref[...])

    if i == 0:
        for c, p in slots:
            own_a_refill(i, c, p, bm_ref[0] if p == 0 else None)
    else:
        for pp in range(i - 1):
            far_pair(pp)
        kt1 = keys((n_wide - 1) * t2, t2)
        far_tile(n_wide - 2,
                 lambda c, p: far_refill(kt1, i, c, p, "before" if p == 0 else None))
        far_tile(n_wide - 1, lambda c, p: own_a_refill(i, c, p))
    own_keys_and_finish(look_ahead=i < n_q - 1)


def _prompt_attn(qa, qb, kb, vt, km, vmt, bias_near, bias_meta, gate, subln_g, lam_vecs,
                 x, yc, wo_bf, final_g):
    b, l, _ = qa.shape
    t = ATT_TILE
    tq = ATT_SUB * t
    assert ATT_SUB == 4 and l % tq == 0
    assert vt.shape == (b, l // (2 * t), W_ATT, 2 * t)
    tile_spec = pl.BlockSpec((1, tq, D_V), lambda bi, h, i: (bi, i, h))
    seq_spec = pl.BlockSpec((1, l, D_V), lambda bi, h, i: (bi, 0, h))
    vt_spec = pl.BlockSpec((1, l // (2 * t), D_V, 2 * t), lambda bi, h, i: (bi, 0, h, 0))
    per_head = lambda rows: pl.BlockSpec((1, rows, t), lambda bi, h, i: (h, 0, 0))
    d = x.shape[2]
    last_head = lambda width: pl.BlockSpec(
        (1, tq, width), lambda bi, h, i: (bi, jnp.where(h == H_ATT - 1, i, 0), 0))
    return pl.pallas_call(
        functools.partial(_prompt_attn_kernel, t=t, n_sub=ATT_SUB, n_q=l // tq),
        name="prompt_attn",
        grid=(b, H_ATT, l // tq),
        in_specs=[
            seq_spec, seq_spec, seq_spec, vt_spec,
            pl.BlockSpec((N_META, D_V), lambda bi, h, i: (0, h)),
            pl.BlockSpec((D_V, N_META), lambda bi, h, i: (h, 0)),
            per_head(2 * t), per_head(N_META),
            tile_spec,
            pl.BlockSpec((1, D_V), lambda bi, h, i: (0, 0)),
            pl.BlockSpec((4, D_QK), lambda bi, h, i: (0, 0)),
            last_head(d), last_head(yc.shape[2]),
            pl.BlockSpec(wo_bf.shape, lambda bi, h, i: (0, 0), pipeline_mode=pl.Buffered(1)),
            pl.BlockSpec((1, d), lambda bi, h, i: (0, 0)),
        ],
        out_specs=last_head(d),
        out_shape=jax.ShapeDtypeStruct((b, l, d), f32),
        scratch_shapes=[pltpu.VMEM((2, 2 * t + N_META, tq), f32),
                        pltpu.VMEM((2, 1, tq), f32),
                        pltpu.VMEM((2, 1, tq), f32), pltpu.VMEM((2, 1, tq), f32),
                        pltpu.VMEM((2, D_V, tq), f32),
                        pltpu.VMEM((H_ATT - 1, l, D_V), bf16)],
        compiler_params=pltpu.CompilerParams(
            dimension_semantics=("arbitrary", "arbitrary", "arbitrary"),
            vmem_limit_bytes=VMEM_LIMIT),
    )(qa, qb, kb, vt, km, vmt, bias_near, bias_meta, gate, subln_g, lam_vecs,
      x, yc, wo_bf, final_g.reshape(1, d))


def _merge(og, yc, x, wo_ref, fg):
    y = x + jnp.dot(jnp.concatenate([og, yc], axis=1), wo_ref[...], preferred_element_type=f32)
    return _rms(y, fg)


def _small_attn_kernel(*refs, has_cache):
    if has_cache:
        (qa_ref, qb_ref, kn_ref, vn_ref, bn_ref, ck_ref, cv_ref, bc_ref,
         gate_ref, yc_ref, x_ref, sg_ref, lam_ref, wo_ref, fg_ref, y_ref) = refs
    else:
        (qa_ref, qb_ref, kn_ref, vn_ref, bn_ref,
         gate_ref, yc_ref, x_ref, sg_ref, lam_ref, wo_ref, fg_ref, y_ref) = refs
    lq = qa_ref.shape[1]
    lam = _diff_lambda(lam_ref)
    nt = (((1,), (1,)), ((), ()))
    both = lambda a: jnp.concatenate([a, a], axis=0)
    heads = []
    for h in range(H_ATT):
        cols = slice(h * D_V, (h + 1) * D_V)
        q2 = jnp.concatenate([qa_ref[0][:, cols], qb_ref[0][:, cols]], axis=0)
        kn = kn_ref[0][:, cols]
        vn = vn_ref[0][:, cols]
        sn = lax.dot_general(q2, kn, nt, preferred_element_type=f32) + both(bn_ref[h])
        m = jnp.max(sn, axis=-1, keepdims=True)
        if has_cache:
            n_rows = ck_ref.shape[1] // H_ATT
            ck = ck_ref[0, pl.ds(h, n_rows, stride=H_ATT), :].astype(bf16)
            cv = cv_ref[0, pl.ds(h, n_rows, stride=H_ATT), :].astype(bf16)
            sc = lax.dot_general(q2, ck, nt, preferred_element_type=f32) + both(bc_ref[h])
            m = jnp.maximum(m, jnp.max(sc, axis=-1, keepdims=True))
        en = jnp.exp2(sn - m)
        l = jnp.sum(en, axis=-1, keepdims=True)
        acc = jnp.dot(en.astype(bf16), vn, preferred_element_type=f32)
        if has_cache:
            ec = jnp.exp2(sc - m)
            l = l + jnp.sum(ec, axis=-1, keepdims=True)
            acc = acc + jnp.dot(ec.astype(bf16), cv, preferred_element_type=f32)
        o = acc / l
        heads.append(_gate_heads(o[:lq], o[lq:], lam, sg_ref[...], gate_ref[0][:, cols]))
    og = jnp.concatenate(heads, axis=-1).astype(bf16)
    y_ref[0] = _merge(og, yc_ref[0], x_ref[0], wo_ref, fg_ref[...])


def _small_attn(qa, qb, kb, vb, bias_new, gate, yc, x, subln_g, lam_vecs, wo_bf, final_g,
                cache_k=None, cache_v=None, bias_cache=None):
    b, lq, d = x.shape
    has_cache = cache_k is not None
    per_b = lambda rows, width: pl.BlockSpec((1, rows, width), lambda i: (i, 0, 0))
    const = lambda shape: pl.BlockSpec(shape, lambda i: (0,) * len(shape))
    w = qa.shape[2]
    args = [qa, qb, kb, vb, bias_new]
    specs = [per_b(lq, w)] * 4 + [const(bias_new.shape)]
    if has_cache:
        args += [cache_k, cache_v, bias_cache]
        specs += [per_b(*cache_k.shape[1:]), per_b(*cache_v.shape[1:]), const(bias_cache.shape)]
    args += [gate, yc, x, subln_g, lam_vecs, wo_bf, final_g.reshape(1, d)]
    specs += [per_b(lq, w), per_b(lq, yc.shape[2]), per_b(lq, d), const(subln_g.shape),
              const(lam_vecs.shape), const(wo_bf.shape), const((1, d))]
    return pl.pallas_call(
        functools.partial(_small_attn_kernel, has_cache=has_cache),
        name="small_attn",
        grid=(b,),
        in_specs=specs,
        out_specs=per_b(lq, d),
        out_shape=jax.ShapeDtypeStruct((b, lq, d), f32),
        compiler_params=pltpu.CompilerParams(
            dimension_semantics=("arbitrary",), vmem_limit_bytes=VMEM_LIMIT),
    )(*args)


def kernel(x_prompt, x_sample, cache_k, cache_v, state_conv, meta_tokens, rel_bias, norm_g, w_in,
           conv_w, lambda_q1, lambda_k1, lambda_q2, lambda_k2, subln_g, w_out, final_g):
    bp, sp, d = x_prompt.shape
    bd, sd, _ = x_sample.shape
    depth = w_in.shape[0]
    assert depth == 1, "single-layer step only"
    past = cache_k.shape[2] - N_META
    t = ATT_TILE
    assert sp % (ATT_SUB * t) == 0 and t % CHUNK == 0

    w_bf = w_in[0].astype(bf16)
    wo_bf = w_out[0].astype(bf16)
    lam_vecs = jnp.stack([lambda_q1[0], lambda_k1[0], lambda_q2[0], lambda_k2[0]])
    sg = subln_g[0].reshape(1, D_V)
    wg = w_bf.shape[1] // N_SPLIT

    zeros_state = jnp.zeros((1, 1, CONV_W - 1, wg), f32)
    (qma, qmb, km, kmb, vm, vmb, gm, ycm, cm) = _in_proj(
        meta_tokens[None], zeros_state, norm_g[0], w_bf, conv_w[0], tm=N_META)
    (qpa, qpb, kp, kpb, vp, vpt, gp, ycp, cp) = _in_proj(
        x_prompt, jnp.broadcast_to(cm, (bp, 1, CONV_W - 1, wg)), norm_g[0], w_bf, conv_w[0],
        tm=PROJ_ROWS, row_offset=N_META, transpose_v=2 * t)
    kp, vp = _fill_meta_rows(km[0], vm[0], kp, vp)
    per_stream = lambda a: a.reshape(bd, sd, a.shape[-1])
    (qda, qdb, kd, kdb, vd, vdb, gd, ycd, cd) = _in_proj(
        x_sample.reshape(1, bd * sd, d), state_conv[0][None], norm_g[0], w_bf, conv_w[0],
        tm=bd * sd)
    qda, qdb, kdb, vdb, gd, ycd = map(per_stream, (qda, qdb, kdb, vdb, gd, ycd))

    meta_pos = np.arange(-N_META, 0)
    tile_pos = np.arange(t)
    d_qpos = past + np.arange(sd)
    cache_pos = np.concatenate([meta_pos, np.arange(past)])
    (bias_near, bias_meta, bias_meta_self, bias_dec_new, bias_dec_cache) = (
        _build_biases([
            (tile_pos + t, np.arange(0, 2 * t), True),
            (tile_pos, meta_pos, True),
            (meta_pos, meta_pos, False),
            (d_qpos, d_qpos, False),
            (d_qpos, cache_pos, False),
        ], rel_bias))
    for off in range(2, sp // t):
        _assert_far_past(tile_pos + off * t, tile_pos)
    for off in range(1, sp // t):
        _assert_far_past(tile_pos + off * t, meta_pos)

    y_prompt = _prompt_attn(qpa, qpb, kpb, vpt, kmb[0], vmb[0].T, bias_near, bias_meta, gp, sg,
                            lam_vecs, x_prompt, ycp, wo_bf, final_g)

    _small_attn(qma, qmb, kmb, vmb, bias_meta_self, gm, ycm, meta_tokens[None], sg, lam_vecs,
                wo_bf, final_g)

    y_sample = _small_attn(
        qda, qdb, kdb, vdb, bias_dec_new, gd, ycd, x_sample, sg, lam_vecs, wo_bf, final_g,
        cache_k=cache_k[0].reshape(bd, (N_META + past) * H_ATT, D_V),
        cache_v=cache_v[0].reshape(bd, (N_META + past) * H_ATT, D_V),
        bias_cache=bias_dec_cache)

    return (y_prompt, y_sample,
            kp.reshape(1, bp, N_META + sp, H_ATT, D_V),
            vp.reshape(1, bp, N_META + sp, H_ATT, D_V),
            cp.reshape(1, bp, CONV_W - 1, wg),
            kd.reshape(1, bd, sd, H_ATT, D_V),
            vd.reshape(1, bd, sd, H_ATT, D_V),
            cd)
```

```python
import functools
import math

import numpy as np
import jax
import jax.numpy as jnp
from jax import lax
from jax.experimental import pallas as pl
from jax.experimental.pallas import tpu as pltpu

f32 = jnp.float32
bf16 = jnp.bfloat16

CHUNK = 64
N_META = 16
H_ATT = 4
D_QK = 64
D_V = 2 * D_QK
W_ATT = H_ATT * D_V
CONV_W = 3
N_SPLIT = 8
N_BUCKETS = 32
MAX_DIST = 128
EPS = 1e-6
NEG = -1e30
SCALE = D_QK ** -0.5
LOG2E = math.log2(math.e)
LAM_INIT = 0.8 - 0.6 * math.exp(-0.3 * 0)
FAR_BUCKET = N_BUCKETS // 2 - 1

ATT_TILE = 256
ATT_SUB = 4
VMEM_LIMIT = 48 * 1024 * 1024
PROJ_ROWS = 1024


def _silu(z):
    return z * (1.0 / (1.0 + jnp.exp(-z)))


def _rms(x, g):
    return x * lax.rsqrt(jnp.mean(x * x, axis=-1, keepdims=True) + EPS) * g


def _in_proj_kernel(x_ref, cinit_ref, g_ref, w_ref, cw_ref,
                    qa_ref, qb_ref, k_ref, kb_ref, v_ref, vb_ref, gate_ref, yc_ref, clast_ref,
                    carry_ref, *, tm, wg, n_seg, transpose_v):
    j = pl.program_id(1)

    @pl.when(j == 0)
    def _():
        carry_ref[...] = cinit_ref[0]

    x = x_ref[0]
    h = _rms(x, g_ref[...]).astype(bf16)

    def proj(g):
        return jnp.dot(h, w_ref[:, g * wg:(g + 1) * wg], preferred_element_type=f32)

    def store_heads(ref, a):
        for hd in range(H_ATT):
            ref[pl.ds(hd, tm, stride=H_ATT), :] = a[:, hd * D_V:(hd + 1) * D_V]

    q = (proj(0) * (SCALE * LOG2E)).astype(bf16)
    first_half = lax.broadcasted_iota(jnp.int32, q.shape, 1) % D_V < D_QK
    zero = jnp.zeros_like(q)
    qa_ref[0] = jnp.where(first_half, q, zero)
    qb_ref[0] = jnp.where(first_half, zero, q)
    k = proj(1)
    store_heads(k_ref, k)
    kb_ref[0] = k.astype(bf16)
    v = proj(2)
    store_heads(v_ref, v)
    if transpose_v:
        for ch in range(tm // transpose_v):
            vb_ref[ch] = v[ch * transpose_v:(ch + 1) * transpose_v].T.astype(bf16)
    else:
        vb_ref[...] = v.astype(bf16)
    gate_ref[0] = _silu(proj(3))

    cu = proj(5) * proj(6)
    prev = carry_ref[...]
    seg = tm // n_seg
    row = lax.broadcasted_iota(jnp.int32, cu.shape, 0) % seg

    def before(k):
        if n_seg == 1:
            return prev[0, k:k + 1]
        return jnp.concatenate(
            [jnp.broadcast_to(prev[sg, k:k + 1], (seg, wg)) for sg in range(n_seg)], axis=0)

    cu_m1 = jnp.where(row == 0, before(1), pltpu.roll(cu, 1, axis=0))
    cu_m2 = jnp.where(row == 0, before(0),
                      jnp.where(row == 1, before(1), pltpu.roll(cu, 2, axis=0)))
    cw = cw_ref[...]
    conv = cw[0:1] * cu_m2 + cw[1:2] * cu_m1 + cw[2:3] * cu
    yc_ref[0] = (_silu(proj(7)) * proj(4) * conv).astype(bf16)

    for sg in range(n_seg):
        last = cu[(sg + 1) * seg - 2:(sg + 1) * seg]
        carry_ref[sg] = last
        clast_ref[0, sg] = last


def _in_proj(x, conv_init, norm_g, w_bf, conv_w, tm, row_offset=0, transpose_v=0):
    b, l, d = x.shape
    w_all = w_bf.shape[1]
    wg = w_all // N_SPLIT
    assert wg == W_ATT
    nj = l // tm
    n_seg = conv_init.shape[1]
    assert n_seg == 1 or nj == 1
    row_spec = lambda width: pl.BlockSpec((1, tm, width), lambda i, j: (i, j, 0))
    const = lambda shape: pl.BlockSpec(shape, lambda i, j: (0,) * len(shape))
    state_spec = pl.BlockSpec((1, n_seg, CONV_W - 1, wg), lambda i, j: (i, 0, 0, 0))
    if row_offset:
        cache_spec = pl.BlockSpec(
            (None, pl.Element(tm * H_ATT), pl.Element(D_V)),
            lambda i, j: (i, _aligned((row_offset + j * tm) * H_ATT, 8 * H_ATT), 0))
    else:
        cache_spec = pl.BlockSpec((None, tm * H_ATT, D_V), lambda i, j: (i, j, 0))
    if transpose_v:
        assert tm % transpose_v == 0
        vb_spec = pl.BlockSpec((None, tm // transpose_v, wg, transpose_v),
                               lambda i, j: (i, j, 0, 0))
        vb_shape = jax.ShapeDtypeStruct((b, l // transpose_v, wg, transpose_v), bf16)
    else:
        vb_spec = pl.BlockSpec((None, tm, wg), lambda i, j: (i, j, 0))
        vb_shape = jax.ShapeDtypeStruct((b, l, wg), bf16)
    cache_shape = jax.ShapeDtypeStruct((b, (row_offset + l) * H_ATT, D_V), f32)
    out_shape = (
        jax.ShapeDtypeStruct((b, l, wg), bf16),
        jax.ShapeDtypeStruct((b, l, wg), bf16),
        cache_shape,
        jax.ShapeDtypeStruct((b, l, wg), bf16),
        cache_shape,
        vb_shape,
        jax.ShapeDtypeStruct((b, l, wg), f32),
        jax.ShapeDtypeStruct((b, l, wg), bf16),
        jax.ShapeDtypeStruct((b, n_seg, CONV_W - 1, wg), f32),
    )
    return pl.pallas_call(
        functools.partial(_in_proj_kernel, tm=tm, wg=wg, n_seg=n_seg, transpose_v=transpose_v),
        name="in_proj",
        grid=(b, nj),
        in_specs=[row_spec(d), state_spec, const((1, d)),
                  pl.BlockSpec((d, w_all), lambda i, j: (0, 0), pipeline_mode=pl.Buffered(1)),
                  const((CONV_W, wg))],
        out_specs=(row_spec(wg), row_spec(wg), cache_spec, row_spec(wg), cache_spec, vb_spec,
                   row_spec(wg), row_spec(wg), state_spec),
        out_shape=out_shape,
        scratch_shapes=[pltpu.VMEM((n_seg, CONV_W - 1, wg), f32)],
        compiler_params=pltpu.CompilerParams(
            dimension_semantics=("arbitrary", "arbitrary"), vmem_limit_bytes=VMEM_LIMIT),
    )(x, conv_init, norm_g.reshape(1, d), w_bf, conv_w)


def _fill_rows_kernel(km_ref, vm_ref, k_hbm, v_hbm, ko_ref, vo_ref):
    del k_hbm, v_hbm
    ko_ref[0] = km_ref[...]
    vo_ref[0] = vm_ref[...]


def _fill_meta_rows(k_meta, v_meta, k_all, v_all):
    b = k_all.shape[0]
    rows = k_meta.shape[0]
    small = pl.BlockSpec((rows, D_V), lambda i: (0, 0))
    lead = pl.BlockSpec((1, rows, D_V), lambda i: (i, 0, 0))
    whole = pl.BlockSpec(memory_space=pl.ANY)
    return pl.pallas_call(
        _fill_rows_kernel,
        name="fill_meta_rows",
        grid=(b,),
        in_specs=[small, small, whole, whole],
        out_specs=(lead, lead),
        out_shape=(jax.ShapeDtypeStruct(k_all.shape, k_all.dtype),
                   jax.ShapeDtypeStruct(v_all.shape, v_all.dtype)),
        input_output_aliases={2: 0, 3: 1},
    )(k_meta, v_meta, k_all, v_all)


def _rel_bucket(rel):
    nb = N_BUCKETS // 2
    ret = np.where(rel > 0, nb, 0)
    n = np.abs(rel)
    max_exact = nb // 2
    nf = np.maximum(n, 1).astype(np.float32)
    large = max_exact + (np.log(nf / np.float32(max_exact)) / np.float32(math.log(MAX_DIST / max_exact))
                         * np.float32(nb - max_exact)).astype(np.int32)
    large = np.minimum(large, nb - 1)
    return (ret + np.where(n < max_exact, n, large)).astype(np.int32)


def _chunk_id_np(pos):
    return np.where(pos < 0, -1, pos // CHUNK)


def _assert_far_past(q_pos, k_pos):
    rel = k_pos[None, :] - q_pos[:, None]
    nb = N_BUCKETS // 2
    max_exact = nb // 2
    n = np.abs(rel).astype(np.float64)
    large = max_exact + np.log(n / max_exact) / math.log(MAX_DIST / max_exact) * (nb - max_exact)
    assert np.all(rel < 0) and np.all(large >= nb), "tile is not in the saturated bucket"
    assert np.all(_chunk_id_np(k_pos)[None, :] <= _chunk_id_np(q_pos)[:, None])


def _bias_kernel(rb_ref, *refs):
    n = len(refs) // 3
    h = pl.program_id(0)
    far = rb_ref[FAR_BUCKET, h]
    shifted = [rb_ref[b, h] - far for b in range(N_BUCKETS)]
    for j in range(n):
        bucket = refs[2 * j][...]
        acc = jnp.zeros(bucket.shape, f32)
        for b in range(N_BUCKETS):
            acc = jnp.where(bucket == b, shifted[b], acc)
        refs[2 * n + j][0] = jnp.where(refs[2 * j + 1][...] != 0, acc * LOG2E, NEG)


def _build_biases(tiles, rel_bias):
    args, specs, out_specs, out_shapes = [], [], [], []
    for q_pos, k_pos, keys_major in tiles:
        rel = k_pos[None, :] - q_pos[:, None]
        vis = _chunk_id_np(k_pos)[None, :] <= _chunk_id_np(q_pos)[:, None]
        if keys_major:
            rel, vis = rel.T, vis.T
        r, c = rel.shape
        args += [jnp.asarray(_rel_bucket(rel)), jnp.asarray(vis.astype(np.int32))]
        specs += [pl.BlockSpec((r, c), lambda h: (0, 0))] * 2
        out_specs.append(pl.BlockSpec((1, r, c), lambda h: (h, 0, 0)))
        out_shapes.append(jax.ShapeDtypeStruct((H_ATT, r, c), f32))
    return pl.pallas_call(
        _bias_kernel,
        name="bias_tiles",
        grid=(H_ATT,),
        in_specs=[pl.BlockSpec(memory_space=pltpu.SMEM)] + specs,
        out_specs=tuple(out_specs),
        out_shape=tuple(out_shapes),
    )(rel_bias, *args)


def _aligned(x, m):
    return x if isinstance(x, int) else pl.multiple_of(x, m)


def _diff_lambda(lam_ref):
    lv = lam_ref[...]
    s1 = jnp.sum(lv[0:1] * lv[1:2], axis=-1, keepdims=True)
    s2 = jnp.sum(lv[2:3] * lv[3:4], axis=-1, keepdims=True)
    return jnp.exp(s1) - jnp.exp(s2) + LAM_INIT


def _gate_heads(o1, o2, lam, sg, gate):
    o = o1 - lam * o2
    return _rms(o, sg) * (1.0 - LAM_INIT) * gate


def _prompt_attn_kernel(*refs, t, n_sub, n_q):
    for k in range(n_q):
        pl.when(pl.program_id(2) == k)(
            functools.partial(_attn_tile, *refs, i=k, t=t, n_sub=n_sub, n_q=n_q))

    x_ref, yc_ref, wo_ref, fg_ref, y_ref = refs[11:16]
    og_sc = refs[-1]

    @pl.when(pl.program_id(1) == H_ATT - 1)
    def _():
        for p in range(n_sub):
            sub = pl.ds(pl.multiple_of((pl.program_id(2) * n_sub + p) * t, t), t)
            heads = [og_sc[hd, sub, :] for hd in range(H_ATT)]
            y_ref[0, p * t:(p + 1) * t, :] = _merge(
                jnp.concatenate(heads, axis=1), yc_ref[0, p * t:(p + 1) * t, :],
                x_ref[0, p * t:(p + 1) * t, :], wo_ref, fg_ref[...])


def _attn_tile(qa_ref, qb_ref, k_ref, vt_ref, km_ref, vmt_ref, bn_ref, bm_ref, gate_ref,
               sg_ref, lam_ref, x_ref, yc_ref, wo_ref, fg_ref, y_ref,
               sa_sc, mx_sc, m_sc, l_sc, acc_sc, og_sc, *, i, t, n_sub, n_q):
    t2 = 2 * t
    n_wide = i * (n_sub // 2)
    slots = [(c, p) for c in range(2) for p in range(n_sub)]

    def keys(start, rows):
        return k_ref[0, pl.ds(_aligned(start, t), rows), :]

    def cols(p):
        return slice(p * t, (p + 1) * t)

    def qk(kt, j, c, p):
        rows = pl.ds(_aligned((j * n_sub + p) * t, t), t)
        return lax.dot_general(kt, (qa_ref, qb_ref)[c][0, rows, :], (((1,), (1,)), ((), ())),
                               preferred_element_type=f32)

    def col_max(*parts):
        m = jnp.max(parts[0], axis=0, keepdims=True)
        for part in parts[1:]:
            m = jnp.maximum(m, jnp.max(part, axis=0, keepdims=True))
        return m

    def update(c, p, s_parts, vt_parts):
        m_prev = m_sc[c, :, cols(p)]
        m_new = jnp.maximum(m_prev, mx_sc[c, :, cols(p)])
        alpha = jnp.exp2(m_prev - m_new)
        e_sum = pv = None
        for s, vt in zip(s_parts, vt_parts):
            e = jnp.exp2(s - m_new)
            e_k = jnp.sum(e, axis=0, keepdims=True)
            pv_k = jnp.dot(vt, e.astype(bf16), preferred_element_type=f32)
            e_sum = e_k if e_sum is None else e_sum + e_k
            pv = pv_k if pv is None else pv + pv_k
        l_sc[c, :, cols(p)] = alpha * l_sc[c, :, cols(p)] + e_sum
        acc_sc[c, :, cols(p)] = alpha * acc_sc[c, :, cols(p)] + pv
        m_sc[c, :, cols(p)] = m_new

    def near_bias(s, kind):
        if kind == "both":
            return s + bn_ref[0]
        if kind == "own":
            return s + bn_ref[0, t:t2]
        return jnp.concatenate([s[:t], s[t:] + bn_ref[0, 0:t]], axis=0)

    def far_refill(kt, j, c, p, bias=None):
        s = qk(kt, j, c, p)
        if bias is not None:
            s = near_bias(s, bias)
        sa_sc[c, 0:t2, cols(p)] = s
        mx_sc[c, :, cols(p)] = col_max(s)

    def far_update(c, p, vt):
        update(c, p, [sa_sc[c, 0:t2, cols(p)]], [vt])

    def far_tile(w, refill):
        vt = vt_ref[0, w]
        for c, p in slots:
            far_update(c, p, vt)
            refill(c, p)

    def own_a_rows(p):
        return t if p == 0 else t2

    def own_a_refill(j, c, p, meta_bias=None):
        rows = own_a_rows(p)
        s = qk(keys(n_wide * t2, rows), j, c, p)
        if p < 3:
            s = near_bias(s, ("own", "both", "before")[p])
        sm = qk(km_ref[...], j, c, p)
        if meta_bias is not None:
            sm = sm + meta_bias
        sa_sc[c, 0:rows, cols(p)] = s
        sa_sc[c, t2:t2 + N_META, cols(p)] = sm
        mx_sc[c, :, cols(p)] = col_max(s, sm)

    def own_a_update(c, p):
        rows = own_a_rows(p)
        update(c, p, [sa_sc[c, 0:rows, cols(p)], sa_sc[c, t2:t2 + N_META, cols(p)]],
               [vt_ref[0, n_wide, :, 0:rows], vmt_ref[...]])

    def own_b_rows(p):
        return t if p == 2 else t2

    def own_b_refill(j, c, p):
        rows = own_b_rows(p)
        s = near_bias(qk(keys((n_wide + 1) * t2, rows), j, c, p), ("own", "both")[p - 2])
        sa_sc[c, 0:rows, cols(p)] = s
        mx_sc[c, :, cols(p)] = col_max(s)

    def own_b_update(c, p):
        rows = own_b_rows(p)
        update(c, p, [sa_sc[c, 0:rows, cols(p)]], [vt_ref[0, n_wide + 1, :, 0:rows]])

    m_sc[...] = jnp.full(m_sc.shape, NEG, f32)
    l_sc[...] = jnp.zeros(l_sc.shape, f32)
    acc_sc[...] = jnp.zeros(acc_sc.shape, f32)

    def far_pair(pp):
        w = 2 * pp
        kt1, kt2 = keys((w + 1) * t2, t2), keys((w + 2) * t2, t2)
        far_tile(w, lambda c, p: far_refill(kt1, i, c, p))
        far_tile(w + 1, lambda c, p: far_refill(kt2, i, c, p))

    def own_keys_and_finish(look_ahead):
        kt0 = keys(0, t2)
        for c, p in slots:
            own_a_update(c, p)
            if p >= 2:
                own_b_refill(i, c, p)
            elif look_ahead:
                far_refill(kt0, i + 1, c, p)
        for c, p in slots:
            if p >= 2:
                own_b_update(c, p)
                if look_ahead:
                    far_refill(kt0, i + 1, c, p)
        o = acc_sc[0] / l_sc[0] - _diff_lambda(lam_ref) * (acc_sc[1] / l_sc[1])
        o = o * lax.rsqrt(jnp.mean(o * o, axis=0, keepdims=True) + EPS)
        og = (o.T * (sg_ref[...] * (1.0 - LAM_INIT)) * gate_ref[0]).astype(bf16)
        og_sc[pl.program_id(1), pl.ds(i * (n_sub * t), n_sub * t), :] = og

    if i == 0:
        for c, p in slots:
            own_a_refill(i, c, p, bm_ref[0] if p == 0 else None)
    else:
        for pp in range(i - 1):
            far_pair(pp)
        kt1 = keys((n_wide - 1) * t2, t2)
        far_tile(n_wide - 2,
                 lambda c, p: far_refill(kt1, i, c, p, "before" if p == 0 else None))
        far_tile(n_wide - 1, lambda c, p: own_a_refill(i, c, p))
    own_keys_and_finish(look_ahead=i < n_q - 1)


def _prompt_attn(qa, qb, kb, vt, km, vmt, bias_near, bias_meta, gate, subln_g, lam_vecs,
                 x, yc, wo_bf, final_g):
    b, l, _ = qa.shape
    t = ATT_TILE
    tq = ATT_SUB * t
    assert ATT_SUB == 4 and l % tq == 0
    assert vt.shape == (b, l // (2 * t), W_ATT, 2 * t)
    tile_spec = pl.BlockSpec((1, tq, D_V), lambda bi, h, i: (bi, i, h))
    seq_spec = pl.BlockSpec((1, l, D_V), lambda bi, h, i: (bi, 0, h))
    vt_spec = pl.BlockSpec((1, l // (2 * t), D_V, 2 * t), lambda bi, h, i: (bi, 0, h, 0))
    per_head = lambda rows: pl.BlockSpec((1, rows, t), lambda bi, h, i: (h, 0, 0))
    d = x.shape[2]
    last_head = lambda width: pl.BlockSpec(
        (1, tq, width), lambda bi, h, i: (bi, jnp.where(h == H_ATT - 1, i, 0), 0))
    return pl.pallas_call(
        functools.partial(_prompt_attn_kernel, t=t, n_sub=ATT_SUB, n_q=l // tq),
        name="prompt_attn",
        grid=(b, H_ATT, l // tq),
        in_specs=[
            seq_spec, seq_spec, seq_spec, vt_spec,
            pl.BlockSpec((N_META, D_V), lambda bi, h, i: (0, h)),
            pl.BlockSpec((D_V, N_META), lambda bi, h, i: (h, 0)),
            per_head(2 * t), per_head(N_META),
            tile_spec,
            pl.BlockSpec((1, D_V), lambda bi, h, i: (0, 0)),
            pl.BlockSpec((4, D_QK), lambda bi, h, i: (0, 0)),
            last_head(d), last_head(yc.shape[2]),
            pl.BlockSpec(wo_bf.shape, lambda bi, h, i: (0, 0), pipeline_mode=pl.Buffered(1)),
            pl.BlockSpec((1, d), lambda bi, h, i: (0, 0)),
        ],
        out_specs=last_head(d),
        out_shape=jax.ShapeDtypeStruct((b, l, d), f32),
        scratch_shapes=[pltpu.VMEM((2, 2 * t + N_META, tq), f32),
                        pltpu.VMEM((2, 1, tq), f32),
                        pltpu.VMEM((2, 1, tq), f32), pltpu.VMEM((2, 1, tq), f32),
                        pltpu.VMEM((2, D_V, tq), f32),
                        pltpu.VMEM((H_ATT, l, D_V), bf16)],
        compiler_params=pltpu.CompilerParams(
            dimension_semantics=("arbitrary", "arbitrary", "arbitrary"),
            vmem_limit_bytes=VMEM_LIMIT),
    )(qa, qb, kb, vt, km, vmt, bias_near, bias_meta, gate, subln_g, lam_vecs,
      x, yc, wo_bf, final_g.reshape(1, d))


def _merge(og, yc, x, wo_ref, fg):
    y = x + jnp.dot(jnp.concatenate([og, yc], axis=1), wo_ref[...], preferred_element_type=f32)
    return _rms(y, fg)


def _small_attn_kernel(*refs, has_cache):
    if has_cache:
        (qa_ref, qb_ref, kn_ref, vn_ref, bn_ref, ck_ref, cv_ref, bc_ref,
         gate_ref, yc_ref, x_ref, sg_ref, lam_ref, wo_ref, fg_ref, y_ref) = refs
    else:
        (qa_ref, qb_ref, kn_ref, vn_ref, bn_ref,
         gate_ref, yc_ref, x_ref, sg_ref, lam_ref, wo_ref, fg_ref, y_ref) = refs
    lq = qa_ref.shape[1]
    lam = _diff_lambda(lam_ref)
    nt = (((1,), (1,)), ((), ()))
    both = lambda a: jnp.concatenate([a, a], axis=0)
    heads = []
    for h in range(H_ATT):
        cols = slice(h * D_V, (h + 1) * D_V)
        q2 = jnp.concatenate([qa_ref[0][:, cols], qb_ref[0][:, cols]], axis=0)
        kn = kn_ref[0][:, cols]
        vn = vn_ref[0][:, cols]
        sn = lax.dot_general(q2, kn, nt, preferred_element_type=f32) + both(bn_ref[h])
        m = jnp.max(sn, axis=-1, keepdims=True)
        if has_cache:
            n_rows = ck_ref.shape[1] // H_ATT
            ck = ck_ref[0, pl.ds(h, n_rows, stride=H_ATT), :].astype(bf16)
            cv = cv_ref[0, pl.ds(h, n_rows, stride=H_ATT), :].astype(bf16)
            sc = lax.dot_general(q2, ck, nt, preferred_element_type=f32) + both(bc_ref[h])
            m = jnp.maximum(m, jnp.max(sc, axis=-1, keepdims=True))
        en = jnp.exp2(sn - m)
        l = jnp.sum(en, axis=-1, keepdims=True)
        acc = jnp.dot(en.astype(bf16), vn, preferred_element_type=f32)
        if has_cache:
            ec = jnp.exp2(sc - m)
            l = l + jnp.sum(ec, axis=-1, keepdims=True)
            acc = acc + jnp.dot(ec.astype(bf16), cv, preferred_element_type=f32)
        o = acc / l
        heads.append(_gate_heads(o[:lq], o[lq:], lam, sg_ref[...], gate_ref[0][:, cols]))
    og = jnp.concatenate(heads, axis=-1).astype(bf16)
    y_ref[0] = _merge(og, yc_ref[0], x_ref[0], wo_ref, fg_ref[...])


def _small_attn(qa, qb, kb, vb, bias_new, gate, yc, x, subln_g, lam_vecs, wo_bf, final_g,
                cache_k=None, cache_v=None, bias_cache=None):
    b, lq, d = x.shape
    has_cache = cache_k is not None
    per_b = lambda rows, width: pl.BlockSpec((1, rows, width), lambda i: (i, 0, 0))
    const = lambda shape: pl.BlockSpec(shape, lambda i: (0,) * len(shape))
    w = qa.shape[2]
    args = [qa, qb, kb, vb, bias_new]
    specs = [per_b(lq, w)] * 4 + [const(bias_new.shape)]
    if has_cache:
        args += [cache_k, cache_v, bias_cache]
        specs += [per_b(*cache_k.shape[1:]), per_b(*cache_v.shape[1:]), const(bias_cache.shape)]
    args += [gate, yc, x, subln_g, lam_vecs, wo_bf, final_g.reshape(1, d)]
    specs += [per_b(lq, w), per_b(lq, yc.shape[2]), per_b(lq, d), const(subln_g.shape),
              const(lam_vecs.shape), const(wo_bf.shape), const((1, d))]
    return pl.pallas_call(
        functools.partial(_small_attn_kernel, has_cache=has_cache),
        name="small_attn",
        grid=(b,),
        in_specs=specs,
        out_specs=per_b(lq, d),
        out_shape=jax.ShapeDtypeStruct((b, lq, d), f32),
        compiler_params=pltpu.CompilerParams(
            dimension_semantics=("arbitrary",), vmem_limit_bytes=VMEM_LIMIT),
    )(*args)


def kernel(x_prompt, x_sample, cache_k, cache_v, state_conv, meta_tokens, rel_bias, norm_g, w_in,
           conv_w, lambda_q1, lambda_k1, lambda_q2, lambda_k2, subln_g, w_out, final_g):
    bp, sp, d = x_prompt.shape
    bd, sd, _ = x_sample.shape
    depth = w_in.shape[0]
    assert depth == 1, "single-layer step only"
    past = cache_k.shape[2] - N_META
    t = ATT_TILE
    assert sp % (ATT_SUB * t) == 0 and t % CHUNK == 0

    w_bf = w_in[0].astype(bf16)
    wo_bf = w_out[0].astype(bf16)
    lam_vecs = jnp.stack([lambda_q1[0], lambda_k1[0], lambda_q2[0], lambda_k2[0]])
    sg = subln_g[0].reshape(1, D_V)
    wg = w_bf.shape[1] // N_SPLIT

    zeros_state = jnp.zeros((1, 1, CONV_W - 1, wg), f32)
    (qma, qmb, km, kmb, vm, vmb, gm, ycm, cm) = _in_proj(
        meta_tokens[None], zeros_state, norm_g[0], w_bf, conv_w[0], tm=N_META)
    (qpa, qpb, kp, kpb, vp, vpt, gp, ycp, cp) = _in_proj(
        x_prompt, jnp.broadcast_to(cm, (bp, 1, CONV_W - 1, wg)), norm_g[0], w_bf, conv_w[0],
        tm=PROJ_ROWS, row_offset=N_META, transpose_v=2 * t)
    kp, vp = _fill_meta_rows(km[0], vm[0], kp, vp)
    per_stream = lambda a: a.reshape(bd, sd, a.shape[-1])
    (qda, qdb, kd, kdb, vd, vdb, gd, ycd, cd) = _in_proj(
        x_sample.reshape(1, bd * sd, d), state_conv[0][None], norm_g[0], w_bf, conv_w[0],
        tm=bd * sd)
    qda, qdb, kdb, vdb, gd, ycd = map(per_stream, (qda, qdb, kdb, vdb, gd, ycd))

    meta_pos = np.arange(-N_META, 0)
    tile_pos = np.arange(t)
    d_qpos = past + np.arange(sd)
    cache_pos = np.concatenate([meta_pos, np.arange(past)])
    (bias_near, bias_meta, bias_meta_self, bias_dec_new, bias_dec_cache) = (
        _build_biases([
            (tile_pos + t, np.arange(0, 2 * t), True),
            (tile_pos, meta_pos, True),
            (meta_pos, meta_pos, False),
            (d_qpos, d_qpos, False),
            (d_qpos, cache_pos, False),
        ], rel_bias))
    for off in range(2, sp // t):
        _assert_far_past(tile_pos + off * t, tile_pos)
    for off in range(1, sp // t):
        _assert_far_past(tile_pos + off * t, meta_pos)

    y_prompt = _prompt_attn(qpa, qpb, kpb, vpt, kmb[0], vmb[0].T, bias_near, bias_meta, gp, sg,
                            lam_vecs, x_prompt, ycp, wo_bf, final_g)

    _small_attn(qma, qmb, kmb, vmb, bias_meta_self, gm, ycm, meta_tokens[None], sg, lam_vecs,
                wo_bf, final_g)

    y_sample = _small_attn(
        qda, qdb, kdb, vdb, bias_dec_new, gd, ycd, x_sample, sg, lam_vecs, wo_bf, final_g,
        cache_k=cache_k[0].reshape(bd, (N_META + past) * H_ATT, D_V),
        cache_v=cache_v[0].reshape(bd, (N_META + past) * H_ATT, D_V),
        bias_cache=bias_dec_cache)

    return (y_prompt, y_sample,
            kp.reshape(1, bp, N_META + sp, H_ATT, D_V),
            vp.reshape(1, bp, N_META + sp, H_ATT, D_V),
            cp.reshape(1, bp, CONV_W - 1, wg),
            kd.reshape(1, bd, sd, H_ATT, D_V),
            vd.reshape(1, bd, sd, H_ATT, D_V),
            cd)
```

```python
import functools
import math

import numpy as np
import jax
import jax.numpy as jnp
from jax import lax
from jax.experimental import pallas as pl
from jax.experimental.pallas import tpu as pltpu

f32 = jnp.float32
bf16 = jnp.bfloat16

CHUNK = 64
N_META = 16
H_ATT = 4
D_QK = 64
D_V = 2 * D_QK
W_ATT = H_ATT * D_V
CONV_W = 3
N_SPLIT = 8
N_BUCKETS = 32
MAX_DIST = 128
EPS = 1e-6
NEG = -1e30
SCALE = D_QK ** -0.5
LOG2E = math.log2(math.e)
LAM_INIT = 0.8 - 0.6 * math.exp(-0.3 * 0)
FAR_BUCKET = N_BUCKETS // 2 - 1

ATT_TILE = 256
ATT_SUB = 4
VMEM_LIMIT = 48 * 1024 * 1024
PROJ_ROWS = 1024


def _silu(z):
    return z * (1.0 / (1.0 + jnp.exp(-z)))


def _rms(x, g):
    return x * lax.rsqrt(jnp.mean(x * x, axis=-1, keepdims=True) + EPS) * g


def _in_proj_kernel(x_ref, cinit_ref, g_ref, w_ref, cw_ref,
                    qa_ref, qb_ref, k_ref, kb_ref, v_ref, vb_ref, gate_ref, yc_ref, clast_ref,
                    carry_ref, *, tm, wg, n_seg, transpose_v):
    j = pl.program_id(1)

    @pl.when(j == 0)
    def _():
        carry_ref[...] = cinit_ref[0]

    x = x_ref[0]
    h = _rms(x, g_ref[...]).astype(bf16)

    def proj(g):
        return jnp.dot(h, w_ref[:, g * wg:(g + 1) * wg], preferred_element_type=f32)

    def store_heads(ref, a):
        for hd in range(H_ATT):
            ref[pl.ds(hd, tm, stride=H_ATT), :] = a[:, hd * D_V:(hd + 1) * D_V]

    q = (proj(0) * (SCALE * LOG2E)).astype(bf16)
    first_half = lax.broadcasted_iota(jnp.int32, q.shape, 1) % D_V < D_QK
    zero = jnp.zeros_like(q)
    qa_ref[0] = jnp.where(first_half, q, zero)
    qb_ref[0] = jnp.where(first_half, zero, q)
    k = proj(1)
    store_heads(k_ref, k)
    kb_ref[0] = k.astype(bf16)
    v = proj(2)
    store_heads(v_ref, v)
    if transpose_v:
        for ch in range(tm // transpose_v):
            vb_ref[ch] = v[ch * transpose_v:(ch + 1) * transpose_v].T.astype(bf16)
    else:
        vb_ref[...] = v.astype(bf16)
    gate_ref[0] = _silu(proj(3))

    cu = proj(5) * proj(6)
    prev = carry_ref[...]
    seg = tm // n_seg
    row = lax.broadcasted_iota(jnp.int32, cu.shape, 0) % seg

    def before(k):
        if n_seg == 1:
            return prev[0, k:k + 1]
        return jnp.concatenate(
            [jnp.broadcast_to(prev[sg, k:k + 1], (seg, wg)) for sg in range(n_seg)], axis=0)

    cu_m1 = jnp.where(row == 0, before(1), pltpu.roll(cu, 1, axis=0))
    cu_m2 = jnp.where(row == 0, before(0),
                      jnp.where(row == 1, before(1), pltpu.roll(cu, 2, axis=0)))
    cw = cw_ref[...]
    conv = cw[0:1] * cu_m2 + cw[1:2] * cu_m1 + cw[2:3] * cu
    yc_ref[0] = (_silu(proj(7)) * proj(4) * conv).astype(bf16)

    for sg in range(n_seg):
        last = cu[(sg + 1) * seg - 2:(sg + 1) * seg]
        carry_ref[sg] = last
        clast_ref[0, sg] = last


def _in_proj(x, conv_init, norm_g, w_bf, conv_w, tm, row_offset=0, transpose_v=0):
    b, l, d = x.shape
    w_all = w_bf.shape[1]
    wg = w_all // N_SPLIT
    assert wg == W_ATT
    nj = l // tm
    n_seg = conv_init.shape[1]
    assert n_seg == 1 or nj == 1
    row_spec = lambda width: pl.BlockSpec((1, tm, width), lambda i, j: (i, j, 0))
    const = lambda shape: pl.BlockSpec(shape, lambda i, j: (0,) * len(shape))
    state_spec = pl.BlockSpec((1, n_seg, CONV_W - 1, wg), lambda i, j: (i, 0, 0, 0))
    if row_offset:
        cache_spec = pl.BlockSpec(
            (None, pl.Element(tm * H_ATT), pl.Element(D_V)),
            lambda i, j: (i, _aligned((row_offset + j * tm) * H_ATT, 8 * H_ATT), 0))
    else:
        cache_spec = pl.BlockSpec((None, tm * H_ATT, D_V), lambda i, j: (i, j, 0))
    if transpose_v:
        assert tm % transpose_v == 0
        vb_spec = pl.BlockSpec((None, tm // transpose_v, wg, transpose_v),
                               lambda i, j: (i, j, 0, 0))
        vb_shape = jax.ShapeDtypeStruct((b, l // transpose_v, wg, transpose_v), bf16)
    else:
        vb_spec = pl.BlockSpec((None, tm, wg), lambda i, j: (i, j, 0))
        vb_shape = jax.ShapeDtypeStruct((b, l, wg), bf16)
    cache_shape = jax.ShapeDtypeStruct((b, (row_offset + l) * H_ATT, D_V), f32)
    out_shape = (
        jax.ShapeDtypeStruct((b, l, wg), bf16),
        jax.ShapeDtypeStruct((b, l, wg), bf16),
        cache_shape,
        jax.ShapeDtypeStruct((b, l, wg), bf16),
        cache_shape,
        vb_shape,
        jax.ShapeDtypeStruct((b, l, wg), f32),
        jax.ShapeDtypeStruct((b, l, wg), bf16),
        jax.ShapeDtypeStruct((b, n_seg, CONV_W - 1, wg), f32),
    )
    return pl.pallas_call(
        functools.partial(_in_proj_kernel, tm=tm, wg=wg, n_seg=n_seg, transpose_v=transpose_v),
        name="in_proj",
        grid=(b, nj),
        in_specs=[row_spec(d), state_spec, const((1, d)),
                  pl.BlockSpec((d, w_all), lambda i, j: (0, 0), pipeline_mode=pl.Buffered(1)),
                  const((CONV_W, wg))],
        out_specs=(row_spec(wg), row_spec(wg), cache_spec, row_spec(wg), cache_spec, vb_spec,
                   row_spec(wg), row_spec(wg), state_spec),
        out_shape=out_shape,
        scratch_shapes=[pltpu.VMEM((n_seg, CONV_W - 1, wg), f32)],
        compiler_params=pltpu.CompilerParams(
            dimension_semantics=("arbitrary", "arbitrary"), vmem_limit_bytes=VMEM_LIMIT),
    )(x, conv_init, norm_g.reshape(1, d), w_bf, conv_w)


def _fill_rows_kernel(km_ref, vm_ref, k_hbm, v_hbm, ko_ref, vo_ref):
    del k_hbm, v_hbm
    ko_ref[0] = km_ref[...]
    vo_ref[0] = vm_ref[...]


def _fill_meta_rows(k_meta, v_meta, k_all, v_all):
    b = k_all.shape[0]
    rows = k_meta.shape[0]
    small = pl.BlockSpec((rows, D_V), lambda i: (0, 0))
    lead = pl.BlockSpec((1, rows, D_V), lambda i: (i, 0, 0))
    whole = pl.BlockSpec(memory_space=pl.ANY)
    return pl.pallas_call(
        _fill_rows_kernel,
        name="fill_meta_rows",
        grid=(b,),
        in_specs=[small, small, whole, whole],
        out_specs=(lead, lead),
        out_shape=(jax.ShapeDtypeStruct(k_all.shape, k_all.dtype),
                   jax.ShapeDtypeStruct(v_all.shape, v_all.dtype)),
        input_output_aliases={2: 0, 3: 1},
    )(k_meta, v_meta, k_all, v_all)


def _rel_bucket(rel):
    nb = N_BUCKETS // 2
    ret = np.where(rel > 0, nb, 0)
    n = np.abs(rel)
    max_exact = nb // 2
    nf = np.maximum(n, 1).astype(np.float32)
    large = max_exact + (np.log(nf / np.float32(max_exact)) / np.float32(math.log(MAX_DIST / max_exact))
                         * np.float32(nb - max_exact)).astype(np.int32)
    large = np.minimum(large, nb - 1)
    return (ret + np.where(n < max_exact, n, large)).astype(np.int32)


def _chunk_id_np(pos):
    return np.where(pos < 0, -1, pos // CHUNK)


def _assert_far_past(q_pos, k_pos):
    rel = k_pos[None, :] - q_pos[:, None]
    nb = N_BUCKETS // 2
    max_exact = nb // 2
    n = np.abs(rel).astype(np.float64)
    large = max_exact + np.log(n / max_exact) / math.log(MAX_DIST / max_exact) * (nb - max_exact)
    assert np.all(rel < 0) and np.all(large >= nb), "tile is not in the saturated bucket"
    assert np.all(_chunk_id_np(k_pos)[None, :] <= _chunk_id_np(q_pos)[:, None])


def _bias_kernel(rb_ref, *refs):
    n = len(refs) // 3
    h = pl.program_id(0)
    far = rb_ref[FAR_BUCKET, h]
    shifted = [rb_ref[b, h] - far for b in range(N_BUCKETS)]
    for j in range(n):
        bucket = refs[2 * j][...]
        acc = jnp.zeros(bucket.shape, f32)
        for b in range(N_BUCKETS):
            acc = jnp.where(bucket == b, shifted[b], acc)
        refs[2 * n + j][0] = jnp.where(refs[2 * j + 1][...] != 0, acc * LOG2E, NEG)


def _build_biases(tiles, rel_bias):
    args, specs, out_specs, out_shapes = [], [], [], []
    for q_pos, k_pos, keys_major in tiles:
        rel = k_pos[None, :] - q_pos[:, None]
        vis = _chunk_id_np(k_pos)[None, :] <= _chunk_id_np(q_pos)[:, None]
        if keys_major:
            rel, vis = rel.T, vis.T
        r, c = rel.shape
        args += [jnp.asarray(_rel_bucket(rel)), jnp.asarray(vis.astype(np.int32))]
        specs += [pl.BlockSpec((r, c), lambda h: (0, 0))] * 2
        out_specs.append(pl.BlockSpec((1, r, c), lambda h: (h, 0, 0)))
        out_shapes.append(jax.ShapeDtypeStruct((H_ATT, r, c), f32))
    return pl.pallas_call(
        _bias_kernel,
        name="bias_tiles",
        grid=(H_ATT,),
        in_specs=[pl.BlockSpec(memory_space=pltpu.SMEM)] + specs,
        out_specs=tuple(out_specs),
        out_shape=tuple(out_shapes),
    )(rel_bias, *args)


def _aligned(x, m):
    return x if isinstance(x, int) else pl.multiple_of(x, m)


def _diff_lambda(lam_ref):
    lv = lam_ref[...]
    s1 = jnp.sum(lv[0:1] * lv[1:2], axis=-1, keepdims=True)
    s2 = jnp.sum(lv[2:3] * lv[3:4], axis=-1, keepdims=True)
    return jnp.exp(s1) - jnp.exp(s2) + LAM_INIT


def _gate_heads(o1, o2, lam, sg, gate):
    o = o1 - lam * o2
    return _rms(o, sg) * (1.0 - LAM_INIT) * gate


def _prompt_attn_kernel(*refs, t, n_sub, n_q):
    x_ref, yc_ref, wo_ref, fg_ref, y_ref = refs[11:16]
    og_sc = refs[-1]
    for k in range(n_q):
        pl.when(pl.program_id(2) == k)(
            functools.partial(_attn_tile, *refs[:11], *refs[16:], i=k, t=t, n_sub=n_sub, n_q=n_q))

    @pl.when(pl.program_id(1) == H_ATT - 1)
    def _():
        for p in range(n_sub):
            sub = pl.ds(pl.multiple_of((pl.program_id(2) * n_sub + p) * t, t), t)
            heads = [og_sc[hd, sub, :] for hd in range(H_ATT)]
            y_ref[0, p * t:(p + 1) * t, :] = _merge(
                jnp.concatenate(heads, axis=1), yc_ref[0, p * t:(p + 1) * t, :],
                x_ref[0, p * t:(p + 1) * t, :], wo_ref, fg_ref[...])


def _attn_tile(qa_ref, qb_ref, k_ref, vt_ref, km_ref, vmt_ref, bn_ref, bm_ref, gate_ref,
               sg_ref, lam_ref, sa_sc, mx_sc, m_sc, l_sc, acc_sc, og_sc, *, i, t, n_sub, n_q):
    t2 = 2 * t
    n_wide = i * (n_sub // 2)
    slots = [(c, p) for c in range(2) for p in range(n_sub)]

    def keys(start, rows):
        return k_ref[0, pl.ds(_aligned(start, t), rows), :]

    def cols(p):
        return slice(p * t, (p + 1) * t)

    def qk(kt, j, c, p):
        rows = pl.ds(_aligned((j * n_sub + p) * t, t), t)
        return lax.dot_general(kt, (qa_ref, qb_ref)[c][0, rows, :], (((1,), (1,)), ((), ())),
                               preferred_element_type=f32)

    def col_max(*parts):
        m = jnp.max(parts[0], axis=0, keepdims=True)
        for part in parts[1:]:
            m = jnp.maximum(m, jnp.max(part, axis=0, keepdims=True))
        return m

    def update(c, p, s_parts, vt_parts):
        m_prev = m_sc[c, :, cols(p)]
        m_new = jnp.maximum(m_prev, mx_sc[c, :, cols(p)])
        alpha = jnp.exp2(m_prev - m_new)
        e_sum = pv = None
        for s, vt in zip(s_parts, vt_parts):
            e = jnp.exp2(s - m_new)
            e_k = jnp.sum(e, axis=0, keepdims=True)
            pv_k = jnp.dot(vt, e.astype(bf16), preferred_element_type=f32)
            e_sum = e_k if e_sum is None else e_sum + e_k
            pv = pv_k if pv is None else pv + pv_k
        l_sc[c, :, cols(p)] = alpha * l_sc[c, :, cols(p)] + e_sum
        acc_sc[c, :, cols(p)] = alpha * acc_sc[c, :, cols(p)] + pv
        m_sc[c, :, cols(p)] = m_new

    def near_bias(s, kind):
        if kind == "both":
            return s + bn_ref[0]
        if kind == "own":
            return s + bn_ref[0, t:t2]
        return jnp.concatenate([s[:t], s[t:] + bn_ref[0, 0:t]], axis=0)

    def far_refill(kt, j, c, p, bias=None):
        s = qk(kt, j, c, p)
        if bias is not None:
            s = near_bias(s, bias)
        sa_sc[c, 0:t2, cols(p)] = s
        mx_sc[c, :, cols(p)] = col_max(s)

    def far_update(c, p, vt):
        update(c, p, [sa_sc[c, 0:t2, cols(p)]], [vt])

    def far_tile(w, refill):
        vt = vt_ref[0, w]
        for c, p in slots:
            far_update(c, p, vt)
            refill(c, p)

    def own_a_rows(p):
        return t if p == 0 else t2

    def own_a_refill(j, c, p, meta_bias=None):
        rows = own_a_rows(p)
        s = qk(keys(n_wide * t2, rows), j, c, p)
        if p < 3:
            s = near_bias(s, ("own", "both", "before")[p])
        sm = qk(km_ref[...], j, c, p)
        if meta_bias is not None:
            sm = sm + meta_bias
        sa_sc[c, 0:rows, cols(p)] = s
        sa_sc[c, t2:t2 + N_META, cols(p)] = sm
        mx_sc[c, :, cols(p)] = col_max(s, sm)

    def own_a_update(c, p):
        rows = own_a_rows(p)
        update(c, p, [sa_sc[c, 0:rows, cols(p)], sa_sc[c, t2:t2 + N_META, cols(p)]],
               [vt_ref[0, n_wide, :, 0:rows], vmt_ref[...]])

    def own_b_rows(p):
        return t if p == 2 else t2

    def own_b_refill(j, c, p):
        rows = own_b_rows(p)
        s = near_bias(qk(keys((n_wide + 1) * t2, rows), j, c, p), ("own", "both")[p - 2])
        sa_sc[c, 0:rows, cols(p)] = s
        mx_sc[c, :, cols(p)] = col_max(s)

    def own_b_update(c, p):
        rows = own_b_rows(p)
        update(c, p, [sa_sc[c, 0:rows, cols(p)]], [vt_ref[0, n_wide + 1, :, 0:rows]])

    m_sc[...] = jnp.full(m_sc.shape, NEG, f32)
    l_sc[...] = jnp.zeros(l_sc.shape, f32)
    acc_sc[...] = jnp.zeros(acc_sc.shape, f32)

    def far_pair(pp):
        w = 2 * pp
        kt1, kt2 = keys((w + 1) * t2, t2), keys((w + 2) * t2, t2)
        far_tile(w, lambda c, p: far_refill(kt1, i, c, p))
        far_tile(w + 1, lambda c, p: far_refill(kt2, i, c, p))

    def own_keys_and_finish(look_ahead):
        kt0 = keys(0, t2)
        for c, p in slots:
            own_a_update(c, p)
            if p >= 2:
                own_b_refill(i, c, p)
            elif look_ahead:
                far_refill(kt0, i + 1, c, p)
        for c, p in slots:
            if p >= 2:
                own_b_update(c, p)
                if look_ahead:
                    far_refill(kt0, i + 1, c, p)
        o = acc_sc[0] / l_sc[0] - _diff_lambda(lam_ref) * (acc_sc[1] / l_sc[1])
        o = o * lax.rsqrt(jnp.mean(o * o, axis=0, keepdims=True) + EPS)
        og = (o.T * (sg_ref[...] * (1.0 - LAM_INIT)) * gate_ref[0]).astype(bf16)
        og_sc[pl.program_id(1), pl.ds(i * (n_sub * t), n_sub * t), :] = og

    if i == 0:
        for c, p in slots:
            own_a_refill(i, c, p, bm_ref[0] if p == 0 else None)
    else:
        for pp in range(i - 1):
            far_pair(pp)
        kt1 = keys((n_wide - 1) * t2, t2)
        far_tile(n_wide - 2,
                 lambda c, p: far_refill(kt1, i, c, p, "before" if p == 0 else None))
        far_tile(n_wide - 1, lambda c, p: own_a_refill(i, c, p))
    own_keys_and_finish(look_ahead=i < n_q - 1)


def _prompt_attn(qa, qb, kb, vt, km, vmt, bias_near, bias_meta, gate, subln_g, lam_vecs,
                 x, yc, wo_bf, final_g):
    b, l, _ = qa.shape
    t = ATT_TILE
    tq = ATT_SUB * t
    assert ATT_SUB == 4 and l % tq == 0
    assert vt.shape == (b, l // (2 * t), W_ATT, 2 * t)
    tile_spec = pl.BlockSpec((1, tq, D_V), lambda bi, h, i: (bi, i, h))
    seq_spec = pl.BlockSpec((1, l, D_V), lambda bi, h, i: (bi, 0, h))
    vt_spec = pl.BlockSpec((1, l // (2 * t), D_V, 2 * t), lambda bi, h, i: (bi, 0, h, 0))
    per_head = lambda rows: pl.BlockSpec((1, rows, t), lambda bi, h, i: (h, 0, 0))
    d = x.shape[2]
    last_head = lambda width: pl.BlockSpec(
        (1, tq, width), lambda bi, h, i: (bi, jnp.where(h == H_ATT - 1, i, 0), 0))
    return pl.pallas_call(
        functools.partial(_prompt_attn_kernel, t=t, n_sub=ATT_SUB, n_q=l // tq),
        name="prompt_attn",
        grid=(b, H_ATT, l // tq),
        in_specs=[
            seq_spec, seq_spec, seq_spec, vt_spec,
            pl.BlockSpec((N_META, D_V), lambda bi, h, i: (0, h)),
            pl.BlockSpec((D_V, N_META), lambda bi, h, i: (h, 0)),
            per_head(2 * t), per_head(N_META),
            tile_spec,
            pl.BlockSpec((1, D_V), lambda bi, h, i: (0, 0)),
            pl.BlockSpec((4, D_QK), lambda bi, h, i: (0, 0)),
            last_head(d), last_head(yc.shape[2]),
            pl.BlockSpec(wo_bf.shape, lambda bi, h, i: (0, 0), pipeline_mode=pl.Buffered(1)),
            pl.BlockSpec((1, d), lambda bi, h, i: (0, 0)),
        ],
        out_specs=last_head(d),
        out_shape=jax.ShapeDtypeStruct((b, l, d), f32),
        scratch_shapes=[pltpu.VMEM((2, 2 * t + N_META, tq), f32),
                        pltpu.VMEM((2, 1, tq), f32),
                        pltpu.VMEM((2, 1, tq), f32), pltpu.VMEM((2, 1, tq), f32),
                        pltpu.VMEM((2, D_V, tq), f32),
                        pltpu.VMEM((H_ATT, l, D_V), bf16)],
        compiler_params=pltpu.CompilerParams(
            dimension_semantics=("arbitrary", "arbitrary", "arbitrary"),
            vmem_limit_bytes=VMEM_LIMIT),
    )(qa, qb, kb, vt, km, vmt, bias_near, bias_meta, gate, subln_g, lam_vecs,
      x, yc, wo_bf, final_g.reshape(1, d))


def _merge(og, yc, x, wo_ref, fg):
    y = x + jnp.dot(jnp.concatenate([og, yc], axis=1), wo_ref[...], preferred_element_type=f32)
    return _rms(y, fg)


def _small_attn_kernel(*refs, has_cache):
    if has_cache:
        (qa_ref, qb_ref, kn_ref, vn_ref, bn_ref, ck_ref, cv_ref, bc_ref,
         gate_ref, yc_ref, x_ref, sg_ref, lam_ref, wo_ref, fg_ref, y_ref) = refs
    else:
        (qa_ref, qb_ref, kn_ref, vn_ref, bn_ref,
         gate_ref, yc_ref, x_ref, sg_ref, lam_ref, wo_ref, fg_ref, y_ref) = refs
    lq = qa_ref.shape[1]
    lam = _diff_lambda(lam_ref)
    nt = (((1,), (1,)), ((), ()))
    both = lambda a: jnp.concatenate([a, a], axis=0)
    heads = []
    for h in range(H_ATT):
        cols = slice(h * D_V, (h + 1) * D_V)
        q2 = jnp.concatenate([qa_ref[0][:, cols], qb_ref[0][:, cols]], axis=0)
        kn = kn_ref[0][:, cols]
        vn = vn_ref[0][:, cols]
        sn = lax.dot_general(q2, kn, nt, preferred_element_type=f32) + both(bn_ref[h])
        m = jnp.max(sn, axis=-1, keepdims=True)
        if has_cache:
            n_rows = ck_ref.shape[1] // H_ATT
            ck = ck_ref[0, pl.ds(h, n_rows, stride=H_ATT), :].astype(bf16)
            cv = cv_ref[0, pl.ds(h, n_rows, stride=H_ATT), :].astype(bf16)
            sc = lax.dot_general(q2, ck, nt, preferred_element_type=f32) + both(bc_ref[h])
            m = jnp.maximum(m, jnp.max(sc, axis=-1, keepdims=True))
        en = jnp.exp2(sn - m)
        l = jnp.sum(en, axis=-1, keepdims=True)
        acc = jnp.dot(en.astype(bf16), vn, preferred_element_type=f32)
        if has_cache:
            ec = jnp.exp2(sc - m)
            l = l + jnp.sum(ec, axis=-1, keepdims=True)
            acc = acc + jnp.dot(ec.astype(bf16), cv, preferred_element_type=f32)
        o = acc / l
        heads.append(_gate_heads(o[:lq], o[lq:], lam, sg_ref[...], gate_ref[0][:, cols]))
    og = jnp.concatenate(heads, axis=-1).astype(bf16)
    y_ref[0] = _merge(og, yc_ref[0], x_ref[0], wo_ref, fg_ref[...])


def _small_attn(qa, qb, kb, vb, bias_new, gate, yc, x, subln_g, lam_vecs, wo_bf, final_g,
                cache_k=None, cache_v=None, bias_cache=None):
    b, lq, d = x.shape
    has_cache = cache_k is not None
    per_b = lambda rows, width: pl.BlockSpec((1, rows, width), lambda i: (i, 0, 0))
    const = lambda shape: pl.BlockSpec(shape, lambda i: (0,) * len(shape))
    w = qa.shape[2]
    args = [qa, qb, kb, vb, bias_new]
    specs = [per_b(lq, w)] * 4 + [const(bias_new.shape)]
    if has_cache:
        args += [cache_k, cache_v, bias_cache]
        specs += [per_b(*cache_k.shape[1:]), per_b(*cache_v.shape[1:]), const(bias_cache.shape)]
    args += [gate, yc, x, subln_g, lam_vecs, wo_bf, final_g.reshape(1, d)]
    specs += [per_b(lq, w), per_b(lq, yc.shape[2]), per_b(lq, d), const(subln_g.shape),
              const(lam_vecs.shape), const(wo_bf.shape), const((1, d))]
    return pl.pallas_call(
        functools.partial(_small_attn_kernel, has_cache=has_cache),
        name="small_attn",
        grid=(b,),
        in_specs=specs,
        out_specs=per_b(lq, d),
        out_shape=jax.ShapeDtypeStruct((b, lq, d), f32),
        compiler_params=pltpu.CompilerParams(
            dimension_semantics=("arbitrary",), vmem_limit_bytes=VMEM_LIMIT),
    )(*args)


def kernel(x_prompt, x_sample, cache_k, cache_v, state_conv, meta_tokens, rel_bias, norm_g, w_in,
           conv_w, lambda_q1, lambda_k1, lambda_q2, lambda_k2, subln_g, w_out, final_g):
    bp, sp, d = x_prompt.shape
    bd, sd, _ = x_sample.shape
    depth = w_in.shape[0]
    assert depth == 1, "single-layer step only"
    past = cache_k.shape[2] - N_META
    t = ATT_TILE
    assert sp % (ATT_SUB * t) == 0 and t % CHUNK == 0

    w_bf = w_in[0].astype(bf16)
    wo_bf = w_out[0].astype(bf16)
    lam_vecs = jnp.stack([lambda_q1[0], lambda_k1[0], lambda_q2[0], lambda_k2[0]])
    sg = subln_g[0].reshape(1, D_V)
    wg = w_bf.shape[1] // N_SPLIT

    zeros_state = jnp.zeros((1, 1, CONV_W - 1, wg), f32)
    (qma, qmb, km, kmb, vm, vmb, gm, ycm, cm) = _in_proj(
        meta_tokens[None], zeros_state, norm_g[0], w_bf, conv_w[0], tm=N_META)
    (qpa, qpb, kp, kpb, vp, vpt, gp, ycp, cp) = _in_proj(
        x_prompt, jnp.broadcast_to(cm, (bp, 1, CONV_W - 1, wg)), norm_g[0], w_bf, conv_w[0],
        tm=PROJ_ROWS, row_offset=N_META, transpose_v=2 * t)
    kp, vp = _fill_meta_rows(km[0], vm[0], kp, vp)
    per_stream = lambda a: a.reshape(bd, sd, a.shape[-1])
    (qda, qdb, kd, kdb, vd, vdb, gd, ycd, cd) = _in_proj(
        x_sample.reshape(1, bd * sd, d), state_conv[0][None], norm_g[0], w_bf, conv_w[0],
        tm=bd * sd)
    qda, qdb, kdb, vdb, gd, ycd = map(per_stream, (qda, qdb, kdb, vdb, gd, ycd))

    meta_pos = np.arange(-N_META, 0)
    tile_pos = np.arange(t)
    d_qpos = past + np.arange(sd)
    cache_pos = np.concatenate([meta_pos, np.arange(past)])
    (bias_near, bias_meta, bias_meta_self, bias_dec_new, bias_dec_cache) = (
        _build_biases([
            (tile_pos + t, np.arange(0, 2 * t), True),
            (tile_pos, meta_pos, True),
            (meta_pos, meta_pos, False),
            (d_qpos, d_qpos, False),
            (d_qpos, cache_pos, False),
        ], rel_bias))
    for off in range(2, sp // t):
        _assert_far_past(tile_pos + off * t, tile_pos)
    for off in range(1, sp // t):
        _assert_far_past(tile_pos + off * t, meta_pos)

    y_prompt = _prompt_attn(qpa, qpb, kpb, vpt, kmb[0], vmb[0].T, bias_near, bias_meta, gp, sg,
                            lam_vecs, x_prompt, ycp, wo_bf, final_g)

    _small_attn(qma, qmb, kmb, vmb, bias_meta_self, gm, ycm, meta_tokens[None], sg, lam_vecs,
                wo_bf, final_g)

    y_sample = _small_attn(
        qda, qdb, kdb, vdb, bias_dec_new, gd, ycd, x_sample, sg, lam_vecs, wo_bf, final_g,
        cache_k=cache_k[0].reshape(bd, (N_META + past) * H_ATT, D_V),
        cache_v=cache_v[0].reshape(bd, (N_META + past) * H_ATT, D_V),
        bias_cache=bias_dec_cache)

    return (y_prompt, y_sample,
            kp.reshape(1, bp, N_META + sp, H_ATT, D_V),
            vp.reshape(1, bp, N_META + sp, H_ATT, D_V),
            cp.reshape(1, bp, CONV_W - 1, wg),
            kd.reshape(1, bd, sd, H_ATT, D_V),
            vd.reshape(1, bd, sd, H_ATT, D_V),
            cd)
```
